```python
import jax
import jax.numpy as jnp
from jax import lax
import numpy as np


D_MODEL = 1024
BATCH = 16
SEQ = 4096
DEPTH = 4

CTX_LEN = 256
GRID_W = 64
POOL_WINDOWS = (2, 4, 8, 16)
POOL_GROUP = D_MODEL // 8
POOL_WIDTH = POOL_GROUP * len(POOL_WINDOWS)
RET_HEADS = 4
RET_QK_DIM = D_MODEL // 8
RET_V_DIM = 2 * RET_QK_DIM
RET_QK_WIDTH = RET_HEADS * RET_QK_DIM
RET_V_WIDTH = RET_HEADS * RET_V_DIM
RET_CHUNK = 128
ROPE_BASE = 10000.0
OFF_Q = POOL_WIDTH
OFF_K = OFF_Q + RET_QK_WIDTH
OFF_V = OFF_K + RET_QK_WIDTH
OFF_G = OFF_V + RET_V_WIDTH
OFF_GATE_POOL = OFF_G + RET_V_WIDTH
OFF_GATE_RET = OFF_GATE_POOL + D_MODEL
IN_WIDTH = OFF_GATE_RET + D_MODEL
D_FF = 2816
N_EXPERTS = 8
TOP_K = 2
EXPERT_FF = 3584
MOE_BLOCK = 256
N_DENSE = (DEPTH + 1) // 2
N_MOE = DEPTH // 2
DEEPNORM_ALPHA = (2 * DEPTH) ** 0.25
DEEPNORM_BETA = (8 * DEPTH) ** -0.25
LN_EPS = 1e-5

kernel_name = 'hybrid_pool_retention_moe_dit'


def _layer_norm(x, g, b):
    xf = x.astype(jnp.float32)
    xc = xf - jnp.mean(xf, -1, keepdims=True)
    var = jnp.mean(xc * xc, -1, keepdims=True)
    return (xc * lax.rsqrt(var + LN_EPS) * g + b).astype(x.dtype)


def _box_mean(u, axis, w):
    n = u.shape[axis]
    left = w // 2
    right = w - 1 - left
    cs = jnp.cumsum(u.astype(jnp.float32), axis=axis)
    pad = [(0, 0)] * u.ndim
    pad[axis] = (1, 0)
    cs = jnp.pad(cs, pad)
    t = jnp.arange(n)
    hi = jnp.minimum(t + right + 1, n)
    lo = jnp.maximum(t - left, 0)
    shape = [1] * u.ndim
    shape[axis] = n
    cnt = (hi - lo).astype(jnp.float32).reshape(shape)
    return ((jnp.take(cs, hi, axis=axis) - jnp.take(cs, lo, axis=axis)) / cnt).astype(u.dtype)


def _pool_mixer(u, rows, w_group, scale):
    B, T, _ = u.shape
    ug = u.reshape(B, T, len(POOL_WINDOWS), POOL_GROUP)
    pooled = []
    for gi, w in enumerate(POOL_WINDOWS):
        g = ug[:, :, gi]
        if rows is None:
            p = _box_mean(g, 1, w)
        else:
            gg = g.reshape(B, rows, GRID_W, POOL_GROUP)
            p = _box_mean(_box_mean(gg, 1, w), 2, w).reshape(B, T, POOL_GROUP)
        pooled.append(p)
    d = jnp.stack(pooled, axis=2) - ug
    y = jnp.einsum('btgc,gcd->btgd', d, w_group).reshape(B, T, POOL_WIDTH)
    return y * scale


def _grid_rope(n_tokens):
    t = jnp.arange(n_tokens)
    row = (t // GRID_W).astype(jnp.float32)
    col = (t % GRID_W).astype(jnp.float32)
    n_freq = RET_QK_DIM // 4
    inv = jnp.exp(-jnp.log(ROPE_BASE) * jnp.arange(n_freq, dtype=jnp.float32) / n_freq)
    ang = jnp.concatenate([row[:, None] * inv, col[:, None] * inv], -1)
    return jnp.cos(ang), jnp.sin(ang)


def _rotate(t, cos, sin):
    half = t.shape[-1] // 2
    t1, t2 = t[..., :half], t[..., half:]
    c = cos[None, :, None, :]
    s = sin[None, :, None, :]
    return jnp.concatenate([t1 * c - t2 * s, t1 * s + t2 * c], -1).astype(t.dtype)


def _heads(t, d):
    B, T, _ = t.shape
    return t.reshape(B, T, RET_HEADS, d)


def _seq_flip(t, rev):
    return jnp.flip(t, axis=2) if rev else t


def _chunk_retention(q, k, v, log_gamma, state0):
    B, H, T, _ = q.shape
    dv = v.shape[-1]
    C = RET_CHUNK
    n = T // C
    f32 = jnp.float32
    pos = jnp.arange(C, dtype=f32)
    lg = log_gamma[:, None]
    diff = pos[:, None] - pos[None, :]
    inner = jnp.where(diff >= 0, jnp.exp(lg[:, :, None] * jnp.maximum(diff, 0.0)), 0.0)
    q_dec = jnp.exp(lg * (pos + 1.0))[:, :, None]
    k_dec = jnp.exp(lg * (C - 1.0 - pos))[:, :, None]
    c_dec = jnp.exp(log_gamma * C)[:, None, None]

    def chunks(t):
        return t.astype(f32).reshape(B, H, n, C, t.shape[-1]).transpose(2, 0, 1, 3, 4)

    def step(s, blk):
        qb, kb, vb = blk
        scores = jnp.einsum('bhid,bhjd->bhij', qb, kb) * inner
        o = jnp.einsum('bhij,bhjv->bhiv', scores, vb) + jnp.einsum('bhid,bhdv->bhiv', qb * q_dec, s)
        s = s * c_dec + jnp.einsum('bhjd,bhjv->bhdv', kb * k_dec, vb)
        return s, o

    s, o = lax.scan(step, state0, (chunks(q), chunks(k), chunks(v)))
    return o.transpose(1, 2, 0, 3, 4).reshape(B, H, T, dv), s


def _final_state(k, v, log_gamma):
    T = k.shape[2]
    w = jnp.exp(log_gamma[:, None] * (T - 1.0 - jnp.arange(T, dtype=jnp.float32)))
    return jnp.einsum('bhtd,bhtv->bhdv', k.astype(jnp.float32) * w[None, :, :, None], v.astype(jnp.float32))


def _head_norm(o, dtype):
    oc = o - jnp.mean(o, -1, keepdims=True)
    var = jnp.mean(oc * oc, -1, keepdims=True)
    on = oc * lax.rsqrt(var + LN_EPS)
    B, H, T, dv = o.shape
    return on.transpose(0, 2, 1, 3).reshape(B, T, H * dv).astype(dtype)


def _token_mixer(h, hc, cos, sin, rows, w_in, pool_w, pool_scale, w_pool_out,
                 w_ret_out, decay_logit, w_out, need_ctx):
    B = h.shape[0]
    log_gamma = jax.nn.log_sigmoid(decay_logit.astype(jnp.float32))
    k_scale = RET_QK_DIM ** -0.5

    p = h @ w_in
    q = _rotate(_heads(p[..., OFF_Q:OFF_K], RET_QK_DIM), cos, sin).transpose(0, 2, 1, 3)
    k = (_rotate(_heads(p[..., OFF_K:OFF_V], RET_QK_DIM), cos, sin) * k_scale).transpose(0, 2, 1, 3)
    v = _heads(p[..., OFF_V:OFF_G], RET_V_DIM).transpose(0, 2, 1, 3)
    if need_ctx:
        pc = hc @ w_in
        qc = _heads(pc[..., OFF_Q:OFF_K], RET_QK_DIM).transpose(0, 2, 1, 3)
        kc_src = pc[..., OFF_K:OFF_V]
        vc_src = pc[..., OFF_V:OFF_G]
    else:
        pc_kv = hc @ w_in[:, OFF_K:OFF_G]
        kc_src = pc_kv[..., :RET_QK_WIDTH]
        vc_src = pc_kv[..., RET_QK_WIDTH:]
    kc = (_heads(kc_src, RET_QK_DIM) * k_scale).transpose(0, 2, 1, 3)
    vc = _heads(vc_src, RET_V_DIM).transpose(0, 2, 1, 3)

    zero_state = jnp.zeros((B, RET_HEADS, RET_QK_DIM, RET_V_DIM), jnp.float32)
    o_lat = 0.0
    o_ctx = 0.0
    for d in range(2):
        rev = d == 1
        if need_ctx:
            oc, s = _chunk_retention(_seq_flip(qc, rev), _seq_flip(kc, rev), _seq_flip(vc, rev),
                                     log_gamma[d], zero_state)
            o_ctx = o_ctx + _seq_flip(oc, rev)
        else:
            s = _final_state(_seq_flip(kc, rev), _seq_flip(vc, rev), log_gamma[d])
        ol, _ = _chunk_retention(_seq_flip(q, rev), _seq_flip(k, rev), _seq_flip(v, rev), log_gamma[d], s)
        o_lat = o_lat + _seq_flip(ol, rev)

    def merge(pz, o, grid_rows):
        y_pool = _pool_mixer(pz[..., :OFF_Q], grid_rows, pool_w, pool_scale) @ w_pool_out
        y_ret = (jax.nn.silu(pz[..., OFF_G:OFF_GATE_POOL]) * _head_norm(o, pz.dtype)) @ w_ret_out
        mix = (jax.nn.sigmoid(pz[..., OFF_GATE_POOL:OFF_GATE_RET]) * y_pool
               + jax.nn.sigmoid(pz[..., OFF_GATE_RET:]) * y_ret)
        return mix @ w_out

    y = merge(p, o_lat, rows)
    yc = merge(pc, o_ctx, None) if need_ctx else None
    return y, yc


def _swiglu(t, w1, w3, w2):
    return (jax.nn.silu(t @ w1) * (t @ w3)) @ w2


def _moe_swiglu(t, w_router, w1, w3, w2):
    N, D = t.shape
    logits = (t @ w_router).astype(jnp.float32)
    top_logit, top_idx = lax.top_k(logits, TOP_K)
    top_w = jax.nn.softmax(top_logit, axis=-1)
    n_assign = N * TOP_K
    flat_e = top_idx.reshape(-1).astype(jnp.int32)
    flat_tok = jnp.arange(n_assign, dtype=jnp.int32) // TOP_K
    flat_w = top_w.reshape(-1)
    order = jnp.argsort(flat_e, stable=True)
    se = flat_e[order]
    counts = jax.ops.segment_sum(jnp.ones_like(flat_e), flat_e, num_segments=N_EXPERTS)
    padded = (counts + MOE_BLOCK - 1) // MOE_BLOCK * MOE_BLOCK
    pad_end = jnp.cumsum(padded)
    pad_start = pad_end - padded
    start = jnp.cumsum(counts) - counts
    dest = pad_start[se] + (jnp.arange(n_assign, dtype=jnp.int32) - start[se])
    n_blocks = (n_assign + MOE_BLOCK - 1) // MOE_BLOCK + N_EXPERTS
    n_rows = n_blocks * MOE_BLOCK
    row_tok = jnp.full((n_rows,), N, jnp.int32).at[dest].set(flat_tok[order])
    row_w = jnp.zeros((n_rows,), jnp.float32).at[dest].set(flat_w[order])
    block_start = jnp.arange(n_blocks, dtype=jnp.int32) * MOE_BLOCK
    block_e = jnp.minimum(jnp.searchsorted(pad_end, block_start, side='right'), N_EXPERTS - 1)
    t_pad = jnp.concatenate([t, jnp.zeros((1, D), t.dtype)], 0)

    def expert_block(args):
        tok, e = args
        xb = t_pad[tok]
        return (jax.nn.silu(xb @ w1[e]) * (xb @ w3[e])) @ w2[e]

    y = lax.map(expert_block, (row_tok.reshape(n_blocks, MOE_BLOCK), block_e))
    y = y.reshape(n_rows, D) * row_w[:, None].astype(y.dtype)
    return jax.ops.segment_sum(y, row_tok, num_segments=N + 1)[:N]


def setup_inputs(seed: int = 0) -> dict:
    key = jax.random.key(seed)
    ks = jax.random.split(key, 24)
    f32 = jnp.float32
    D = D_MODEL

    def nrm(k, shape, s):
        return jax.random.normal(k, shape, f32) * s

    base_logit = jnp.log(2.0 ** (5.0 + jnp.arange(RET_HEADS, dtype=f32)) - 1.0)
    return {
        'x': nrm(ks[0], (BATCH, SEQ, D), 1.0),
        'c': nrm(ks[1], (BATCH, D), 1.0),
        'ctx': nrm(ks[2], (BATCH, CTX_LEN, D), 1.0),
        'c_ctx': nrm(ks[3], (D,), 1.0),
        'ada_w': nrm(ks[4], (DEPTH, D, 6 * D), 0.5 * D ** -0.5),
        'ada_b': nrm(ks[5], (DEPTH, 6 * D), 0.01),
        'w_in': nrm(ks[6], (DEPTH, D, IN_WIDTH), D ** -0.5),
        'pool_w': nrm(ks[7], (DEPTH, len(POOL_WINDOWS), POOL_GROUP, POOL_GROUP), POOL_GROUP ** -0.5),
        'pool_scale': 1.0 + nrm(ks[8], (DEPTH, POOL_WIDTH), 0.02),
        'w_pool_out': nrm(ks[9], (DEPTH, POOL_WIDTH, D), POOL_WIDTH ** -0.5),
        'w_ret_out': nrm(ks[10], (DEPTH, RET_V_WIDTH, D), RET_V_WIDTH ** -0.5),
        'ret_decay_logit': base_logit + nrm(ks[11], (DEPTH, 2, RET_HEADS), 0.05),
        'w_out': nrm(ks[12], (DEPTH, D, D), DEEPNORM_BETA * D ** -0.5),
        'ln_mix_g': 1.0 + nrm(ks[13], (DEPTH, D), 0.02),
        'ln_mix_b': nrm(ks[14], (DEPTH, D), 0.01),
        'ln_ffn_g': 1.0 + nrm(ks[15], (DEPTH, D), 0.02),
        'ln_ffn_b': nrm(ks[16], (DEPTH, D), 0.01),
        'ffn_w1': nrm(ks[17], (N_DENSE, D, D_FF), D ** -0.5),
        'ffn_w3': nrm(ks[18], (N_DENSE, D, D_FF), D ** -0.5),
        'ffn_w2': nrm(ks[19], (N_DENSE, D_FF, D), DEEPNORM_BETA * D_FF ** -0.5),
        'moe_router': nrm(ks[20], (N_MOE, D, N_EXPERTS), D ** -0.5),
        'moe_w1': nrm(ks[21], (N_MOE, N_EXPERTS, D, EXPERT_FF), D ** -0.5),
        'moe_w3': nrm(ks[22], (N_MOE, N_EXPERTS, D, EXPERT_FF), D ** -0.5),
        'moe_w2': nrm(ks[23], (N_MOE, N_EXPERTS, EXPERT_FF, D), DEEPNORM_BETA * EXPERT_FF ** -0.5),
    }


def reference(x, c, ctx, c_ctx, ada_w, ada_b, w_in, pool_w, pool_scale, w_pool_out, w_ret_out,
              ret_decay_logit, w_out, ln_mix_g, ln_mix_b, ln_ffn_g, ln_ffn_b, ffn_w1, ffn_w3, ffn_w2,
              moe_router, moe_w1, moe_w3, moe_w2):
    B, L, D = x.shape
    rows = L // GRID_W
    n_ctx = ctx.shape[1]
    n_lat = B * L
    cos, sin = _grid_rope(L)
    s_lat = jax.nn.silu(c)
    s_ctx = jax.nn.silu(c_ctx)
    for l in range(DEPTH):
        need_ctx = l < DEPTH - 1
        mod = jnp.split((s_lat @ ada_w[l] + ada_b[l])[:, None, :], 6, axis=-1)
        mod_c = jnp.split(s_ctx @ ada_w[l] + ada_b[l], 6, axis=-1)

        h = x * (1.0 + mod[1]) + mod[0]
        hc = ctx * (1.0 + mod_c[1]) + mod_c[0]
        y, yc = _token_mixer(h, hc, cos, sin, rows, w_in[l], pool_w[l], pool_scale[l], w_pool_out[l],
                             w_ret_out[l], ret_decay_logit[l], w_out[l], need_ctx)
        x = _layer_norm(DEEPNORM_ALPHA * x + mod[2] * y, ln_mix_g[l], ln_mix_b[l])
        h = x * (1.0 + mod[4]) + mod[3]
        if need_ctx:
            ctx = _layer_norm(DEEPNORM_ALPHA * ctx + mod_c[2] * yc, ln_mix_g[l], ln_mix_b[l])
            hc = ctx * (1.0 + mod_c[4]) + mod_c[3]
            tokens = jnp.concatenate([h.reshape(-1, D), hc.reshape(-1, D)], 0)
        else:
            tokens = h.reshape(-1, D)

        if l % 2 == 0:
            f = _swiglu(tokens, ffn_w1[l // 2], ffn_w3[l // 2], ffn_w2[l // 2])
        else:
            f = _moe_swiglu(tokens, moe_router[l // 2], moe_w1[l // 2], moe_w3[l // 2], moe_w2[l // 2])
        x = _layer_norm(DEEPNORM_ALPHA * x + mod[5] * f[:n_lat].reshape(B, L, D), ln_ffn_g[l], ln_ffn_b[l])
        if need_ctx:
            ctx = _layer_norm(DEEPNORM_ALPHA * ctx + mod_c[5] * f[n_lat:].reshape(B, n_ctx, D),
                              ln_ffn_g[l], ln_ffn_b[l])
    return x
```

```python
import functools

import jax
import jax.numpy as jnp
import numpy as np
from jax import lax
from jax.experimental import pallas as pl
from jax.experimental.pallas import tpu as pltpu

F32 = jnp.float32
BF16 = jnp.bfloat16

D_MODEL = 1024
DEPTH = 4
GRID_W = 64
POOL_WINDOWS = (2, 4, 8, 16)
POOL_GROUP = 128
POOL_WIDTH = POOL_GROUP * len(POOL_WINDOWS)
RET_HEADS = 4
RET_QK_DIM = 128
RET_V_DIM = 256
RET_CHUNK = 128
ROPE_BASE = 10000.0
OFF_Q = POOL_WIDTH
OFF_K = OFF_Q + RET_HEADS * RET_QK_DIM
OFF_V = OFF_K + RET_HEADS * RET_QK_DIM
OFF_G = OFF_V + RET_HEADS * RET_V_DIM
WIDTH_B = 3 * D_MODEL
D_FF = 2816
N_EXPERTS = 8
EXPERT_FF = 3584
DEEPNORM_ALPHA = (2 * DEPTH) ** 0.25
LN_EPS = 1e-5
K_SCALE = RET_QK_DIM ** -0.5

MOD_ROWS = 24
TOKEN_TILE = 512
ROUTE_TILE = 256
EXPERT_ROWS = 512
FFN_CHUNK = D_FF // 2
EXPERT_CHUNK = EXPERT_FF // 2
VMEM_LIMIT = 56 * 1024 * 1024


def _dot(a, b):
    return jnp.dot(a, b, preferred_element_type=F32)


def _sigmoid(x):
    return 1.0 / (1.0 + jnp.exp(-x))


def _split_bf16(a):
    hi = a.astype(BF16)
    lo = (a - hi.astype(F32)).astype(BF16)
    return hi, lo


def _layer_norm(v, g, b):
    mean = jnp.mean(v, axis=-1, keepdims=True)
    vc = v - mean
    var = jnp.mean(vc * vc, axis=-1, keepdims=True)
    return vc * lax.rsqrt(var + LN_EPS) * g + b


def _params(sem, vmem=VMEM_LIMIT):
    return pltpu.CompilerParams(dimension_semantics=sem, vmem_limit_bytes=vmem)


def _mod_kernel(s_ref, w_ref, b_ref, o_ref):
    s = s_ref[...]
    s = s * _sigmoid(s)
    s_hi, s_lo = _split_bf16(s)
    w_hi, w_lo = _split_bf16(w_ref[...])
    o_ref[...] = _dot(s_hi, w_hi) + (_dot(s_hi, w_lo) + _dot(s_lo, w_hi)) + b_ref[...]


def _modulation(s_in, ada_w, ada_b):
    depth, d, width = ada_w.shape
    tn = 1536
    return pl.pallas_call(
        _mod_kernel,
        grid=(depth, width // tn),
        in_specs=[
            pl.BlockSpec((MOD_ROWS, d), lambda l, j: (0, 0)),
            pl.BlockSpec((None, d, tn), lambda l, j: (l, 0, j)),
            pl.BlockSpec((None, 1, tn), lambda l, j: (l, 0, j)),
        ],
        out_specs=pl.BlockSpec((None, MOD_ROWS, tn), lambda l, j: (l, 0, j)),
        out_shape=jax.ShapeDtypeStruct((depth, MOD_ROWS, width), F32),
        compiler_params=_params(("arbitrary", "arbitrary")),
        name="adaln_mod",
    )(s_in, ada_w, ada_b.reshape(depth, 1, width))


def _in_kernel(x_ref, sh_ref, sc_ref, cos_ref, sin_ref, wa_ref, wb_ref, pa_ref, pb_ref):
    h = (x_ref[...] * (1.0 + sc_ref[...]) + sh_ref[...]).astype(BF16)
    cos = cos_ref[...]
    sin = sin_ref[...]
    pa_ref[:, 0:OFF_Q] = _dot(h, wa_ref[:, 0:OFF_Q]).astype(BF16)
    for c0, scale in ((OFF_Q, None), (OFF_K, K_SCALE)):
        acc = _dot(h, wa_ref[:, c0:c0 + 512])
        for hh in range(RET_HEADS):
            t = acc[:, hh * RET_QK_DIM:(hh + 1) * RET_QK_DIM]
            r = t * cos + pltpu.roll(t, RET_QK_DIM // 2, 1) * sin
            if scale is not None:
                r = r * scale
            pa_ref[:, c0 + hh * RET_QK_DIM:c0 + (hh + 1) * RET_QK_DIM] = r.astype(BF16)
    for c0 in range(OFF_V, OFF_G, 512):
        pa_ref[:, c0:c0 + 512] = _dot(h, wa_ref[:, c0:c0 + 512]).astype(BF16)
    for c0 in range(0, WIDTH_B, 512):
        pb_ref[:, c0:c0 + 512] = _dot(h, wb_ref[:, c0:c0 + 512]).astype(BF16)


def _in_proj(x, mod3, cos2, sin2, wa, wb, n_lat, seq, batch):
    n = x.shape[0]
    tm = TOKEN_TILE
    nl = n_lat // tm
    tpb = seq // tm

    def row(i):
        return jnp.where(i < nl, i // tpb, batch)

    def rope(i):
        return jnp.where(i < nl, i % tpb, tpb)

    return pl.pallas_call(
        _in_kernel,
        grid=(n // tm,),
        in_specs=[
            pl.BlockSpec((tm, D_MODEL), lambda i: (i, 0)),
            pl.BlockSpec((None, 1, D_MODEL), lambda i: (row(i), 0, 0)),
            pl.BlockSpec((None, 1, D_MODEL), lambda i: (row(i), 0, 1)),
            pl.BlockSpec((tm, RET_QK_DIM), lambda i: (rope(i), 0)),
            pl.BlockSpec((tm, RET_QK_DIM), lambda i: (rope(i), 0)),
            pl.BlockSpec((D_MODEL, OFF_G), lambda i: (0, 0)),
            pl.BlockSpec((D_MODEL, WIDTH_B), lambda i: (0, 0)),
        ],
        out_specs=[
            pl.BlockSpec((tm, OFF_G), lambda i: (i, 0)),
            pl.BlockSpec((tm, WIDTH_B), lambda i: (i, 0)),
        ],
        out_shape=[
            jax.ShapeDtypeStruct((n, OFF_G), BF16),
            jax.ShapeDtypeStruct((n, WIDTH_B), BF16),
        ],
        compiler_params=_params(("arbitrary",)),
        name="in_proj",
    )(x, mod3, mod3, cos2, sin2, wa, wb)


POOL_PAD_ROWS = 8
POOL_ROW_CHUNK = 8


def _pool_kernel(u_ref, mc_ref, inv_ref, pw_ref, ps_ref, o_ref, zp_ref, *, rows):
    seq = rows * GRID_W
    pad = POOL_PAD_ROWS * GRID_W
    chunk = POOL_ROW_CHUNK * GRID_W
    zeros = jnp.zeros((pad, POOL_GROUP), F32)
    for g, w in enumerate(POOL_WINDOWS):
        lanes = slice(g * POOL_GROUP, (g + 1) * POOL_GROUP)
        zp_ref[0:pad, :] = zeros
        zp_ref[pad + seq:pad + seq + pad, :] = zeros
        for c in range(seq // 256):
            zp_ref[pad + c * 256:pad + (c + 1) * 256, :] = _dot(mc_ref[g], u_ref[c * 256:(c + 1) * 256, lanes])
        left = w // 2

        def body(rc, carry, g=g, w=w, left=left, lanes=lanes):
            tok = pl.multiple_of(rc * chunk, chunk)
            acc = zp_ref[pl.ds(tok + pad - left * GRID_W, chunk), :]
            for k in range(1, w):
                acc = acc + zp_ref[pl.ds(tok + pad + (k - left) * GRID_W, chunk), :]
            pooled = acc * inv_ref[g, pl.ds(tok, chunk), :]
            d = (pooled - u_ref[pl.ds(tok, chunk), lanes].astype(F32)).astype(BF16)
            y = _dot(d, pw_ref[g]) * ps_ref[:, lanes]
            o_ref[pl.ds(tok, chunk), lanes] = y.astype(BF16)
            return carry

        lax.fori_loop(0, rows // POOL_ROW_CHUNK, body, 0)


def _pool_lat(pa, mc, inv, pw, ps, batch, seq):
    rows = seq // GRID_W
    return pl.pallas_call(
        functools.partial(_pool_kernel, rows=rows),
        grid=(batch,),
        in_specs=[
            pl.BlockSpec((seq, POOL_WIDTH), lambda b: (b, 0)),
            pl.BlockSpec((4, 256, 256), lambda b: (0, 0, 0)),
            pl.BlockSpec((4, seq, POOL_GROUP), lambda b: (0, 0, 0)),
            pl.BlockSpec((4, POOL_GROUP, POOL_GROUP), lambda b: (0, 0, 0)),
            pl.BlockSpec((1, POOL_WIDTH), lambda b: (0, 0)),
        ],
        out_specs=pl.BlockSpec((seq, POOL_WIDTH), lambda b: (b, 0)),
        out_shape=jax.ShapeDtypeStruct((batch * seq, POOL_WIDTH), BF16),
        scratch_shapes=[pltpu.VMEM(((rows + 2 * POOL_PAD_ROWS) * GRID_W, POOL_GROUP), F32)],
        compiler_params=_params(("arbitrary",)),
        name="pool_lat",
    )(pa, mc, inv, pw, ps)


def _pool_ctx_kernel(u_ref, m_ref, inv_ref, pw_ref, ps_ref, o_ref):
    for g in range(len(POOL_WINDOWS)):
        lanes = slice(g * POOL_GROUP, (g + 1) * POOL_GROUP)
        ug = u_ref[:, lanes]
        pooled = _dot(m_ref[g], ug) * inv_ref[g]
        d = (pooled - ug.astype(F32)).astype(BF16)
        o_ref[:, lanes] = (_dot(d, pw_ref[g]) * ps_ref[:, lanes]).astype(BF16)


def _pool_ctx(pa, m1d, inv1d, pw, ps, batch, n_lat, n_ctx):
    first = n_lat // n_ctx
    return pl.pallas_call(
        _pool_ctx_kernel,
        grid=(batch,),
        in_specs=[
            pl.BlockSpec((n_ctx, POOL_WIDTH), lambda b: (first + b, 0)),
            pl.BlockSpec((4, n_ctx, n_ctx), lambda b: (0, 0, 0)),
            pl.BlockSpec((4, n_ctx, POOL_GROUP), lambda b: (0, 0, 0)),
            pl.BlockSpec((4, POOL_GROUP, POOL_GROUP), lambda b: (0, 0, 0)),
            pl.BlockSpec((1, POOL_WIDTH), lambda b: (0, 0)),
        ],
        out_specs=pl.BlockSpec((n_ctx, POOL_WIDTH), lambda b: (b, 0)),
        out_shape=jax.ShapeDtypeStruct((batch * n_ctx, POOL_WIDTH), BF16),
        compiler_params=_params(("arbitrary",)),
        name="pool_ctx",
    )(pa, m1d, inv1d, pw, ps)


def _log_sigmoid(x):
    return jnp.minimum(x, 0.0) - jnp.log1p(jnp.exp(-jnp.abs(x)))


def _decays(dl, backward):
    c = RET_CHUNK
    lg = _log_sigmoid(dl)
    lgq = lg[:, :RET_QK_DIM]
    ii = lax.broadcasted_iota(jnp.int32, (c, c), 0)
    jj = lax.broadcasted_iota(jnp.int32, (c, c), 1)
    pos = lax.broadcasted_iota(jnp.int32, (c, RET_QK_DIM), 0).astype(F32)
    if backward:
        diff = (jj - ii).astype(F32)
        qdec = jnp.exp(lgq * (c - pos))
        kdec = jnp.exp(lgq * pos)
    else:
        diff = (ii - jj).astype(F32)
        qdec = jnp.exp(lgq * (pos + 1.0))
        kdec = jnp.exp(lgq * (c - 1.0 - pos))
    inner = jnp.where(diff >= 0, jnp.exp(lgq * jnp.maximum(diff, 0.0)), 0.0)
    cdec = jnp.exp(lg * float(c))
    return inner, qdec, kdec, cdec


def _ret_step(qc, kc, vc, s, dec):
    inner, qdec, kdec, cdec = dec
    scores = lax.dot_general(qc, kc, (((1,), (1,)), ((), ())), preferred_element_type=F32) * inner
    o = _dot(scores.astype(BF16), vc) + _dot((qc.astype(F32) * qdec).astype(BF16), s.astype(BF16))
    kd = (kc.astype(F32) * kdec).astype(BF16)
    s_new = s * cdec + lax.dot_general(kd, vc, (((0,), (0,)), ((), ())), preferred_element_type=F32)
    return o, s_new


def _head_norm_gate(o, g):
    mean = jnp.mean(o, axis=-1, keepdims=True)
    oc = o - mean
    var = jnp.mean(oc * oc, axis=-1, keepdims=True)
    on = oc * lax.rsqrt(var + LN_EPS)
    gf = g.astype(F32)
    return (gf * _sigmoid(gf) * on).astype(BF16)


def _ret_kernel(q_ref, k_ref, v_ref, g_ref, qc_ref, kc_ref, vc_ref, gc_ref, dl_ref,
                z_ref, zc_ref, of_ref, ob_ref, oc_ref, s_ref, *, n_chunks, n_ctx_chunks):
    c = RET_CHUNK
    dec_f = _decays(dl_ref[0], False)
    dec_b = _decays(dl_ref[1], True)

    s = jnp.zeros((RET_QK_DIM, RET_V_DIM), F32)
    for j in range(n_ctx_chunks):
        rows = slice(j * c, (j + 1) * c)
        o, s = _ret_step(qc_ref[rows, :], kc_ref[rows, :], vc_ref[rows, :], s, dec_f)
        oc_ref[rows, :] = o
    s_ref[0] = s
    s = jnp.zeros((RET_QK_DIM, RET_V_DIM), F32)
    for j in reversed(range(n_ctx_chunks)):
        rows = slice(j * c, (j + 1) * c)
        o, s = _ret_step(qc_ref[rows, :], kc_ref[rows, :], vc_ref[rows, :], s, dec_b)
        oc_ref[rows, :] = oc_ref[rows, :] + o
    s_ref[1] = s
    for j in range(n_ctx_chunks):
        rows = slice(j * c, (j + 1) * c)
        zc_ref[rows, :] = _head_norm_gate(oc_ref[rows, :], gc_ref[rows, :])

    def scan(t, carry):
        rf = pl.ds(pl.multiple_of(t * c, c), c)
        o, s_new = _ret_step(q_ref[rf, :], k_ref[rf, :], v_ref[rf, :], s_ref[0], dec_f)
        of_ref[rf, :] = o
        s_ref[0] = s_new
        rb = pl.ds(pl.multiple_of((n_chunks - 1 - t) * c, c), c)
        o, s_new = _ret_step(q_ref[rb, :], k_ref[rb, :], v_ref[rb, :], s_ref[1], dec_b)
        ob_ref[rb, :] = o
        s_ref[1] = s_new
        return carry

    lax.fori_loop(0, n_chunks, scan, 0)

    def finish(t, carry):
        r = pl.ds(pl.multiple_of(t * c, c), c)
        z_ref[r, :] = _head_norm_gate(of_ref[r, :] + ob_ref[r, :], g_ref[r, :])
        return carry

    lax.fori_loop(0, n_chunks, finish, 0)


def _retention(pa, pb, dl, batch, seq, n_ctx):
    n_lat = batch * seq
    first = n_lat // n_ctx
    qk0 = OFF_Q // RET_QK_DIM
    kk0 = OFF_K // RET_QK_DIM
    v0 = OFF_V // RET_V_DIM
    return pl.pallas_call(
        functools.partial(_ret_kernel, n_chunks=seq // RET_CHUNK, n_ctx_chunks=n_ctx // RET_CHUNK),
        grid=(batch, RET_HEADS),
        in_specs=[
            pl.BlockSpec((seq, RET_QK_DIM), lambda b, h: (b, qk0 + h)),
            pl.BlockSpec((seq, RET_QK_DIM), lambda b, h: (b, kk0 + h)),
            pl.BlockSpec((seq, RET_V_DIM), lambda b, h: (b, v0 + h)),
            pl.BlockSpec((seq, RET_V_DIM), lambda b, h: (b, h)),
            pl.BlockSpec((n_ctx, RET_QK_DIM), lambda b, h: (first + b, qk0 + h)),
            pl.BlockSpec((n_ctx, RET_QK_DIM), lambda b, h: (first + b, kk0 + h)),
            pl.BlockSpec((n_ctx, RET_V_DIM), lambda b, h: (first + b, v0 + h)),
            pl.BlockSpec((n_ctx, RET_V_DIM), lambda b, h: (first + b, h)),
            pl.BlockSpec((2, None, RET_CHUNK, RET_V_DIM), lambda b, h: (0, h, 0, 0)),
        ],
        out_specs=[
            pl.BlockSpec((seq, RET_V_DIM), lambda b, h: (b, h)),
            pl.BlockSpec((n_ctx, RET_V_DIM), lambda b, h: (b, h)),
        ],
        out_shape=[
            jax.ShapeDtypeStruct((n_lat, RET_HEADS * RET_V_DIM), BF16),
            jax.ShapeDtypeStruct((batch * n_ctx, RET_HEADS * RET_V_DIM), BF16),
        ],
        scratch_shapes=[
            pltpu.VMEM((seq, RET_V_DIM), F32),
            pltpu.VMEM((seq, RET_V_DIM), F32),
            pltpu.VMEM((n_ctx, RET_V_DIM), F32),
            pltpu.VMEM((2, RET_QK_DIM, RET_V_DIM), F32),
        ],
        compiler_params=_params(("arbitrary", "arbitrary")),
        name="retention",
    )(pa, pa, pa, pb, pa, pa, pa, pb, dl)


def _merge_kernel(x_ref, gm_ref, sh_ref, sc_ref, ypl_ref, ypc_ref, zl_ref, zc_ref, gp_ref, gr_ref,
                  wp_ref, wr_ref, wo_ref, lg_ref, lb_ref, x1_ref, t_ref, *, n_lat_tiles):
    is_lat = pl.program_id(0) < n_lat_tiles
    yp = jnp.where(is_lat, ypl_ref[...], ypc_ref[...])
    z = jnp.where(is_lat, zl_ref[...], zc_ref[...])
    y_pool = _dot(yp, wp_ref[...])
    y_ret = _dot(z, wr_ref[...])
    mix = _sigmoid(gp_ref[...].astype(F32)) * y_pool + _sigmoid(gr_ref[...].astype(F32)) * y_ret
    y = _dot(mix.astype(BF16), wo_ref[...])
    x1 = _layer_norm(DEEPNORM_ALPHA * x_ref[...] + gm_ref[...] * y, lg_ref[...], lb_ref[...])
    x1_ref[...] = x1
    t_ref[...] = (x1 * (1.0 + sc_ref[...]) + sh_ref[...]).astype(BF16)


def _merge(x, mod3, yp_lat, yp_ctx, z_lat, z_ctx, pb, wp, wr, wo, ln_g, ln_b, n_out, n_lat, seq, batch):
    tm = TOKEN_TILE
    nl = n_lat // tm
    tpb = seq // tm

    def row(i):
        return jnp.where(i < nl, i // tpb, batch)

    def lat(i):
        return jnp.minimum(i, nl - 1)

    def ctx(i):
        return jnp.maximum(i - nl, 0)

    return pl.pallas_call(
        functools.partial(_merge_kernel, n_lat_tiles=nl),
        grid=(n_out // tm,),
        in_specs=[
            pl.BlockSpec((tm, D_MODEL), lambda i: (i, 0)),
            pl.BlockSpec((None, 1, D_MODEL), lambda i: (row(i), 0, 2)),
            pl.BlockSpec((None, 1, D_MODEL), lambda i: (row(i), 0, 3)),
            pl.BlockSpec((None, 1, D_MODEL), lambda i: (row(i), 0, 4)),
            pl.BlockSpec((tm, POOL_WIDTH), lambda i: (lat(i), 0)),
            pl.BlockSpec((tm, POOL_WIDTH), lambda i: (ctx(i), 0)),
            pl.BlockSpec((tm, D_MODEL), lambda i: (lat(i), 0)),
            pl.BlockSpec((tm, D_MODEL), lambda i: (ctx(i), 0)),
            pl.BlockSpec((tm, D_MODEL), lambda i: (i, 1)),
            pl.BlockSpec((tm, D_MODEL), lambda i: (i, 2)),
            pl.BlockSpec((POOL_WIDTH, D_MODEL), lambda i: (0, 0)),
            pl.BlockSpec((D_MODEL, D_MODEL), lambda i: (0, 0)),
            pl.BlockSpec((D_MODEL, D_MODEL), lambda i: (0, 0)),
            pl.BlockSpec((1, D_MODEL), lambda i: (0, 0)),
            pl.BlockSpec((1, D_MODEL), lambda i: (0, 0)),
        ],
        out_specs=[
            pl.BlockSpec((tm, D_MODEL), lambda i: (i, 0)),
            pl.BlockSpec((tm, D_MODEL), lambda i: (i, 0)),
        ],
        out_shape=[
            jax.ShapeDtypeStruct((n_out, D_MODEL), F32),
            jax.ShapeDtypeStruct((n_out, D_MODEL), BF16),
        ],
        compiler_params=_params(("arbitrary",)),
        name="merge",
    )(x, mod3, mod3, mod3, yp_lat, yp_ctx, z_lat, z_ctx, pb, pb, wp, wr, wo, ln_g, ln_b)


def _ffn_kernel(t_ref, x1_ref, gm_ref, w1_ref, w3_ref, w2_ref, lg_ref, lb_ref, o_ref, acc_ref):
    f = pl.program_id(1)
    t = t_ref[...]
    a = _dot(t, w1_ref[...])
    hmid = (a * _sigmoid(a) * _dot(t, w3_ref[...])).astype(BF16)
    part = _dot(hmid, w2_ref[...])

    @pl.when(f == 0)
    def _():
        acc_ref[...] = part

    @pl.when(f > 0)
    def _():
        acc_ref[...] = acc_ref[...] + part

    @pl.when(f == pl.num_programs(1) - 1)
    def _():
        o_ref[...] = _layer_norm(DEEPNORM_ALPHA * x1_ref[...] + gm_ref[...] * acc_ref[...],
                                 lg_ref[...], lb_ref[...])


def _ffn(t, x1, mod3, w1, w3, w2, ln_g, ln_b, n_lat, seq, batch):
    n = t.shape[0]
    tm = TOKEN_TILE
    nl = n_lat // tm
    tpb = seq // tm

    def row(i):
        return jnp.where(i < nl, i // tpb, batch)

    return pl.pallas_call(
        _ffn_kernel,
        grid=(n // tm, D_FF // FFN_CHUNK),
        in_specs=[
            pl.BlockSpec((tm, D_MODEL), lambda i, f: (i, 0)),
            pl.BlockSpec((tm, D_MODEL), lambda i, f: (i, 0)),
            pl.BlockSpec((None, 1, D_MODEL), lambda i, f: (row(i), 0, 5)),
            pl.BlockSpec((D_MODEL, FFN_CHUNK), lambda i, f: (0, f)),
            pl.BlockSpec((D_MODEL, FFN_CHUNK), lambda i, f: (0, f)),
            pl.BlockSpec((FFN_CHUNK, D_MODEL), lambda i, f: (f, 0)),
            pl.BlockSpec((1, D_MODEL), lambda i, f: (0, 0)),
            pl.BlockSpec((1, D_MODEL), lambda i, f: (0, 0)),
        ],
        out_specs=pl.BlockSpec((tm, D_MODEL), lambda i, f: (i, 0)),
        out_shape=jax.ShapeDtypeStruct((n, D_MODEL), F32),
        scratch_shapes=[pltpu.VMEM((tm, D_MODEL), F32)],
        compiler_params=_params(("arbitrary", "arbitrary")),
        name="ffn_dense",
    )(t, x1, mod3, w1, w3, w2, ln_g, ln_b)


ROUTE_LANES = 128


def _router_kernel(t_ref, wr_ref, tri_ref, route_ref, before_ref, total_ref, run_ref):
    @pl.when(pl.program_id(0) == 0)
    def _():
        run_ref[...] = jnp.zeros_like(run_ref)

    tt = t_ref.shape[0]
    logits = _dot(t_ref[...], wr_ref[...])
    lane = lax.broadcasted_iota(jnp.int32, (tt, ROUTE_LANES), 1)
    neg = jnp.float32(-jnp.inf)
    lg = jnp.where(lane < N_EXPERTS, logits, neg)
    m1 = jnp.max(lg, axis=1, keepdims=True)
    i1 = jnp.min(jnp.where(lg == m1, lane, ROUTE_LANES), axis=1, keepdims=True)
    lg2 = jnp.where(lane == i1, neg, lg)
    m2 = jnp.max(lg2, axis=1, keepdims=True)
    i2 = jnp.min(jnp.where(lg2 == m2, lane, ROUTE_LANES), axis=1, keepdims=True)
    e = jnp.exp(m2 - m1)
    w1 = 1.0 / (1.0 + e)
    w2 = e / (1.0 + e)
    hit1 = lane == i1
    hit2 = lane == i2
    onehot = jnp.where(hit1 | hit2, 1.0, 0.0)
    run = run_ref[...]
    prefix = _dot(tri_ref[...], onehot.astype(BF16)) + run
    r1 = jnp.sum(jnp.where(hit1, prefix, 0.0), axis=1, keepdims=True)
    r2 = jnp.sum(jnp.where(hit2, prefix, 0.0), axis=1, keepdims=True)
    before_ref[...] = run
    run = run + jnp.sum(onehot, axis=0, keepdims=True)
    run_ref[...] = run
    total_ref[...] = run
    out = jnp.where(lane == 0, i1.astype(F32), 0.0)
    out = jnp.where(lane == 1, i2.astype(F32), out)
    out = jnp.where(lane == 2, w1, out)
    out = jnp.where(lane == 3, w2, out)
    out = jnp.where(lane == 4, r1, out)
    out = jnp.where(lane == 5, r2, out)
    route_ref[...] = out


def _router(t, wr, tri):
    n = t.shape[0]
    tt = ROUTE_TILE
    n_tiles = n // tt
    return pl.pallas_call(
        _router_kernel,
        grid=(n_tiles,),
        in_specs=[
            pl.BlockSpec((tt, D_MODEL), lambda i: (i, 0)),
            pl.BlockSpec((D_MODEL, ROUTE_LANES), lambda i: (0, 0)),
            pl.BlockSpec((tt, tt), lambda i: (0, 0)),
        ],
        out_specs=[
            pl.BlockSpec((tt, ROUTE_LANES), lambda i: (i, 0)),
            pl.BlockSpec((None, 1, ROUTE_LANES), lambda i: (i, 0, 0)),
            pl.BlockSpec((1, ROUTE_LANES), lambda i: (0, 0)),
        ],
        out_shape=[
            jax.ShapeDtypeStruct((n, ROUTE_LANES), F32),
            jax.ShapeDtypeStruct((n_tiles, 1, ROUTE_LANES), F32),
            jax.ShapeDtypeStruct((1, ROUTE_LANES), F32),
        ],
        scratch_shapes=[pltpu.VMEM((1, ROUTE_LANES), F32)],
        compiler_params=_params(("arbitrary",)),
        name="moe_router",
    )(t, wr, tri)


def _gather_kernel(pb_ref, ps_ref, pv_ref, pf_ref, t_ref, dst_ref, o_ref):
    i = pl.program_id(0)

    @pl.when(pf_ref[i] == 1)
    def _():
        o_ref[...] = jnp.zeros_like(o_ref)

    @pl.when(pv_ref[i] == 1)
    def _():
        rows = pb_ref[i] * EXPERT_ROWS + lax.broadcasted_iota(jnp.int32, (EXPERT_ROWS, ROUTE_TILE), 0)
        sel = (dst_ref[0:1, :] == rows) | (dst_ref[1:2, :] == rows)
        picked = _dot(jnp.where(sel, 1.0, 0.0).astype(BF16), t_ref[...])
        o_ref[...] = o_ref[...] + picked.astype(BF16)


def _gather(lists, t, dst, n_blocks):
    n_steps = lists[0].shape[0]
    spec = pltpu.PrefetchScalarGridSpec(
        num_scalar_prefetch=4,
        grid=(n_steps,),
        in_specs=[
            pl.BlockSpec((ROUTE_TILE, D_MODEL), lambda i, pb, ps, pv, pf: (ps[i], 0)),
            pl.BlockSpec((None, 2, ROUTE_TILE), lambda i, pb, ps, pv, pf: (ps[i], 0, 0)),
        ],
        out_specs=pl.BlockSpec((EXPERT_ROWS, D_MODEL), lambda i, pb, ps, pv, pf: (pb[i], 0)),
    )
    return pl.pallas_call(
        _gather_kernel,
        grid_spec=spec,
        out_shape=jax.ShapeDtypeStruct((n_blocks * EXPERT_ROWS, D_MODEL), BF16),
        compiler_params=_params(("arbitrary",)),
        name="moe_gather",
    )(*lists, t, dst)


def _expert_kernel(be_ref, bv_ref, x_ref, w1_ref, w3_ref, w2_ref, o_ref, acc_ref):
    b = pl.program_id(0)
    f = pl.program_id(1)
    last = pl.num_programs(1) - 1
    used = bv_ref[b] == 1

    @pl.when(used)
    def _():
        x = x_ref[...]
        a = _dot(x, w1_ref[...])
        hmid = (a * _sigmoid(a) * _dot(x, w3_ref[...])).astype(BF16)
        part = _dot(hmid, w2_ref[...])

        @pl.when(f == 0)
        def _():
            acc_ref[...] = part

        @pl.when(f > 0)
        def _():
            acc_ref[...] = acc_ref[...] + part

        @pl.when(f == last)
        def _():
            o_ref[...] = acc_ref[...].astype(BF16)

    @pl.when(jnp.logical_not(used) & (f == last))
    def _():
        o_ref[...] = jnp.zeros_like(o_ref)


def _experts(block_e, block_used, xs, w1, w3, w2):
    n_blocks = block_e.shape[0]
    spec = pltpu.PrefetchScalarGridSpec(
        num_scalar_prefetch=2,
        grid=(n_blocks, EXPERT_FF // EXPERT_CHUNK),
        in_specs=[
            pl.BlockSpec((EXPERT_ROWS, D_MODEL), lambda b, f, be, bv: (b, 0)),
            pl.BlockSpec((None, D_MODEL, EXPERT_CHUNK), lambda b, f, be, bv: (be[b], 0, f)),
            pl.BlockSpec((None, D_MODEL, EXPERT_CHUNK), lambda b, f, be, bv: (be[b], 0, f)),
            pl.BlockSpec((None, EXPERT_CHUNK, D_MODEL), lambda b, f, be, bv: (be[b], f, 0)),
        ],
        out_specs=pl.BlockSpec((EXPERT_ROWS, D_MODEL), lambda b, f, be, bv: (b, 0)),
        scratch_shapes=[pltpu.VMEM((EXPERT_ROWS, D_MODEL), F32)],
    )
    return pl.pallas_call(
        _expert_kernel,
        grid_spec=spec,
        out_shape=jax.ShapeDtypeStruct(xs.shape, BF16),
        compiler_params=_params(("arbitrary", "arbitrary")),
        name="moe_experts",
    )(block_e, block_used, xs, w1, w3, w2)


def _combine_kernel(pb_ref, ps_ref, pv_ref, pf_ref, pl_ref, y_ref, dw_ref, x1_ref, gm_ref, lg_ref, lb_ref,
                    o_ref, acc_ref):
    i = pl.program_id(0)

    @pl.when(pf_ref[i] == 1)
    def _():
        acc_ref[...] = jnp.zeros_like(acc_ref)

    @pl.when(pv_ref[i] == 1)
    def _():
        base = pb_ref[i] * EXPERT_ROWS
        d1 = dw_ref[:, 0:1].astype(jnp.int32) - base
        d2 = dw_ref[:, 1:2].astype(jnp.int32) - base
        col = lax.broadcasted_iota(jnp.int32, (ROUTE_TILE, EXPERT_ROWS), 1)
        sel = (d1 == col) | (d2 == col)
        in1 = (d1 >= 0) & (d1 < EXPERT_ROWS)
        in2 = (d2 >= 0) & (d2 < EXPERT_ROWS)
        wsel = jnp.where(in1, dw_ref[:, 2:3], 0.0) + jnp.where(in2, dw_ref[:, 3:4], 0.0)
        acc_ref[...] = acc_ref[...] + wsel * _dot(jnp.where(sel, 1.0, 0.0).astype(BF16), y_ref[...])

    @pl.when(pl_ref[i] == 1)
    def _():
        o_ref[...] = _layer_norm(DEEPNORM_ALPHA * x1_ref[...] + gm_ref[...] * acc_ref[...],
                                 lg_ref[...], lb_ref[...])


def _combine(lists, y, dw, x1, mod3, ln_g, ln_b, n_lat, seq, batch):
    n = x1.shape[0]
    n_steps = lists[0].shape[0]
    nl = n_lat // ROUTE_TILE
    tpb = seq // ROUTE_TILE

    def row(s):
        return jnp.where(s < nl, s // tpb, batch)

    spec = pltpu.PrefetchScalarGridSpec(
        num_scalar_prefetch=5,
        grid=(n_steps,),
        in_specs=[
            pl.BlockSpec((EXPERT_ROWS, D_MODEL), lambda i, pb, ps, pv, pf, pq: (pb[i], 0)),
            pl.BlockSpec((ROUTE_TILE, 4), lambda i, pb, ps, pv, pf, pq: (ps[i], 0)),
            pl.BlockSpec((ROUTE_TILE, D_MODEL), lambda i, pb, ps, pv, pf, pq: (ps[i], 0)),
            pl.BlockSpec((None, 1, D_MODEL), lambda i, pb, ps, pv, pf, pq: (row(ps[i]), 0, 5)),
            pl.BlockSpec((1, D_MODEL), lambda i, pb, ps, pv, pf, pq: (0, 0)),
            pl.BlockSpec((1, D_MODEL), lambda i, pb, ps, pv, pf, pq: (0, 0)),
        ],
        out_specs=pl.BlockSpec((ROUTE_TILE, D_MODEL), lambda i, pb, ps, pv, pf, pq: (ps[i], 0)),
        scratch_shapes=[pltpu.VMEM((ROUTE_TILE, D_MODEL), F32)],
    )
    return pl.pallas_call(
        _combine_kernel,
        grid_spec=spec,
        out_shape=jax.ShapeDtypeStruct((n, D_MODEL), F32),
        compiler_params=_params(("arbitrary",)),
        name="moe_combine",
    )(*lists, y, dw, x1, mod3, ln_g, ln_b)


def _moe(t, x1, mod3, wr, tri, w1, w3, w2, ln_g, ln_b, n_lat, seq, batch):
    n = t.shape[0]
    n_tiles = n // ROUTE_TILE
    n_blocks = -(-(2 * n + N_EXPERTS * (EXPERT_ROWS - 1)) // EXPERT_ROWS)
    i32 = jnp.int32

    route, before, total = _router(t, wr, tri)

    e12 = route[:, 0:2].astype(i32)
    rank = route[:, 4:6].astype(i32)
    counts = total[0, :N_EXPERTS].astype(i32)
    padded = (counts + EXPERT_ROWS - 1) // EXPERT_ROWS * EXPERT_ROWS
    pad_end = jnp.cumsum(padded)
    pad_start = pad_end - padded
    dest = pad_start[e12] + rank
    dst = dest.reshape(n_tiles, ROUTE_TILE, 2).transpose(0, 2, 1)
    dw = jnp.concatenate([dest.astype(F32), route[:, 2:4]], axis=1)
    block_start = jnp.arange(n_blocks, dtype=i32) * EXPERT_ROWS
    block_e = jnp.minimum(jnp.searchsorted(pad_end, block_start, side="right"), N_EXPERTS - 1).astype(i32)
    block_used = (block_start < pad_end[-1]).astype(i32)

    cb = before[:, 0, :N_EXPERTS].astype(i32)
    ca = jnp.concatenate([cb[1:], counts[None, :]], axis=0)
    lo = pad_start[None, :] + cb
    hi = pad_start[None, :] + ca
    b_lo = lo // EXPERT_ROWS
    b_hi = (hi - 1) // EXPERT_ROWS
    some = hi > lo
    pair_b = jnp.stack([b_lo, b_lo + 1], axis=-1)
    pair_v = jnp.stack([some, some & (b_hi > b_lo)], axis=-1)
    pair_s = jnp.broadcast_to(jnp.arange(n_tiles, dtype=i32)[:, None, None], pair_b.shape)
    pair_b = pair_b.reshape(-1).astype(i32)
    pair_v = pair_v.reshape(-1)
    pair_s = pair_s.reshape(-1)

    gb = jnp.concatenate([jnp.arange(n_blocks, dtype=i32), jnp.where(pair_v, pair_b, n_blocks - 1)])
    gs = jnp.concatenate([jnp.zeros((n_blocks,), i32), pair_s])
    gv = jnp.concatenate([jnp.zeros((n_blocks,), bool), pair_v])
    order = jnp.argsort(gb, stable=True)
    gb, gs, gv = gb[order], gs[order], gv[order]
    gf = jnp.concatenate([jnp.ones((1,), bool), gb[1:] != gb[:-1]])
    xs = _gather((gb, gs, gv.astype(i32), gf.astype(i32)), t, dst, n_blocks)

    y = _experts(block_e, block_used, xs, w1, w3, w2)

    cs = jnp.where(pair_v, pair_s, n_tiles)
    order = jnp.argsort(cs, stable=True)
    cs, cbk, cv = cs[order], pair_b[order], pair_v[order]
    cf = cv & jnp.concatenate([jnp.ones((1,), bool), cs[1:] != cs[:-1]])
    cl = cv & jnp.concatenate([cs[1:] != cs[:-1], jnp.ones((1,), bool)])
    cs = jnp.minimum(cs, n_tiles - 1)
    cbk = jnp.where(cv, cbk, 0)
    lists = (cbk.astype(i32), cs.astype(i32), cv.astype(i32), cf.astype(i32), cl.astype(i32))
    return _combine(lists, y, dw, x1, mod3, ln_g, ln_b, n_lat, seq, batch)


def _window_counts(n, w):
    t = np.arange(n)
    left = w // 2
    right = w - 1 - left
    return (np.minimum(t + right + 1, n) - np.maximum(t - left, 0)).astype(np.float32)


def _window_matrix(n, w):
    left = w // 2
    right = w - 1 - left
    t = np.arange(n)
    return ((t[None, :] >= t[:, None] - left) & (t[None, :] <= t[:, None] + right)).astype(np.float32)


def _pool_tables(seq, n_ctx):
    rows = seq // GRID_W
    per_tile = 256 // GRID_W
    mc = np.stack([np.kron(np.eye(per_tile, dtype=np.float32), _window_matrix(GRID_W, w)) for w in POOL_WINDOWS])
    inv = np.stack([1.0 / np.outer(_window_counts(rows, w), _window_counts(GRID_W, w)).reshape(seq)
                    for w in POOL_WINDOWS])
    inv = np.broadcast_to(inv[:, :, None], (4, seq, POOL_GROUP)).astype(np.float32)
    m1d = np.stack([_window_matrix(n_ctx, w) for w in POOL_WINDOWS])
    inv1d = np.stack([1.0 / _window_counts(n_ctx, w) for w in POOL_WINDOWS])
    inv1d = np.broadcast_to(inv1d[:, :, None], (4, n_ctx, POOL_GROUP)).astype(np.float32)
    return (jnp.asarray(mc, BF16), jnp.asarray(inv), jnp.asarray(m1d, BF16), jnp.asarray(inv1d))


def _rope_tables(seq):
    t = jnp.arange(seq)
    row = (t // GRID_W).astype(F32)
    col = (t % GRID_W).astype(F32)
    n_freq = RET_QK_DIM // 4
    inv = jnp.exp(-jnp.log(ROPE_BASE) * jnp.arange(n_freq, dtype=F32) / n_freq)
    ang = jnp.concatenate([row[:, None] * inv, col[:, None] * inv], -1)
    cos, sin = jnp.cos(ang), jnp.sin(ang)
    cos2 = jnp.concatenate([cos, cos], -1)
    sin2 = jnp.concatenate([-sin, sin], -1)
    cos2 = jnp.concatenate([cos2, jnp.ones((TOKEN_TILE, RET_QK_DIM), F32)], 0)
    sin2 = jnp.concatenate([sin2, jnp.zeros((TOKEN_TILE, RET_QK_DIM), F32)], 0)
    return cos2, sin2


def kernel(x, c, ctx, c_ctx, ada_w, ada_b, w_in, pool_w, pool_scale, w_pool_out, w_ret_out, ret_decay_logit,
           w_out, ln_mix_g, ln_mix_b, ln_ffn_g, ln_ffn_b, ffn_w1, ffn_w3, ffn_w2, moe_router, moe_w1, moe_w3,
           moe_w2):
    batch, seq, d = x.shape
    n_ctx = ctx.shape[1]
    n_lat = batch * seq
    depth = ada_w.shape[0]
    assert d == D_MODEL and depth == DEPTH and batch < MOD_ROWS
    assert seq % TOKEN_TILE == 0 and (batch * n_ctx) % TOKEN_TILE == 0 and n_lat % n_ctx == 0

    s_in = jnp.zeros((MOD_ROWS, d), F32).at[:batch].set(c).at[batch].set(c_ctx)
    mod = _modulation(s_in, ada_w, ada_b)
    cos2, sin2 = _rope_tables(seq)
    mc, inv, m1d, inv1d = _pool_tables(seq, n_ctx)
    tri = jnp.asarray(np.tril(np.ones((ROUTE_TILE, ROUTE_TILE), np.float32), -1), BF16)

    xs = jnp.concatenate([x.reshape(n_lat, d), ctx.reshape(batch * n_ctx, d)], axis=0)
    for l in range(depth):
        last = l == depth - 1
        mod3 = mod[l].reshape(MOD_ROWS, 1, 6 * d)
        w_l = w_in[l].astype(BF16)
        pa, pb = _in_proj(xs, mod3, cos2, sin2, w_l[:, :OFF_G], w_l[:, OFF_G:], n_lat, seq, batch)
        pw = pool_w[l].astype(BF16)
        ps = pool_scale[l].reshape(1, POOL_WIDTH)
        yp_lat = _pool_lat(pa, mc, inv, pw, ps, batch, seq)
        yp_ctx = _pool_ctx(pa, m1d, inv1d, pw, ps, batch, n_lat, n_ctx)
        dl = jnp.broadcast_to(ret_decay_logit[l].astype(F32)[:, :, None, None],
                              (2, RET_HEADS, RET_CHUNK, RET_V_DIM))
        z_lat, z_ctx = _retention(pa, pb, dl, batch, seq, n_ctx)
        n_out = n_lat if last else xs.shape[0]
        x1, t = _merge(xs, mod3, yp_lat, yp_ctx, z_lat, z_ctx, pb,
                       w_pool_out[l].astype(BF16), w_ret_out[l].astype(BF16), w_out[l].astype(BF16),
                       ln_mix_g[l].reshape(1, d), ln_mix_b[l].reshape(1, d), n_out, n_lat, seq, batch)
        lg = ln_ffn_g[l].reshape(1, d)
        lb = ln_ffn_b[l].reshape(1, d)
        if l % 2 == 0:
            j = l // 2
            xs = _ffn(t, x1, mod3, ffn_w1[j].astype(BF16), ffn_w3[j].astype(BF16), ffn_w2[j].astype(BF16),
                      lg, lb, n_lat, seq, batch)
        else:
            j = l // 2
            wr = jnp.zeros((d, ROUTE_LANES), BF16).at[:, :N_EXPERTS].set(moe_router[j].astype(BF16))
            xs = _moe(t, x1, mod3, wr, tri, moe_w1[j].astype(BF16), moe_w3[j].astype(BF16),
                      moe_w2[j].astype(BF16), lg, lb, n_lat, seq, batch)
    return xs[:n_lat].reshape(batch, seq, d)
```

```python
import functools

import jax
import jax.numpy as jnp
import numpy as np
from jax import lax
from jax.experimental import pallas as pl
from jax.experimental.pallas import tpu as pltpu

F32 = jnp.float32
BF16 = jnp.bfloat16

D_MODEL = 1024
DEPTH = 4
GRID_W = 64
POOL_WINDOWS = (2, 4, 8, 16)
POOL_GROUP = 128
POOL_WIDTH = POOL_GROUP * len(POOL_WINDOWS)
RET_HEADS = 4
RET_QK_DIM = 128
RET_V_DIM = 256
RET_CHUNK = 128
ROPE_BASE = 10000.0
OFF_Q = POOL_WIDTH
OFF_K = OFF_Q + RET_HEADS * RET_QK_DIM
OFF_V = OFF_K + RET_HEADS * RET_QK_DIM
OFF_G = OFF_V + RET_HEADS * RET_V_DIM
WIDTH_B = 3 * D_MODEL
D_FF = 2816
N_EXPERTS = 8
EXPERT_FF = 3584
DEEPNORM_ALPHA = (2 * DEPTH) ** 0.25
LN_EPS = 1e-5
K_SCALE = RET_QK_DIM ** -0.5

MOD_ROWS = 24
TOKEN_TILE = 512
ROUTE_TILE = 256
EXPERT_ROWS = 512
FFN_CHUNK = D_FF // 2
EXPERT_CHUNK = EXPERT_FF // 2
VMEM_LIMIT = 56 * 1024 * 1024


def _dot(a, b):
    return jnp.dot(a, b, preferred_element_type=F32)


def _sigmoid(x):
    return 1.0 / (1.0 + jnp.exp(-x))


def _split_bf16(a):
    hi = a.astype(BF16)
    lo = (a - hi.astype(F32)).astype(BF16)
    return hi, lo


def _layer_norm(v, g, b):
    mean = jnp.mean(v, axis=-1, keepdims=True)
    vc = v - mean
    var = jnp.mean(vc * vc, axis=-1, keepdims=True)
    return vc * lax.rsqrt(var + LN_EPS) * g + b


def _params(sem, vmem=VMEM_LIMIT):
    return pltpu.CompilerParams(dimension_semantics=sem, vmem_limit_bytes=vmem)


def _mod_kernel(s_ref, w_ref, b_ref, o_ref):
    s = s_ref[...]
    s = s * _sigmoid(s)
    s_hi, s_lo = _split_bf16(s)
    w_hi, w_lo = _split_bf16(w_ref[...])
    o_ref[...] = _dot(s_hi, w_hi) + (_dot(s_hi, w_lo) + _dot(s_lo, w_hi)) + b_ref[...]


def _modulation(s_in, ada_w, ada_b):
    depth, d, width = ada_w.shape
    tn = 1536
    return pl.pallas_call(
        _mod_kernel,
        grid=(depth, width // tn),
        in_specs=[
            pl.BlockSpec((MOD_ROWS, d), lambda l, j: (0, 0)),
            pl.BlockSpec((None, d, tn), lambda l, j: (l, 0, j)),
            pl.BlockSpec((None, 1, tn), lambda l, j: (l, 0, j)),
        ],
        out_specs=pl.BlockSpec((None, MOD_ROWS, tn), lambda l, j: (l, 0, j)),
        out_shape=jax.ShapeDtypeStruct((depth, MOD_ROWS, width), F32),
        compiler_params=_params(("arbitrary", "arbitrary")),
        name="adaln_mod",
    )(s_in, ada_w, ada_b.reshape(depth, 1, width))


def _in_kernel(x_ref, sh_ref, sc_ref, cos_ref, sin_ref, wa_ref, wb_ref, pa_ref, pb_ref):
    h = (x_ref[...] * (1.0 + sc_ref[...]) + sh_ref[...]).astype(BF16)
    cos = cos_ref[...]
    sin = sin_ref[...]
    pa_ref[:, 0:OFF_Q] = _dot(h, wa_ref[:, 0:OFF_Q]).astype(BF16)
    for c0, scale in ((OFF_Q, None), (OFF_K, K_SCALE)):
        acc = _dot(h, wa_ref[:, c0:c0 + 512])
        for hh in range(RET_HEADS):
            t = acc[:, hh * RET_QK_DIM:(hh + 1) * RET_QK_DIM]
            r = t * cos + pltpu.roll(t, RET_QK_DIM // 2, 1) * sin
            if scale is not None:
                r = r * scale
            pa_ref[:, c0 + hh * RET_QK_DIM:c0 + (hh + 1) * RET_QK_DIM] = r.astype(BF16)
    for c0 in range(OFF_V, OFF_G, 512):
        pa_ref[:, c0:c0 + 512] = _dot(h, wa_ref[:, c0:c0 + 512]).astype(BF16)
    for c0 in range(0, WIDTH_B, 512):
        pb_ref[:, c0:c0 + 512] = _dot(h, wb_ref[:, c0:c0 + 512]).astype(BF16)


def _in_proj(x, mod3, cos2, sin2, wa, wb, n_lat, seq, batch):
    n = x.shape[0]
    tm = TOKEN_TILE
    nl = n_lat // tm
    tpb = seq // tm

    def row(i):
        return jnp.where(i < nl, i // tpb, batch)

    def rope(i):
        return jnp.where(i < nl, i % tpb, tpb)

    return pl.pallas_call(
        _in_kernel,
        grid=(n // tm,),
        in_specs=[
            pl.BlockSpec((tm, D_MODEL), lambda i: (i, 0)),
            pl.BlockSpec((None, 1, D_MODEL), lambda i: (row(i), 0, 0)),
            pl.BlockSpec((None, 1, D_MODEL), lambda i: (row(i), 0, 1)),
            pl.BlockSpec((tm, RET_QK_DIM), lambda i: (rope(i), 0)),
            pl.BlockSpec((tm, RET_QK_DIM), lambda i: (rope(i), 0)),
            pl.BlockSpec((D_MODEL, OFF_G), lambda i: (0, 0)),
            pl.BlockSpec((D_MODEL, WIDTH_B), lambda i: (0, 0)),
        ],
        out_specs=[
            pl.BlockSpec((tm, OFF_G), lambda i: (i, 0)),
            pl.BlockSpec((tm, WIDTH_B), lambda i: (i, 0)),
        ],
        out_shape=[
            jax.ShapeDtypeStruct((n, OFF_G), BF16),
            jax.ShapeDtypeStruct((n, WIDTH_B), BF16),
        ],
        compiler_params=_params(("arbitrary",)),
        name="in_proj",
    )(x, mod3, mod3, cos2, sin2, wa, wb)


POOL_PAD_ROWS = 8
POOL_ROW_CHUNK = 8


def _pool_kernel(u_ref, mc_ref, inv_ref, pw_ref, ps_ref, o_ref, zp_ref, *, rows):
    seq = rows * GRID_W
    pad = POOL_PAD_ROWS * GRID_W
    chunk = POOL_ROW_CHUNK * GRID_W
    zeros = jnp.zeros((pad, POOL_GROUP), F32)
    for g, w in enumerate(POOL_WINDOWS):
        lanes = slice(g * POOL_GROUP, (g + 1) * POOL_GROUP)
        zp_ref[0:pad, :] = zeros
        zp_ref[pad + seq:pad + seq + pad, :] = zeros
        for c in range(seq // 256):
            zp_ref[pad + c * 256:pad + (c + 1) * 256, :] = _dot(mc_ref[g], u_ref[c * 256:(c + 1) * 256, lanes])
        left = w // 2

        def body(rc, carry, g=g, w=w, left=left, lanes=lanes):
            tok = pl.multiple_of(rc * chunk, chunk)
            acc = zp_ref[pl.ds(tok + pad - left * GRID_W, chunk), :]
            for k in range(1, w):
                acc = acc + zp_ref[pl.ds(tok + pad + (k - left) * GRID_W, chunk), :]
            pooled = acc * inv_ref[g, pl.ds(tok, chunk), :]
            d = (pooled - u_ref[pl.ds(tok, chunk), lanes].astype(F32)).astype(BF16)
            y = _dot(d, pw_ref[g]) * ps_ref[:, lanes]
            o_ref[pl.ds(tok, chunk), lanes] = y.astype(BF16)
            return carry

        lax.fori_loop(0, rows // POOL_ROW_CHUNK, body, 0)


def _pool_lat(pa, mc, inv, pw, ps, batch, seq):
    rows = seq // GRID_W
    return pl.pallas_call(
        functools.partial(_pool_kernel, rows=rows),
        grid=(batch,),
        in_specs=[
            pl.BlockSpec((seq, POOL_WIDTH), lambda b: (b, 0)),
            pl.BlockSpec((4, 256, 256), lambda b: (0, 0, 0)),
            pl.BlockSpec((4, seq, POOL_GROUP), lambda b: (0, 0, 0)),
            pl.BlockSpec((4, POOL_GROUP, POOL_GROUP), lambda b: (0, 0, 0)),
            pl.BlockSpec((1, POOL_WIDTH), lambda b: (0, 0)),
        ],
        out_specs=pl.BlockSpec((seq, POOL_WIDTH), lambda b: (b, 0)),
        out_shape=jax.ShapeDtypeStruct((batch * seq, POOL_WIDTH), BF16),
        scratch_shapes=[pltpu.VMEM(((rows + 2 * POOL_PAD_ROWS) * GRID_W, POOL_GROUP), F32)],
        compiler_params=_params(("arbitrary",)),
        name="pool_lat",
    )(pa, mc, inv, pw, ps)


def _pool_ctx_kernel(u_ref, m_ref, inv_ref, pw_ref, ps_ref, o_ref):
    for g in range(len(POOL_WINDOWS)):
        lanes = slice(g * POOL_GROUP, (g + 1) * POOL_GROUP)
        ug = u_ref[:, lanes]
        pooled = _dot(m_ref[g], ug) * inv_ref[g]
        d = (pooled - ug.astype(F32)).astype(BF16)
        o_ref[:, lanes] = (_dot(d, pw_ref[g]) * ps_ref[:, lanes]).astype(BF16)


def _pool_ctx(pa, m1d, inv1d, pw, ps, batch, n_lat, n_ctx):
    first = n_lat // n_ctx
    return pl.pallas_call(
        _pool_ctx_kernel,
        grid=(batch,),
        in_specs=[
            pl.BlockSpec((n_ctx, POOL_WIDTH), lambda b: (first + b, 0)),
            pl.BlockSpec((4, n_ctx, n_ctx), lambda b: (0, 0, 0)),
            pl.BlockSpec((4, n_ctx, POOL_GROUP), lambda b: (0, 0, 0)),
            pl.BlockSpec((4, POOL_GROUP, POOL_GROUP), lambda b: (0, 0, 0)),
            pl.BlockSpec((1, POOL_WIDTH), lambda b: (0, 0)),
        ],
        out_specs=pl.BlockSpec((n_ctx, POOL_WIDTH), lambda b: (b, 0)),
        out_shape=jax.ShapeDtypeStruct((batch * n_ctx, POOL_WIDTH), BF16),
        compiler_params=_params(("arbitrary",)),
        name="pool_ctx",
    )(pa, m1d, inv1d, pw, ps)


def _log_sigmoid(x):
    return jnp.minimum(x, 0.0) - jnp.log1p(jnp.exp(-jnp.abs(x)))


def _decays(dl, backward):
    c = RET_CHUNK
    lg = _log_sigmoid(dl)
    lgq = lg[:, :RET_QK_DIM]
    ii = lax.broadcasted_iota(jnp.int32, (c, c), 0)
    jj = lax.broadcasted_iota(jnp.int32, (c, c), 1)
    pos = lax.broadcasted_iota(jnp.int32, (c, RET_QK_DIM), 0).astype(F32)
    if backward:
        diff = (jj - ii).astype(F32)
        qdec = jnp.exp(lgq * (c - pos))
        kdec = jnp.exp(lgq * pos)
    else:
        diff = (ii - jj).astype(F32)
        qdec = jnp.exp(lgq * (pos + 1.0))
        kdec = jnp.exp(lgq * (c - 1.0 - pos))
    inner = jnp.where(diff >= 0, jnp.exp(lgq * jnp.maximum(diff, 0.0)), 0.0)
    cdec = jnp.exp(lg * float(c))
    return inner, qdec, kdec, cdec


def _ret_step(qc, kc, vc, s, dec):
    inner, qdec, kdec, cdec = dec
    scores = lax.dot_general(qc, kc, (((1,), (1,)), ((), ())), preferred_element_type=F32) * inner
    o = _dot(scores.astype(BF16), vc) + _dot((qc.astype(F32) * qdec).astype(BF16), s.astype(BF16))
    kd = (kc.astype(F32) * kdec).astype(BF16)
    s_new = s * cdec + lax.dot_general(kd, vc, (((0,), (0,)), ((), ())), preferred_element_type=F32)
    return o, s_new


def _head_norm_gate(o, g):
    mean = jnp.mean(o, axis=-1, keepdims=True)
    oc = o - mean
    var = jnp.mean(oc * oc, axis=-1, keepdims=True)
    on = oc * lax.rsqrt(var + LN_EPS)
    gf = g.astype(F32)
    return (gf * _sigmoid(gf) * on).astype(BF16)


def _ret_kernel(q_ref, k_ref, v_ref, g_ref, qc_ref, kc_ref, vc_ref, gc_ref, dl_ref,
                z_ref, zc_ref, of_ref, ob_ref, oc_ref, s_ref, *, n_chunks, n_ctx_chunks):
    c = RET_CHUNK
    dec_f = _decays(dl_ref[0], False)
    dec_b = _decays(dl_ref[1], True)

    s = jnp.zeros((RET_QK_DIM, RET_V_DIM), F32)
    for j in range(n_ctx_chunks):
        rows = slice(j * c, (j + 1) * c)
        o, s = _ret_step(qc_ref[rows, :], kc_ref[rows, :], vc_ref[rows, :], s, dec_f)
        oc_ref[rows, :] = o
    s_ref[0] = s
    s = jnp.zeros((RET_QK_DIM, RET_V_DIM), F32)
    for j in reversed(range(n_ctx_chunks)):
        rows = slice(j * c, (j + 1) * c)
        o, s = _ret_step(qc_ref[rows, :], kc_ref[rows, :], vc_ref[rows, :], s, dec_b)
        oc_ref[rows, :] = oc_ref[rows, :] + o
    s_ref[1] = s
    for j in range(n_ctx_chunks):
        rows = slice(j * c, (j + 1) * c)
        zc_ref[rows, :] = _head_norm_gate(oc_ref[rows, :], gc_ref[rows, :])

    def scan(t, carry):
        rf = pl.ds(pl.multiple_of(t * c, c), c)
        o, s_new = _ret_step(q_ref[rf, :], k_ref[rf, :], v_ref[rf, :], s_ref[0], dec_f)
        of_ref[rf, :] = o
        s_ref[0] = s_new
        rb = pl.ds(pl.multiple_of((n_chunks - 1 - t) * c, c), c)
        o, s_new = _ret_step(q_ref[rb, :], k_ref[rb, :], v_ref[rb, :], s_ref[1], dec_b)
        ob_ref[rb, :] = o
        s_ref[1] = s_new
        return carry

    lax.fori_loop(0, n_chunks, scan, 0)

    def finish(t, carry):
        r = pl.ds(pl.multiple_of(t * c, c), c)
        z_ref[r, :] = _head_norm_gate(of_ref[r, :] + ob_ref[r, :], g_ref[r, :])
        return carry

    lax.fori_loop(0, n_chunks, finish, 0)


def _retention(pa, pb, dl, batch, seq, n_ctx):
    n_lat = batch * seq
    first = n_lat // n_ctx
    qk0 = OFF_Q // RET_QK_DIM
    kk0 = OFF_K // RET_QK_DIM
    v0 = OFF_V // RET_V_DIM
    return pl.pallas_call(
        functools.partial(_ret_kernel, n_chunks=seq // RET_CHUNK, n_ctx_chunks=n_ctx // RET_CHUNK),
        grid=(batch, RET_HEADS),
        in_specs=[
            pl.BlockSpec((seq, RET_QK_DIM), lambda b, h: (b, qk0 + h)),
            pl.BlockSpec((seq, RET_QK_DIM), lambda b, h: (b, kk0 + h)),
            pl.BlockSpec((seq, RET_V_DIM), lambda b, h: (b, v0 + h)),
            pl.BlockSpec((seq, RET_V_DIM), lambda b, h: (b, h)),
            pl.BlockSpec((n_ctx, RET_QK_DIM), lambda b, h: (first + b, qk0 + h)),
            pl.BlockSpec((n_ctx, RET_QK_DIM), lambda b, h: (first + b, kk0 + h)),
            pl.BlockSpec((n_ctx, RET_V_DIM), lambda b, h: (first + b, v0 + h)),
            pl.BlockSpec((n_ctx, RET_V_DIM), lambda b, h: (first + b, h)),
            pl.BlockSpec((2, None, RET_CHUNK, RET_V_DIM), lambda b, h: (0, h, 0, 0)),
        ],
        out_specs=[
            pl.BlockSpec((seq, RET_V_DIM), lambda b, h: (b, h)),
            pl.BlockSpec((n_ctx, RET_V_DIM), lambda b, h: (b, h)),
        ],
        out_shape=[
            jax.ShapeDtypeStruct((n_lat, RET_HEADS * RET_V_DIM), BF16),
            jax.ShapeDtypeStruct((batch * n_ctx, RET_HEADS * RET_V_DIM), BF16),
        ],
        scratch_shapes=[
            pltpu.VMEM((seq, RET_V_DIM), F32),
            pltpu.VMEM((seq, RET_V_DIM), F32),
            pltpu.VMEM((n_ctx, RET_V_DIM), F32),
            pltpu.VMEM((2, RET_QK_DIM, RET_V_DIM), F32),
        ],
        compiler_params=_params(("arbitrary", "arbitrary")),
        name="retention",
    )(pa, pa, pa, pb, pa, pa, pa, pb, dl)


def _merge_kernel(x_ref, gm_ref, sh_ref, sc_ref, ypl_ref, ypc_ref, zl_ref, zc_ref, gp_ref, gr_ref,
                  wp_ref, wr_ref, wo_ref, lg_ref, lb_ref, x1_ref, t_ref, *, n_lat_tiles):
    is_lat = pl.program_id(0) < n_lat_tiles
    yp = jnp.where(is_lat, ypl_ref[...], ypc_ref[...])
    z = jnp.where(is_lat, zl_ref[...], zc_ref[...])
    y_pool = _dot(yp, wp_ref[...])
    y_ret = _dot(z, wr_ref[...])
    mix = _sigmoid(gp_ref[...].astype(F32)) * y_pool + _sigmoid(gr_ref[...].astype(F32)) * y_ret
    y = _dot(mix.astype(BF16), wo_ref[...])
    x1 = _layer_norm(DEEPNORM_ALPHA * x_ref[...] + gm_ref[...] * y, lg_ref[...], lb_ref[...])
    x1_ref[...] = x1
    t_ref[...] = (x1 * (1.0 + sc_ref[...]) + sh_ref[...]).astype(BF16)


def _merge(x, mod3, yp_lat, yp_ctx, z_lat, z_ctx, pb, wp, wr, wo, ln_g, ln_b, n_out, n_lat, seq, batch):
    tm = TOKEN_TILE
    nl = n_lat // tm
    tpb = seq // tm

    def row(i):
        return jnp.where(i < nl, i // tpb, batch)

    def lat(i):
        return jnp.minimum(i, nl - 1)

    def ctx(i):
        return jnp.maximum(i - nl, 0)

    return pl.pallas_call(
        functools.partial(_merge_kernel, n_lat_tiles=nl),
        grid=(n_out // tm,),
        in_specs=[
            pl.BlockSpec((tm, D_MODEL), lambda i: (i, 0)),
            pl.BlockSpec((None, 1, D_MODEL), lambda i: (row(i), 0, 2)),
            pl.BlockSpec((None, 1, D_MODEL), lambda i: (row(i), 0, 3)),
            pl.BlockSpec((None, 1, D_MODEL), lambda i: (row(i), 0, 4)),
            pl.BlockSpec((tm, POOL_WIDTH), lambda i: (lat(i), 0)),
            pl.BlockSpec((tm, POOL_WIDTH), lambda i: (ctx(i), 0)),
            pl.BlockSpec((tm, D_MODEL), lambda i: (lat(i), 0)),
            pl.BlockSpec((tm, D_MODEL), lambda i: (ctx(i), 0)),
            pl.BlockSpec((tm, D_MODEL), lambda i: (i, 1)),
            pl.BlockSpec((tm, D_MODEL), lambda i: (i, 2)),
            pl.BlockSpec((POOL_WIDTH, D_MODEL), lambda i: (0, 0)),
            pl.BlockSpec((D_MODEL, D_MODEL), lambda i: (0, 0)),
            pl.BlockSpec((D_MODEL, D_MODEL), lambda i: (0, 0)),
            pl.BlockSpec((1, D_MODEL), lambda i: (0, 0)),
            pl.BlockSpec((1, D_MODEL), lambda i: (0, 0)),
        ],
        out_specs=[
            pl.BlockSpec((tm, D_MODEL), lambda i: (i, 0)),
            pl.BlockSpec((tm, D_MODEL), lambda i: (i, 0)),
        ],
        out_shape=[
            jax.ShapeDtypeStruct((n_out, D_MODEL), F32),
            jax.ShapeDtypeStruct((n_out, D_MODEL), BF16),
        ],
        compiler_params=_params(("arbitrary",)),
        name="merge",
    )(x, mod3, mod3, mod3, yp_lat, yp_ctx, z_lat, z_ctx, pb, pb, wp, wr, wo, ln_g, ln_b)


def _ffn_kernel(t_ref, x1_ref, gm_ref, w1_ref, w3_ref, w2_ref, lg_ref, lb_ref, o_ref, acc_ref):
    f = pl.program_id(1)
    t = t_ref[...]
    a = _dot(t, w1_ref[...])
    hmid = (a * _sigmoid(a) * _dot(t, w3_ref[...])).astype(BF16)
    part = _dot(hmid, w2_ref[...])

    @pl.when(f == 0)
    def _():
        acc_ref[...] = part

    @pl.when(f > 0)
    def _():
        acc_ref[...] = acc_ref[...] + part

    @pl.when(f == pl.num_programs(1) - 1)
    def _():
        o_ref[...] = _layer_norm(DEEPNORM_ALPHA * x1_ref[...] + gm_ref[...] * acc_ref[...],
                                 lg_ref[...], lb_ref[...])


def _ffn(t, x1, mod3, w1, w3, w2, ln_g, ln_b, n_lat, seq, batch):
    n = t.shape[0]
    tm = TOKEN_TILE
    nl = n_lat // tm
    tpb = seq // tm

    def row(i):
        return jnp.where(i < nl, i // tpb, batch)

    return pl.pallas_call(
        _ffn_kernel,
        grid=(n // tm, D_FF // FFN_CHUNK),
        in_specs=[
            pl.BlockSpec((tm, D_MODEL), lambda i, f: (i, 0)),
            pl.BlockSpec((tm, D_MODEL), lambda i, f: (i, 0)),
            pl.BlockSpec((None, 1, D_MODEL), lambda i, f: (row(i), 0, 5)),
            pl.BlockSpec((D_MODEL, FFN_CHUNK), lambda i, f: (0, f)),
            pl.BlockSpec((D_MODEL, FFN_CHUNK), lambda i, f: (0, f)),
            pl.BlockSpec((FFN_CHUNK, D_MODEL), lambda i, f: (f, 0)),
            pl.BlockSpec((1, D_MODEL), lambda i, f: (0, 0)),
            pl.BlockSpec((1, D_MODEL), lambda i, f: (0, 0)),
        ],
        out_specs=pl.BlockSpec((tm, D_MODEL), lambda i, f: (i, 0)),
        out_shape=jax.ShapeDtypeStruct((n, D_MODEL), F32),
        scratch_shapes=[pltpu.VMEM((tm, D_MODEL), F32)],
        compiler_params=_params(("arbitrary", "arbitrary")),
        name="ffn_dense",
    )(t, x1, mod3, w1, w3, w2, ln_g, ln_b)


ROUTE_LANES = 128


def _router_kernel(t_ref, wr_ref, tri_ref, route_ref, before_ref, total_ref, run_ref):
    @pl.when(pl.program_id(0) == 0)
    def _():
        run_ref[...] = jnp.zeros_like(run_ref)

    tt = t_ref.shape[0]
    logits = _dot(t_ref[...], wr_ref[...])
    lane = lax.broadcasted_iota(jnp.int32, (tt, ROUTE_LANES), 1)
    neg = jnp.float32(-jnp.inf)
    lg = jnp.where(lane < N_EXPERTS, logits, neg)
    m1 = jnp.max(lg, axis=1, keepdims=True)
    i1 = jnp.min(jnp.where(lg == m1, lane, ROUTE_LANES), axis=1, keepdims=True)
    lg2 = jnp.where(lane == i1, neg, lg)
    m2 = jnp.max(lg2, axis=1, keepdims=True)
    i2 = jnp.min(jnp.where(lg2 == m2, lane, ROUTE_LANES), axis=1, keepdims=True)
    e = jnp.exp(m2 - m1)
    w1 = 1.0 / (1.0 + e)
    w2 = e / (1.0 + e)
    hit1 = lane == i1
    hit2 = lane == i2
    onehot = jnp.where(hit1 | hit2, 1.0, 0.0)
    run = run_ref[...]
    prefix = _dot(tri_ref[...], onehot.astype(BF16)) + run
    r1 = jnp.sum(jnp.where(hit1, prefix, 0.0), axis=1, keepdims=True)
    r2 = jnp.sum(jnp.where(hit2, prefix, 0.0), axis=1, keepdims=True)
    before_ref[...] = run
    run = run + jnp.sum(onehot, axis=0, keepdims=True)
    run_ref[...] = run
    total_ref[...] = run
    out = jnp.where(lane == 0, i1.astype(F32), 0.0)
    out = jnp.where(lane == 1, i2.astype(F32), out)
    out = jnp.where(lane == 2, w1, out)
    out = jnp.where(lane == 3, w2, out)
    out = jnp.where(lane == 4, r1, out)
    out = jnp.where(lane == 5, r2, out)
    route_ref[...] = out


def _router(t, wr, tri):
    n = t.shape[0]
    tt = ROUTE_TILE
    n_tiles = n // tt
    return pl.pallas_call(
        _router_kernel,
        grid=(n_tiles,),
        in_specs=[
            pl.BlockSpec((tt, D_MODEL), lambda i: (i, 0)),
            pl.BlockSpec((D_MODEL, ROUTE_LANES), lambda i: (0, 0)),
            pl.BlockSpec((tt, tt), lambda i: (0, 0)),
        ],
        out_specs=[
            pl.BlockSpec((tt, ROUTE_LANES), lambda i: (i, 0)),
            pl.BlockSpec((None, 1, ROUTE_LANES), lambda i: (i, 0, 0)),
            pl.BlockSpec((1, ROUTE_LANES), lambda i: (0, 0)),
        ],
        out_shape=[
            jax.ShapeDtypeStruct((n, ROUTE_LANES), F32),
            jax.ShapeDtypeStruct((n_tiles, 1, ROUTE_LANES), F32),
            jax.ShapeDtypeStruct((1, ROUTE_LANES), F32),
        ],
        scratch_shapes=[pltpu.VMEM((1, ROUTE_LANES), F32)],
        compiler_params=_params(("arbitrary",)),
        name="moe_router",
    )(t, wr, tri)


GATHER_ROWS = ROUTE_TILE + 16


def _gather_kernel(es_ref, s_ref, a_ref, dst_ref, t_hbm, o_ref, buf, sem):
    b = pl.program_id(0)
    j0 = es_ref[b]
    n = es_ref[b + 1] - j0
    o_ref[...] = jnp.zeros_like(o_ref)

    def tile_copy(j, slot):
        start = pl.multiple_of(s_ref[j] * ROUTE_TILE, ROUTE_TILE)
        return pltpu.make_async_copy(t_hbm.at[pl.ds(start, ROUTE_TILE)], buf.at[slot], sem.at[slot])

    @pl.when(n > 0)
    def _():
        tile_copy(j0, 0).start()

    def body(k, carry):
        j = j0 + k
        slot = k % 2
        tile_copy(j, slot).wait()

        @pl.when(k + 1 < n)
        def _():
            tile_copy(j + 1, 1 - slot).start()

        a = pl.multiple_of(a_ref[j], 16)
        rows = b * EXPERT_ROWS + a + lax.broadcasted_iota(jnp.int32, (GATHER_ROWS, ROUTE_TILE), 0)
        d = dst_ref[s_ref[j]]
        sel = (d[0:1, :] == rows) | (d[1:2, :] == rows)
        win = pl.ds(a, GATHER_ROWS)
        o_ref[win, :] = o_ref[win, :] + _dot(jnp.where(sel, 1.0, 0.0).astype(BF16), buf[slot]).astype(BF16)
        return carry

    lax.fori_loop(0, n, body, 0)


def _gather(lists, t, dst, n_blocks):
    n_tiles = dst.shape[0]
    spec = pltpu.PrefetchScalarGridSpec(
        num_scalar_prefetch=3,
        grid=(n_blocks,),
        in_specs=[
            pl.BlockSpec((n_tiles, 2, ROUTE_TILE), lambda b, es, ss, aa: (0, 0, 0)),
            pl.BlockSpec(memory_space=pl.ANY),
        ],
        out_specs=pl.BlockSpec((EXPERT_ROWS, D_MODEL), lambda b, es, ss, aa: (b, 0)),
        scratch_shapes=[
            pltpu.VMEM((2, ROUTE_TILE, D_MODEL), BF16),
            pltpu.SemaphoreType.DMA((2,)),
        ],
    )
    return pl.pallas_call(
        _gather_kernel,
        grid_spec=spec,
        out_shape=jax.ShapeDtypeStruct((n_blocks * EXPERT_ROWS, D_MODEL), BF16),
        compiler_params=_params(("arbitrary",)),
        name="moe_gather",
    )(*lists, dst, t)


def _expert_kernel(be_ref, bv_ref, x_ref, w1_ref, w3_ref, w2_ref, o_ref, acc_ref):
    b = pl.program_id(0)
    f = pl.program_id(1)
    last = pl.num_programs(1) - 1
    used = bv_ref[b] == 1

    @pl.when(used)
    def _():
        x = x_ref[...]
        a = _dot(x, w1_ref[...])
        hmid = (a * _sigmoid(a) * _dot(x, w3_ref[...])).astype(BF16)
        part = _dot(hmid, w2_ref[...])

        @pl.when(f == 0)
        def _():
            acc_ref[...] = part

        @pl.when(f > 0)
        def _():
            acc_ref[...] = acc_ref[...] + part

        @pl.when(f == last)
        def _():
            o_ref[...] = acc_ref[...].astype(BF16)

    @pl.when(jnp.logical_not(used) & (f == last))
    def _():
        o_ref[...] = jnp.zeros_like(o_ref)


def _experts(block_e, block_used, xs, w1, w3, w2):
    n_blocks = block_e.shape[0]
    spec = pltpu.PrefetchScalarGridSpec(
        num_scalar_prefetch=2,
        grid=(n_blocks, EXPERT_FF // EXPERT_CHUNK),
        in_specs=[
            pl.BlockSpec((EXPERT_ROWS, D_MODEL), lambda b, f, be, bv: (b, 0)),
            pl.BlockSpec((None, D_MODEL, EXPERT_CHUNK), lambda b, f, be, bv: (be[b], 0, f)),
            pl.BlockSpec((None, D_MODEL, EXPERT_CHUNK), lambda b, f, be, bv: (be[b], 0, f)),
            pl.BlockSpec((None, EXPERT_CHUNK, D_MODEL), lambda b, f, be, bv: (be[b], f, 0)),
        ],
        out_specs=pl.BlockSpec((EXPERT_ROWS, D_MODEL), lambda b, f, be, bv: (b, 0)),
        scratch_shapes=[pltpu.VMEM((EXPERT_ROWS, D_MODEL), F32)],
    )
    return pl.pallas_call(
        _expert_kernel,
        grid_spec=spec,
        out_shape=jax.ShapeDtypeStruct(xs.shape, BF16),
        compiler_params=_params(("arbitrary", "arbitrary")),
        name="moe_experts",
    )(block_e, block_used, xs, w1, w3, w2)


COMBINE_ROWS = ROUTE_TILE
DW_LANES = 8


def _combine_kernel(es_ref, r_ref, e_ref, dw_ref, x1_ref, gm_ref, lg_ref, lb_ref, y_hbm, o_ref, buf, sem, acc_ref):
    s = pl.program_id(0)
    j0 = es_ref[s]
    n = es_ref[s + 1] - j0
    acc_ref[...] = jnp.zeros_like(acc_ref)

    def win_copy(j, slot):
        start = pl.multiple_of(r_ref[j], 16)
        return pltpu.make_async_copy(y_hbm.at[pl.ds(start, COMBINE_ROWS)], buf.at[slot], sem.at[slot])

    @pl.when(n > 0)
    def _():
        win_copy(j0, 0).start()

    def body(k, carry):
        j = j0 + k
        slot = k % 2
        win_copy(j, slot).wait()

        @pl.when(k + 1 < n)
        def _():
            win_copy(j + 1, 1 - slot).start()

        base = r_ref[j]
        expert = e_ref[j]
        col = lax.broadcasted_iota(jnp.int32, (ROUTE_TILE, COMBINE_ROWS), 1)
        d1 = dw_ref[:, 0:1].astype(jnp.int32) - base
        d2 = dw_ref[:, 1:2].astype(jnp.int32) - base
        ok1 = dw_ref[:, 4:5].astype(jnp.int32) == expert
        ok2 = dw_ref[:, 5:6].astype(jnp.int32) == expert
        sel = ((d1 == col) & ok1) | ((d2 == col) & ok2)
        in1 = ok1 & (d1 >= 0) & (d1 < COMBINE_ROWS)
        in2 = ok2 & (d2 >= 0) & (d2 < COMBINE_ROWS)
        wsel = jnp.where(in1, dw_ref[:, 2:3], 0.0) + jnp.where(in2, dw_ref[:, 3:4], 0.0)
        acc_ref[...] = acc_ref[...] + wsel * _dot(jnp.where(sel, 1.0, 0.0).astype(BF16), buf[slot])
        return carry

    lax.fori_loop(0, n, body, 0)
    o_ref[...] = _layer_norm(DEEPNORM_ALPHA * x1_ref[...] + gm_ref[...] * acc_ref[...], lg_ref[...], lb_ref[...])


def _combine(lists, y, dw, x1, mod3, ln_g, ln_b, n_lat, seq, batch):
    n = x1.shape[0]
    nl = n_lat // ROUTE_TILE
    tpb = seq // ROUTE_TILE

    def row(s):
        return jnp.where(s < nl, s // tpb, batch)

    spec = pltpu.PrefetchScalarGridSpec(
        num_scalar_prefetch=3,
        grid=(n // ROUTE_TILE,),
        in_specs=[
            pl.BlockSpec((ROUTE_TILE, DW_LANES), lambda s, es, rr, ee: (s, 0)),
            pl.BlockSpec((ROUTE_TILE, D_MODEL), lambda s, es, rr, ee: (s, 0)),
            pl.BlockSpec((None, 1, D_MODEL), lambda s, es, rr, ee: (row(s), 0, 5)),
            pl.BlockSpec((1, D_MODEL), lambda s, es, rr, ee: (0, 0)),
            pl.BlockSpec((1, D_MODEL), lambda s, es, rr, ee: (0, 0)),
            pl.BlockSpec(memory_space=pl.ANY),
        ],
        out_specs=pl.BlockSpec((ROUTE_TILE, D_MODEL), lambda s, es, rr, ee: (s, 0)),
        scratch_shapes=[
            pltpu.VMEM((2, COMBINE_ROWS, D_MODEL), BF16),
            pltpu.SemaphoreType.DMA((2,)),
            pltpu.VMEM((ROUTE_TILE, D_MODEL), F32),
        ],
    )
    return pl.pallas_call(
        _combine_kernel,
        grid_spec=spec,
        out_shape=jax.ShapeDtypeStruct((n, D_MODEL), F32),
        compiler_params=_params(("arbitrary",)),
        name="moe_combine",
    )(*lists, dw, x1, mod3, ln_g, ln_b, y)


def _moe(t, x1, mod3, wr, tri, w1, w3, w2, ln_g, ln_b, n_lat, seq, batch):
    n = t.shape[0]
    n_tiles = n // ROUTE_TILE
    n_blocks = -(-(2 * n + N_EXPERTS * (EXPERT_ROWS - 1)) // EXPERT_ROWS) + 1
    i32 = jnp.int32

    route, before, total = _router(t, wr, tri)

    e12 = route[:, 0:2].astype(i32)
    rank = route[:, 4:6].astype(i32)
    counts = total[0, :N_EXPERTS].astype(i32)
    padded = (counts + EXPERT_ROWS - 1) // EXPERT_ROWS * EXPERT_ROWS
    pad_end = jnp.cumsum(padded)
    pad_start = pad_end - padded
    dest = pad_start[e12] + rank
    dst = dest.reshape(n_tiles, ROUTE_TILE, 2).transpose(0, 2, 1)
    dw = jnp.concatenate([dest.astype(F32), route[:, 2:4], route[:, 0:2], jnp.zeros((n, 2), F32)], axis=1)
    block_start = jnp.arange(n_blocks, dtype=i32) * EXPERT_ROWS
    block_e = jnp.minimum(jnp.searchsorted(pad_end, block_start, side="right"), N_EXPERTS - 1).astype(i32)
    block_used = (block_start < pad_end[-1]).astype(i32)

    cb = before[:, 0, :N_EXPERTS].astype(i32)
    ca = jnp.concatenate([cb[1:], counts[None, :]], axis=0)
    lo = pad_start[None, :] + cb
    hi = pad_start[None, :] + ca
    some = hi > lo
    tile_id = jnp.broadcast_to(jnp.arange(n_tiles, dtype=i32)[:, None, None], (n_tiles, N_EXPERTS, 2))
    expert_id = jnp.broadcast_to(jnp.arange(N_EXPERTS, dtype=i32)[None, :, None], (n_tiles, N_EXPERTS, 2))

    b_lo = lo // EXPERT_ROWS
    b_hi = (hi - 1) // EXPERT_ROWS
    off = jnp.minimum((lo - b_lo * EXPERT_ROWS) // 16 * 16, EXPERT_ROWS - GATHER_ROWS)
    g_block = jnp.stack([b_lo, b_lo + 1], axis=-1)
    g_off = jnp.stack([off, jnp.zeros_like(off)], axis=-1)
    g_ok = jnp.stack([some, some & (b_hi > b_lo)], axis=-1)
    key = jnp.where(g_ok, g_block, n_blocks).reshape(-1)
    order = jnp.argsort(key, stable=True)
    g_start = jnp.searchsorted(key[order], jnp.arange(n_blocks + 1, dtype=i32), side="left").astype(i32)
    lists = (g_start, tile_id.reshape(-1)[order], g_off.reshape(-1)[order].astype(i32))
    xs = _gather(lists, t, dst, n_blocks)

    y = _experts(block_e, block_used, xs, w1, w3, w2)

    r0 = lo // 16 * 16
    c_row = jnp.stack([r0, r0 + COMBINE_ROWS], axis=-1)
    c_ok = jnp.stack([some, some & (hi > r0 + COMBINE_ROWS)], axis=-1)
    key = jnp.where(c_ok, tile_id, n_tiles).reshape(-1)
    order = jnp.argsort(key, stable=True)
    c_start = jnp.searchsorted(key[order], jnp.arange(n_tiles + 1, dtype=i32), side="left").astype(i32)
    lists = (c_start, c_row.reshape(-1)[order].astype(i32), expert_id.reshape(-1)[order])
    return _combine(lists, y, dw, x1, mod3, ln_g, ln_b, n_lat, seq, batch)


def _window_counts(n, w):
    t = np.arange(n)
    left = w // 2
    right = w - 1 - left
    return (np.minimum(t + right + 1, n) - np.maximum(t - left, 0)).astype(np.float32)


def _window_matrix(n, w):
    left = w // 2
    right = w - 1 - left
    t = np.arange(n)
    return ((t[None, :] >= t[:, None] - left) & (t[None, :] <= t[:, None] + right)).astype(np.float32)


def _pool_tables(seq, n_ctx):
    rows = seq // GRID_W
    per_tile = 256 // GRID_W
    mc = np.stack([np.kron(np.eye(per_tile, dtype=np.float32), _window_matrix(GRID_W, w)) for w in POOL_WINDOWS])
    inv = np.stack([1.0 / np.outer(_window_counts(rows, w), _window_counts(GRID_W, w)).reshape(seq)
                    for w in POOL_WINDOWS])
    inv = np.broadcast_to(inv[:, :, None], (4, seq, POOL_GROUP)).astype(np.float32)
    m1d = np.stack([_window_matrix(n_ctx, w) for w in POOL_WINDOWS])
    inv1d = np.stack([1.0 / _window_counts(n_ctx, w) for w in POOL_WINDOWS])
    inv1d = np.broadcast_to(inv1d[:, :, None], (4, n_ctx, POOL_GROUP)).astype(np.float32)
    return (jnp.asarray(mc, BF16), jnp.asarray(inv), jnp.asarray(m1d, BF16), jnp.asarray(inv1d))


def _rope_tables(seq):
    t = jnp.arange(seq)
    row = (t // GRID_W).astype(F32)
    col = (t % GRID_W).astype(F32)
    n_freq = RET_QK_DIM // 4
    inv = jnp.exp(-jnp.log(ROPE_BASE) * jnp.arange(n_freq, dtype=F32) / n_freq)
    ang = jnp.concatenate([row[:, None] * inv, col[:, None] * inv], -1)
    cos, sin = jnp.cos(ang), jnp.sin(ang)
    cos2 = jnp.concatenate([cos, cos], -1)
    sin2 = jnp.concatenate([-sin, sin], -1)
    cos2 = jnp.concatenate([cos2, jnp.ones((TOKEN_TILE, RET_QK_DIM), F32)], 0)
    sin2 = jnp.concatenate([sin2, jnp.zeros((TOKEN_TILE, RET_QK_DIM), F32)], 0)
    return cos2, sin2


def kernel(x, c, ctx, c_ctx, ada_w, ada_b, w_in, pool_w, pool_scale, w_pool_out, w_ret_out, ret_decay_logit,
           w_out, ln_mix_g, ln_mix_b, ln_ffn_g, ln_ffn_b, ffn_w1, ffn_w3, ffn_w2, moe_router, moe_w1, moe_w3,
           moe_w2):
    batch, seq, d = x.shape
    n_ctx = ctx.shape[1]
    n_lat = batch * seq
    depth = ada_w.shape[0]
    assert d == D_MODEL and depth == DEPTH and batch < MOD_ROWS
    assert seq % TOKEN_TILE == 0 and (batch * n_ctx) % TOKEN_TILE == 0 and n_lat % n_ctx == 0

    s_in = jnp.zeros((MOD_ROWS, d), F32).at[:batch].set(c).at[batch].set(c_ctx)
    mod = _modulation(s_in, ada_w, ada_b)
    cos2, sin2 = _rope_tables(seq)
    mc, inv, m1d, inv1d = _pool_tables(seq, n_ctx)
    tri = jnp.asarray(np.tril(np.ones((ROUTE_TILE, ROUTE_TILE), np.float32), -1), BF16)

    xs = jnp.concatenate([x.reshape(n_lat, d), ctx.reshape(batch * n_ctx, d)], axis=0)
    for l in range(depth):
        last = l == depth - 1
        mod3 = mod[l].reshape(MOD_ROWS, 1, 6 * d)
        w_l = w_in[l].astype(BF16)
        pa, pb = _in_proj(xs, mod3, cos2, sin2, w_l[:, :OFF_G], w_l[:, OFF_G:], n_lat, seq, batch)
        pw = pool_w[l].astype(BF16)
        ps = pool_scale[l].reshape(1, POOL_WIDTH)
        yp_lat = _pool_lat(pa, mc, inv, pw, ps, batch, seq)
        yp_ctx = _pool_ctx(pa, m1d, inv1d, pw, ps, batch, n_lat, n_ctx)
        dl = jnp.broadcast_to(ret_decay_logit[l].astype(F32)[:, :, None, None],
                              (2, RET_HEADS, RET_CHUNK, RET_V_DIM))
        z_lat, z_ctx = _retention(pa, pb, dl, batch, seq, n_ctx)
        n_out = n_lat if last else xs.shape[0]
        x1, t = _merge(xs, mod3, yp_lat, yp_ctx, z_lat, z_ctx, pb,
                       w_pool_out[l].astype(BF16), w_ret_out[l].astype(BF16), w_out[l].astype(BF16),
                       ln_mix_g[l].reshape(1, d), ln_mix_b[l].reshape(1, d), n_out, n_lat, seq, batch)
        lg = ln_ffn_g[l].reshape(1, d)
        lb = ln_ffn_b[l].reshape(1, d)
        if l % 2 == 0:
            j = l // 2
            xs = _ffn(t, x1, mod3, ffn_w1[j].astype(BF16), ffn_w3[j].astype(BF16), ffn_w2[j].astype(BF16),
                      lg, lb, n_lat, seq, batch)
        else:
            j = l // 2
            wr = jnp.zeros((d, ROUTE_LANES), BF16).at[:, :N_EXPERTS].set(moe_router[j].astype(BF16))
            xs = _moe(t, x1, mod3, wr, tri, moe_w1[j].astype(BF16), moe_w3[j].astype(BF16),
                      moe_w2[j].astype(BF16), lg, lb, n_lat, seq, batch)
    return xs[:n_lat].reshape(batch, seq, d)
```

```python
import functools

import jax
import jax.numpy as jnp
import numpy as np
from jax import lax
from jax.experimental import pallas as pl
from jax.experimental.pallas import tpu as pltpu

F32 = jnp.float32
BF16 = jnp.bfloat16

D_MODEL = 1024
DEPTH = 4
GRID_W = 64
POOL_WINDOWS = (2, 4, 8, 16)
POOL_GROUP = 128
POOL_WIDTH = POOL_GROUP * len(POOL_WINDOWS)
RET_HEADS = 4
RET_QK_DIM = 128
RET_V_DIM = 256
RET_CHUNK = 128
ROPE_BASE = 10000.0
OFF_Q = POOL_WIDTH
OFF_K = OFF_Q + RET_HEADS * RET_QK_DIM
OFF_V = OFF_K + RET_HEADS * RET_QK_DIM
OFF_G = OFF_V + RET_HEADS * RET_V_DIM
WIDTH_B = 3 * D_MODEL
D_FF = 2816
N_EXPERTS = 8
EXPERT_FF = 3584
DEEPNORM_ALPHA = (2 * DEPTH) ** 0.25
LN_EPS = 1e-5
K_SCALE = RET_QK_DIM ** -0.5

MOD_ROWS = 24
TOKEN_TILE = 512
ROUTE_TILE = 256
EXPERT_ROWS = 512
FFN_CHUNK = D_FF // 2
EXPERT_CHUNK = EXPERT_FF // 2
VMEM_LIMIT = 56 * 1024 * 1024


def _dot(a, b):
    return jnp.dot(a, b, preferred_element_type=F32)


def _sigmoid(x):
    return 1.0 / (1.0 + jnp.exp(-x))


def _split_bf16(a):
    hi = a.astype(BF16)
    lo = (a - hi.astype(F32)).astype(BF16)
    return hi, lo


def _layer_norm(v, g, b):
    mean = jnp.mean(v, axis=-1, keepdims=True)
    vc = v - mean
    var = jnp.mean(vc * vc, axis=-1, keepdims=True)
    return vc * lax.rsqrt(var + LN_EPS) * g + b


def _params(sem, vmem=VMEM_LIMIT):
    return pltpu.CompilerParams(dimension_semantics=sem, vmem_limit_bytes=vmem)


def _mod_kernel(s_ref, w_ref, b_ref, o_ref):
    s = s_ref[...]
    s = s * _sigmoid(s)
    s_hi, s_lo = _split_bf16(s)
    w_hi, w_lo = _split_bf16(w_ref[...])
    o_ref[...] = _dot(s_hi, w_hi) + (_dot(s_hi, w_lo) + _dot(s_lo, w_hi)) + b_ref[...]


def _modulation(s_in, ada_w, ada_b):
    depth, d, width = ada_w.shape
    tn = 1536
    return pl.pallas_call(
        _mod_kernel,
        grid=(depth, width // tn),
        in_specs=[
            pl.BlockSpec((MOD_ROWS, d), lambda l, j: (0, 0)),
            pl.BlockSpec((None, d, tn), lambda l, j: (l, 0, j)),
            pl.BlockSpec((None, 1, tn), lambda l, j: (l, 0, j)),
        ],
        out_specs=pl.BlockSpec((None, MOD_ROWS, tn), lambda l, j: (l, 0, j)),
        out_shape=jax.ShapeDtypeStruct((depth, MOD_ROWS, width), F32),
        compiler_params=_params(("arbitrary", "arbitrary")),
        name="adaln_mod",
    )(s_in, ada_w, ada_b.reshape(depth, 1, width))


def _in_kernel(x_ref, sh_ref, sc_ref, cos_ref, sin_ref, wa_ref, wb_ref, pa_ref, pb_ref):
    h = (x_ref[...] * (1.0 + sc_ref[...]) + sh_ref[...]).astype(BF16)
    cos = cos_ref[...]
    sin = sin_ref[...]
    pa_ref[:, 0:OFF_Q] = _dot(h, wa_ref[:, 0:OFF_Q]).astype(BF16)
    for c0, scale in ((OFF_Q, None), (OFF_K, K_SCALE)):
        acc = _dot(h, wa_ref[:, c0:c0 + 512])
        for hh in range(RET_HEADS):
            t = acc[:, hh * RET_QK_DIM:(hh + 1) * RET_QK_DIM]
            r = t * cos + pltpu.roll(t, RET_QK_DIM // 2, 1) * sin
            if scale is not None:
                r = r * scale
            pa_ref[:, c0 + hh * RET_QK_DIM:c0 + (hh + 1) * RET_QK_DIM] = r.astype(BF16)
    for c0 in range(OFF_V, OFF_G, 512):
        pa_ref[:, c0:c0 + 512] = _dot(h, wa_ref[:, c0:c0 + 512]).astype(BF16)
    for c0 in range(0, WIDTH_B, 512):
        pb_ref[:, c0:c0 + 512] = _dot(h, wb_ref[:, c0:c0 + 512]).astype(BF16)


def _in_proj(x, mod3, cos2, sin2, wa, wb, n_lat, seq, batch):
    n = x.shape[0]
    tm = TOKEN_TILE
    nl = n_lat // tm
    tpb = seq // tm

    def row(i):
        return jnp.where(i < nl, i // tpb, batch)

    def rope(i):
        return jnp.where(i < nl, i % tpb, tpb)

    return pl.pallas_call(
        _in_kernel,
        grid=(n // tm,),
        in_specs=[
            pl.BlockSpec((tm, D_MODEL), lambda i: (i, 0)),
            pl.BlockSpec((None, 1, D_MODEL), lambda i: (row(i), 0, 0)),
            pl.BlockSpec((None, 1, D_MODEL), lambda i: (row(i), 0, 1)),
            pl.BlockSpec((tm, RET_QK_DIM), lambda i: (rope(i), 0)),
            pl.BlockSpec((tm, RET_QK_DIM), lambda i: (rope(i), 0)),
            pl.BlockSpec((D_MODEL, OFF_G), lambda i: (0, 0)),
            pl.BlockSpec((D_MODEL, WIDTH_B), lambda i: (0, 0)),
        ],
        out_specs=[
            pl.BlockSpec((tm, OFF_G), lambda i: (i, 0)),
            pl.BlockSpec((tm, WIDTH_B), lambda i: (i, 0)),
        ],
        out_shape=[
            jax.ShapeDtypeStruct((n, OFF_G), BF16),
            jax.ShapeDtypeStruct((n, WIDTH_B), BF16),
        ],
        compiler_params=_params(("arbitrary",)),
        name="in_proj",
    )(x, mod3, mod3, cos2, sin2, wa, wb)


POOL_PAD_ROWS = 8
POOL_ROW_CHUNK = 8


def _pool_kernel(u_ref, mc_ref, inv_ref, pw_ref, ps_ref, o_ref, zp_ref, *, rows):
    seq = rows * GRID_W
    pad = POOL_PAD_ROWS * GRID_W
    chunk = POOL_ROW_CHUNK * GRID_W
    zeros = jnp.zeros((pad, POOL_GROUP), F32)
    for g, w in enumerate(POOL_WINDOWS):
        lanes = slice(g * POOL_GROUP, (g + 1) * POOL_GROUP)
        zp_ref[0:pad, :] = zeros
        zp_ref[pad + seq:pad + seq + pad, :] = zeros
        for c in range(seq // 256):
            zp_ref[pad + c * 256:pad + (c + 1) * 256, :] = _dot(mc_ref[g], u_ref[c * 256:(c + 1) * 256, lanes])
        left = w // 2

        def body(rc, carry, g=g, w=w, left=left, lanes=lanes):
            tok = pl.multiple_of(rc * chunk, chunk)
            acc = zp_ref[pl.ds(tok + pad - left * GRID_W, chunk), :]
            for k in range(1, w):
                acc = acc + zp_ref[pl.ds(tok + pad + (k - left) * GRID_W, chunk), :]
            pooled = acc * inv_ref[g, pl.ds(tok, chunk), :]
            d = (pooled - u_ref[pl.ds(tok, chunk), lanes].astype(F32)).astype(BF16)
            y = _dot(d, pw_ref[g]) * ps_ref[:, lanes]
            o_ref[pl.ds(tok, chunk), lanes] = y.astype(BF16)
            return carry

        lax.fori_loop(0, rows // POOL_ROW_CHUNK, body, 0)


def _pool_lat(pa, mc, inv, pw, ps, batch, seq):
    rows = seq // GRID_W
    return pl.pallas_call(
        functools.partial(_pool_kernel, rows=rows),
        grid=(batch,),
        in_specs=[
            pl.BlockSpec((seq, POOL_WIDTH), lambda b: (b, 0)),
            pl.BlockSpec((4, 256, 256), lambda b: (0, 0, 0)),
            pl.BlockSpec((4, seq, POOL_GROUP), lambda b: (0, 0, 0)),
            pl.BlockSpec((4, POOL_GROUP, POOL_GROUP), lambda b: (0, 0, 0)),
            pl.BlockSpec((1, POOL_WIDTH), lambda b: (0, 0)),
        ],
        out_specs=pl.BlockSpec((seq, POOL_WIDTH), lambda b: (b, 0)),
        out_shape=jax.ShapeDtypeStruct((batch * seq, POOL_WIDTH), BF16),
        scratch_shapes=[pltpu.VMEM(((rows + 2 * POOL_PAD_ROWS) * GRID_W, POOL_GROUP), F32)],
        compiler_params=_params(("arbitrary",)),
        name="pool_lat",
    )(pa, mc, inv, pw, ps)


def _pool_ctx_kernel(u_ref, m_ref, inv_ref, pw_ref, ps_ref, o_ref):
    for g in range(len(POOL_WINDOWS)):
        lanes = slice(g * POOL_GROUP, (g + 1) * POOL_GROUP)
        ug = u_ref[:, lanes]
        pooled = _dot(m_ref[g], ug) * inv_ref[g]
        d = (pooled - ug.astype(F32)).astype(BF16)
        o_ref[:, lanes] = (_dot(d, pw_ref[g]) * ps_ref[:, lanes]).astype(BF16)


def _pool_ctx(pa, m1d, inv1d, pw, ps, batch, n_lat, n_ctx):
    first = n_lat // n_ctx
    return pl.pallas_call(
        _pool_ctx_kernel,
        grid=(batch,),
        in_specs=[
            pl.BlockSpec((n_ctx, POOL_WIDTH), lambda b: (first + b, 0)),
            pl.BlockSpec((4, n_ctx, n_ctx), lambda b: (0, 0, 0)),
            pl.BlockSpec((4, n_ctx, POOL_GROUP), lambda b: (0, 0, 0)),
            pl.BlockSpec((4, POOL_GROUP, POOL_GROUP), lambda b: (0, 0, 0)),
            pl.BlockSpec((1, POOL_WIDTH), lambda b: (0, 0)),
        ],
        out_specs=pl.BlockSpec((n_ctx, POOL_WIDTH), lambda b: (b, 0)),
        out_shape=jax.ShapeDtypeStruct((batch * n_ctx, POOL_WIDTH), BF16),
        compiler_params=_params(("arbitrary",)),
        name="pool_ctx",
    )(pa, m1d, inv1d, pw, ps)


def _log_sigmoid(x):
    return jnp.minimum(x, 0.0) - jnp.log1p(jnp.exp(-jnp.abs(x)))


def _decays(dl, backward):
    c = RET_CHUNK
    lg = _log_sigmoid(dl)
    lgq = lg[:, :RET_QK_DIM]
    ii = lax.broadcasted_iota(jnp.int32, (c, c), 0)
    jj = lax.broadcasted_iota(jnp.int32, (c, c), 1)
    pos = lax.broadcasted_iota(jnp.int32, (c, RET_QK_DIM), 0).astype(F32)
    if backward:
        diff = (jj - ii).astype(F32)
        qdec = jnp.exp(lgq * (c - pos))
        kdec = jnp.exp(lgq * pos)
    else:
        diff = (ii - jj).astype(F32)
        qdec = jnp.exp(lgq * (pos + 1.0))
        kdec = jnp.exp(lgq * (c - 1.0 - pos))
    inner = jnp.where(diff >= 0, jnp.exp(lgq * jnp.maximum(diff, 0.0)), 0.0)
    cdec = jnp.exp(lg * float(c))
    return inner, qdec, kdec, cdec


def _ret_step(qc, kc, vc, s, dec):
    inner, qdec, kdec, cdec = dec
    scores = lax.dot_general(qc, kc, (((1,), (1,)), ((), ())), preferred_element_type=F32) * inner
    o = _dot(scores.astype(BF16), vc) + _dot((qc.astype(F32) * qdec).astype(BF16), s.astype(BF16))
    kd = (kc.astype(F32) * kdec).astype(BF16)
    s_new = s * cdec + lax.dot_general(kd, vc, (((0,), (0,)), ((), ())), preferred_element_type=F32)
    return o, s_new


def _head_norm_gate(o, g):
    mean = jnp.mean(o, axis=-1, keepdims=True)
    oc = o - mean
    var = jnp.mean(oc * oc, axis=-1, keepdims=True)
    on = oc * lax.rsqrt(var + LN_EPS)
    gf = g.astype(F32)
    return (gf * _sigmoid(gf) * on).astype(BF16)


def _ret_kernel(q_ref, k_ref, v_ref, g_ref, qc_ref, kc_ref, vc_ref, gc_ref, dl_ref,
                z_ref, zc_ref, of_ref, ob_ref, oc_ref, s_ref, *, n_chunks, n_ctx_chunks):
    c = RET_CHUNK
    dec_f = _decays(dl_ref[0], False)
    dec_b = _decays(dl_ref[1], True)

    s = jnp.zeros((RET_QK_DIM, RET_V_DIM), F32)
    for j in range(n_ctx_chunks):
        rows = slice(j * c, (j + 1) * c)
        o, s = _ret_step(qc_ref[rows, :], kc_ref[rows, :], vc_ref[rows, :], s, dec_f)
        oc_ref[rows, :] = o
    s_ref[0] = s
    s = jnp.zeros((RET_QK_DIM, RET_V_DIM), F32)
    for j in reversed(range(n_ctx_chunks)):
        rows = slice(j * c, (j + 1) * c)
        o, s = _ret_step(qc_ref[rows, :], kc_ref[rows, :], vc_ref[rows, :], s, dec_b)
        oc_ref[rows, :] = oc_ref[rows, :] + o
    s_ref[1] = s
    for j in range(n_ctx_chunks):
        rows = slice(j * c, (j + 1) * c)
        zc_ref[rows, :] = _head_norm_gate(oc_ref[rows, :], gc_ref[rows, :])

    def scan(t, carry):
        rf = pl.ds(pl.multiple_of(t * c, c), c)
        o, s_new = _ret_step(q_ref[rf, :], k_ref[rf, :], v_ref[rf, :], s_ref[0], dec_f)
        of_ref[rf, :] = o
        s_ref[0] = s_new
        rb = pl.ds(pl.multiple_of((n_chunks - 1 - t) * c, c), c)
        o, s_new = _ret_step(q_ref[rb, :], k_ref[rb, :], v_ref[rb, :], s_ref[1], dec_b)
        ob_ref[rb, :] = o
        s_ref[1] = s_new
        return carry

    lax.fori_loop(0, n_chunks, scan, 0, unroll=4)

    def finish(t, carry):
        r = pl.ds(pl.multiple_of(t * c, c), c)
        z_ref[r, :] = _head_norm_gate(of_ref[r, :] + ob_ref[r, :], g_ref[r, :])
        return carry

    lax.fori_loop(0, n_chunks, finish, 0, unroll=2)


def _retention(pa, pb, dl, batch, seq, n_ctx):
    n_lat = batch * seq
    first = n_lat // n_ctx
    qk0 = OFF_Q // RET_QK_DIM
    kk0 = OFF_K // RET_QK_DIM
    v0 = OFF_V // RET_V_DIM
    return pl.pallas_call(
        functools.partial(_ret_kernel, n_chunks=seq // RET_CHUNK, n_ctx_chunks=n_ctx // RET_CHUNK),
        grid=(batch, RET_HEADS),
        in_specs=[
            pl.BlockSpec((seq, RET_QK_DIM), lambda b, h: (b, qk0 + h)),
            pl.BlockSpec((seq, RET_QK_DIM), lambda b, h: (b, kk0 + h)),
            pl.BlockSpec((seq, RET_V_DIM), lambda b, h: (b, v0 + h)),
            pl.BlockSpec((seq, RET_V_DIM), lambda b, h: (b, h)),
            pl.BlockSpec((n_ctx, RET_QK_DIM), lambda b, h: (first + b, qk0 + h)),
            pl.BlockSpec((n_ctx, RET_QK_DIM), lambda b, h: (first + b, kk0 + h)),
            pl.BlockSpec((n_ctx, RET_V_DIM), lambda b, h: (first + b, v0 + h)),
            pl.BlockSpec((n_ctx, RET_V_DIM), lambda b, h: (first + b, h)),
            pl.BlockSpec((2, None, RET_CHUNK, RET_V_DIM), lambda b, h: (0, h, 0, 0)),
        ],
        out_specs=[
            pl.BlockSpec((seq, RET_V_DIM), lambda b, h: (b, h)),
            pl.BlockSpec((n_ctx, RET_V_DIM), lambda b, h: (b, h)),
        ],
        out_shape=[
            jax.ShapeDtypeStruct((n_lat, RET_HEADS * RET_V_DIM), BF16),
            jax.ShapeDtypeStruct((batch * n_ctx, RET_HEADS * RET_V_DIM), BF16),
        ],
        scratch_shapes=[
            pltpu.VMEM((seq, RET_V_DIM), F32),
            pltpu.VMEM((seq, RET_V_DIM), F32),
            pltpu.VMEM((n_ctx, RET_V_DIM), F32),
            pltpu.VMEM((2, RET_QK_DIM, RET_V_DIM), F32),
        ],
        compiler_params=_params(("arbitrary", "arbitrary")),
        name="retention",
    )(pa, pa, pa, pb, pa, pa, pa, pb, dl)


def _merge_kernel(x_ref, gm_ref, sh_ref, sc_ref, ypl_ref, ypc_ref, zl_ref, zc_ref, gp_ref, gr_ref,
                  wp_ref, wr_ref, wo_ref, lg_ref, lb_ref, x1_ref, t_ref, *, n_lat_tiles):
    is_lat = pl.program_id(0) < n_lat_tiles
    yp = jnp.where(is_lat, ypl_ref[...], ypc_ref[...])
    z = jnp.where(is_lat, zl_ref[...], zc_ref[...])
    y_pool = _dot(yp, wp_ref[...])
    y_ret = _dot(z, wr_ref[...])
    mix = _sigmoid(gp_ref[...].astype(F32)) * y_pool + _sigmoid(gr_ref[...].astype(F32)) * y_ret
    y = _dot(mix.astype(BF16), wo_ref[...])
    x1 = _layer_norm(DEEPNORM_ALPHA * x_ref[...] + gm_ref[...] * y, lg_ref[...], lb_ref[...])
    x1_ref[...] = x1
    t_ref[...] = (x1 * (1.0 + sc_ref[...]) + sh_ref[...]).astype(BF16)


def _merge(x, mod3, yp_lat, yp_ctx, z_lat, z_ctx, pb, wp, wr, wo, ln_g, ln_b, n_out, n_lat, seq, batch):
    tm = TOKEN_TILE
    nl = n_lat // tm
    tpb = seq // tm

    def row(i):
        return jnp.where(i < nl, i // tpb, batch)

    def lat(i):
        return jnp.minimum(i, nl - 1)

    def ctx(i):
        return jnp.maximum(i - nl, 0)

    return pl.pallas_call(
        functools.partial(_merge_kernel, n_lat_tiles=nl),
        grid=(n_out // tm,),
        in_specs=[
            pl.BlockSpec((tm, D_MODEL), lambda i: (i, 0)),
            pl.BlockSpec((None, 1, D_MODEL), lambda i: (row(i), 0, 2)),
            pl.BlockSpec((None, 1, D_MODEL), lambda i: (row(i), 0, 3)),
            pl.BlockSpec((None, 1, D_MODEL), lambda i: (row(i), 0, 4)),
            pl.BlockSpec((tm, POOL_WIDTH), lambda i: (lat(i), 0)),
            pl.BlockSpec((tm, POOL_WIDTH), lambda i: (ctx(i), 0)),
            pl.BlockSpec((tm, D_MODEL), lambda i: (lat(i), 0)),
            pl.BlockSpec((tm, D_MODEL), lambda i: (ctx(i), 0)),
            pl.BlockSpec((tm, D_MODEL), lambda i: (i, 1)),
            pl.BlockSpec((tm, D_MODEL), lambda i: (i, 2)),
            pl.BlockSpec((POOL_WIDTH, D_MODEL), lambda i: (0, 0)),
            pl.BlockSpec((D_MODEL, D_MODEL), lambda i: (0, 0)),
            pl.BlockSpec((D_MODEL, D_MODEL), lambda i: (0, 0)),
            pl.BlockSpec((1, D_MODEL), lambda i: (0, 0)),
            pl.BlockSpec((1, D_MODEL), lambda i: (0, 0)),
        ],
        out_specs=[
            pl.BlockSpec((tm, D_MODEL), lambda i: (i, 0)),
            pl.BlockSpec((tm, D_MODEL), lambda i: (i, 0)),
        ],
        out_shape=[
            jax.ShapeDtypeStruct((n_out, D_MODEL), F32),
            jax.ShapeDtypeStruct((n_out, D_MODEL), BF16),
        ],
        compiler_params=_params(("arbitrary",)),
        name="merge",
    )(x, mod3, mod3, mod3, yp_lat, yp_ctx, z_lat, z_ctx, pb, pb, wp, wr, wo, ln_g, ln_b)


def _ffn_kernel(t_ref, x1_ref, gm_ref, w1_ref, w3_ref, w2_ref, lg_ref, lb_ref, o_ref, acc_ref):
    f = pl.program_id(1)
    t = t_ref[...]
    a = _dot(t, w1_ref[...])
    hmid = (a * _sigmoid(a) * _dot(t, w3_ref[...])).astype(BF16)
    part = _dot(hmid, w2_ref[...])

    @pl.when(f == 0)
    def _():
        acc_ref[...] = part

    @pl.when(f > 0)
    def _():
        acc_ref[...] = acc_ref[...] + part

    @pl.when(f == pl.num_programs(1) - 1)
    def _():
        o_ref[...] = _layer_norm(DEEPNORM_ALPHA * x1_ref[...] + gm_ref[...] * acc_ref[...],
                                 lg_ref[...], lb_ref[...])


def _ffn(t, x1, mod3, w1, w3, w2, ln_g, ln_b, n_lat, seq, batch):
    n = t.shape[0]
    tm = TOKEN_TILE
    nl = n_lat // tm
    tpb = seq // tm

    def row(i):
        return jnp.where(i < nl, i // tpb, batch)

    return pl.pallas_call(
        _ffn_kernel,
        grid=(n // tm, D_FF // FFN_CHUNK),
        in_specs=[
            pl.BlockSpec((tm, D_MODEL), lambda i, f: (i, 0)),
            pl.BlockSpec((tm, D_MODEL), lambda i, f: (i, 0)),
            pl.BlockSpec((None, 1, D_MODEL), lambda i, f: (row(i), 0, 5)),
            pl.BlockSpec((D_MODEL, FFN_CHUNK), lambda i, f: (0, f)),
            pl.BlockSpec((D_MODEL, FFN_CHUNK), lambda i, f: (0, f)),
            pl.BlockSpec((FFN_CHUNK, D_MODEL), lambda i, f: (f, 0)),
            pl.BlockSpec((1, D_MODEL), lambda i, f: (0, 0)),
            pl.BlockSpec((1, D_MODEL), lambda i, f: (0, 0)),
        ],
        out_specs=pl.BlockSpec((tm, D_MODEL), lambda i, f: (i, 0)),
        out_shape=jax.ShapeDtypeStruct((n, D_MODEL), F32),
        scratch_shapes=[pltpu.VMEM((tm, D_MODEL), F32)],
        compiler_params=_params(("arbitrary", "arbitrary")),
        name="ffn_dense",
    )(t, x1, mod3, w1, w3, w2, ln_g, ln_b)


ROUTE_LANES = 128


def _router_kernel(t_ref, wr_ref, tri_ref, route_ref, before_ref, total_ref, run_ref):
    @pl.when(pl.program_id(0) == 0)
    def _():
        run_ref[...] = jnp.zeros_like(run_ref)

    tt = t_ref.shape[0]
    logits = _dot(t_ref[...], wr_ref[...])
    lane = lax.broadcasted_iota(jnp.int32, (tt, ROUTE_LANES), 1)
    neg = jnp.float32(-jnp.inf)
    lg = jnp.where(lane < N_EXPERTS, logits, neg)
    m1 = jnp.max(lg, axis=1, keepdims=True)
    i1 = jnp.min(jnp.where(lg == m1, lane, ROUTE_LANES), axis=1, keepdims=True)
    lg2 = jnp.where(lane == i1, neg, lg)
    m2 = jnp.max(lg2, axis=1, keepdims=True)
    i2 = jnp.min(jnp.where(lg2 == m2, lane, ROUTE_LANES), axis=1, keepdims=True)
    e = jnp.exp(m2 - m1)
    w1 = 1.0 / (1.0 + e)
    w2 = e / (1.0 + e)
    hit1 = lane == i1
    hit2 = lane == i2
    onehot = jnp.where(hit1 | hit2, 1.0, 0.0)
    run = run_ref[...]
    prefix = _dot(tri_ref[...], onehot.astype(BF16)) + run
    r1 = jnp.sum(jnp.where(hit1, prefix, 0.0), axis=1, keepdims=True)
    r2 = jnp.sum(jnp.where(hit2, prefix, 0.0), axis=1, keepdims=True)
    before_ref[...] = run
    run = run + jnp.sum(onehot, axis=0, keepdims=True)
    run_ref[...] = run
    total_ref[...] = run
    out = jnp.where(lane == 0, i1.astype(F32), 0.0)
    out = jnp.where(lane == 1, i2.astype(F32), out)
    out = jnp.where(lane == 2, w1, out)
    out = jnp.where(lane == 3, w2, out)
    out = jnp.where(lane == 4, r1, out)
    out = jnp.where(lane == 5, r2, out)
    route_ref[...] = out


def _router(t, wr, tri):
    n = t.shape[0]
    tt = ROUTE_TILE
    n_tiles = n // tt
    return pl.pallas_call(
        _router_kernel,
        grid=(n_tiles,),
        in_specs=[
            pl.BlockSpec((tt, D_MODEL), lambda i: (i, 0)),
            pl.BlockSpec((D_MODEL, ROUTE_LANES), lambda i: (0, 0)),
            pl.BlockSpec((tt, tt), lambda i: (0, 0)),
        ],
        out_specs=[
            pl.BlockSpec((tt, ROUTE_LANES), lambda i: (i, 0)),
            pl.BlockSpec((None, 1, ROUTE_LANES), lambda i: (i, 0, 0)),
            pl.BlockSpec((1, ROUTE_LANES), lambda i: (0, 0)),
        ],
        out_shape=[
            jax.ShapeDtypeStruct((n, ROUTE_LANES), F32),
            jax.ShapeDtypeStruct((n_tiles, 1, ROUTE_LANES), F32),
            jax.ShapeDtypeStruct((1, ROUTE_LANES), F32),
        ],
        scratch_shapes=[pltpu.VMEM((1, ROUTE_LANES), F32)],
        compiler_params=_params(("arbitrary",)),
        name="moe_router",
    )(t, wr, tri)


DMA_RING = 8
GATHER_ROWS = ROUTE_TILE + 16


def _gather_kernel(es_ref, s_ref, a_ref, dst_ref, t_hbm, o_ref, buf, sem):
    b = pl.program_id(0)
    total = es_ref[pl.num_programs(0)]
    o_ref[...] = jnp.zeros_like(o_ref)

    def tile_copy(j):
        slot = j & (DMA_RING - 1)
        start = pl.multiple_of(s_ref[j] * ROUTE_TILE, ROUTE_TILE)
        return pltpu.make_async_copy(t_hbm.at[pl.ds(start, ROUTE_TILE)], buf.at[slot], sem.at[slot])

    @pl.when(b == 0)
    def _():
        for i in range(DMA_RING - 1):
            @pl.when(i < total)
            def _(i=i):
                tile_copy(i).start()

    def body(j, carry):
        tile_copy(j).wait()

        @pl.when(j + (DMA_RING - 1) < total)
        def _():
            tile_copy(j + (DMA_RING - 1)).start()

        a = pl.multiple_of(a_ref[j], 16)
        rows = b * EXPERT_ROWS + a + lax.broadcasted_iota(jnp.int32, (GATHER_ROWS, ROUTE_TILE), 0)
        d = dst_ref[s_ref[j]]
        sel = (d[0:1, :] == rows) | (d[1:2, :] == rows)
        win = pl.ds(a, GATHER_ROWS)
        picked = _dot(jnp.where(sel, 1.0, 0.0).astype(BF16), buf[j & (DMA_RING - 1)])
        o_ref[win, :] = o_ref[win, :] + picked.astype(BF16)
        return carry

    lax.fori_loop(es_ref[b], es_ref[b + 1], body, 0)


def _gather(lists, t, dst, n_blocks):
    n_tiles = dst.shape[0]
    spec = pltpu.PrefetchScalarGridSpec(
        num_scalar_prefetch=3,
        grid=(n_blocks,),
        in_specs=[
            pl.BlockSpec((n_tiles, 2, ROUTE_TILE), lambda b, es, ss, aa: (0, 0, 0)),
            pl.BlockSpec(memory_space=pl.ANY),
        ],
        out_specs=pl.BlockSpec((EXPERT_ROWS, D_MODEL), lambda b, es, ss, aa: (b, 0)),
        scratch_shapes=[
            pltpu.VMEM((DMA_RING, ROUTE_TILE, D_MODEL), BF16),
            pltpu.SemaphoreType.DMA((DMA_RING,)),
        ],
    )
    return pl.pallas_call(
        _gather_kernel,
        grid_spec=spec,
        out_shape=jax.ShapeDtypeStruct((n_blocks * EXPERT_ROWS, D_MODEL), BF16),
        compiler_params=_params(("arbitrary",)),
        name="moe_gather",
    )(*lists, dst, t)


def _expert_kernel(be_ref, bv_ref, x_ref, w1_ref, w3_ref, w2_ref, o_ref, acc_ref):
    b = pl.program_id(0)
    f = pl.program_id(1)
    last = pl.num_programs(1) - 1
    used = bv_ref[b] == 1

    @pl.when(used)
    def _():
        x = x_ref[...]
        a = _dot(x, w1_ref[...])
        hmid = (a * _sigmoid(a) * _dot(x, w3_ref[...])).astype(BF16)
        part = _dot(hmid, w2_ref[...])

        @pl.when(f == 0)
        def _():
            acc_ref[...] = part

        @pl.when(f > 0)
        def _():
            acc_ref[...] = acc_ref[...] + part

        @pl.when(f == last)
        def _():
            o_ref[...] = acc_ref[...].astype(BF16)

    @pl.when(jnp.logical_not(used) & (f == last))
    def _():
        o_ref[...] = jnp.zeros_like(o_ref)


def _experts(block_e, block_used, xs, w1, w3, w2):
    n_blocks = block_e.shape[0]
    spec = pltpu.PrefetchScalarGridSpec(
        num_scalar_prefetch=2,
        grid=(n_blocks, EXPERT_FF // EXPERT_CHUNK),
        in_specs=[
            pl.BlockSpec((EXPERT_ROWS, D_MODEL), lambda b, f, be, bv: (b, 0)),
            pl.BlockSpec((None, D_MODEL, EXPERT_CHUNK), lambda b, f, be, bv: (be[b], 0, f)),
            pl.BlockSpec((None, D_MODEL, EXPERT_CHUNK), lambda b, f, be, bv: (be[b], 0, f)),
            pl.BlockSpec((None, EXPERT_CHUNK, D_MODEL), lambda b, f, be, bv: (be[b], f, 0)),
        ],
        out_specs=pl.BlockSpec((EXPERT_ROWS, D_MODEL), lambda b, f, be, bv: (b, 0)),
        scratch_shapes=[pltpu.VMEM((EXPERT_ROWS, D_MODEL), F32)],
    )
    return pl.pallas_call(
        _expert_kernel,
        grid_spec=spec,
        out_shape=jax.ShapeDtypeStruct(xs.shape, BF16),
        compiler_params=_params(("arbitrary", "arbitrary")),
        name="moe_experts",
    )(block_e, block_used, xs, w1, w3, w2)


COMBINE_ROWS = ROUTE_TILE
DW_LANES = 4


def _combine_kernel(es_ref, r_ref, e_ref, h_ref, dw_ref, x1_ref, gm_ref, lg_ref, lb_ref, y_hbm, o_ref,
                    buf, sem, acc_ref, d_ref, w_ref):
    s = pl.program_id(0)
    total = es_ref[pl.num_programs(0)]
    acc_ref[...] = jnp.zeros_like(acc_ref)
    for k in range(2):
        d_ref[k] = jnp.broadcast_to(dw_ref[:, k:k + 1].astype(jnp.int32), d_ref.shape[1:])
        w_ref[k] = jnp.broadcast_to(dw_ref[:, 2 + k:3 + k], w_ref.shape[1:])

    def win_copy(j):
        slot = j & (DMA_RING - 1)
        start = pl.multiple_of(r_ref[j], 16)
        return pltpu.make_async_copy(y_hbm.at[pl.ds(start, COMBINE_ROWS)], buf.at[slot], sem.at[slot])

    @pl.when(s == 0)
    def _():
        for i in range(DMA_RING - 1):
            @pl.when(i < total)
            def _(i=i):
                win_copy(i).start()

    def body(j, carry):
        win_copy(j).wait()

        @pl.when(j + (DMA_RING - 1) < total)
        def _():
            win_copy(j + (DMA_RING - 1)).start()

        lo = jnp.maximum(e_ref[j], r_ref[j])
        hi = jnp.minimum(h_ref[j], r_ref[j] + COMBINE_ROWS)
        d1 = d_ref[0]
        d2 = d_ref[1]
        in1 = (d1 >= lo) & (d1 < hi)
        in2 = (d2 >= lo) & (d2 < hi)
        wsel = jnp.where(in1, w_ref[0], 0.0) + jnp.where(in2, w_ref[1], 0.0)
        hit = jnp.where(in1, d1, jnp.where(in2, d2, -1)) - r_ref[j]
        hit = jnp.concatenate([hit] * (COMBINE_ROWS // 128), axis=1)
        col = lax.broadcasted_iota(jnp.int32, (ROUTE_TILE, COMBINE_ROWS), 1)
        picked = _dot(jnp.where(hit == col, 1.0, 0.0).astype(BF16), buf[j & (DMA_RING - 1)])
        acc_ref[...] = acc_ref[...] + jnp.concatenate([wsel] * (D_MODEL // 128), axis=1) * picked
        return carry

    lax.fori_loop(es_ref[s], es_ref[s + 1], body, 0)
    o_ref[...] = _layer_norm(DEEPNORM_ALPHA * x1_ref[...] + gm_ref[...] * acc_ref[...], lg_ref[...], lb_ref[...])


def _combine(lists, y, dw, x1, mod3, ln_g, ln_b, n_lat, seq, batch):
    n = x1.shape[0]
    nl = n_lat // ROUTE_TILE
    tpb = seq // ROUTE_TILE

    def row(s):
        return jnp.where(s < nl, s // tpb, batch)

    spec = pltpu.PrefetchScalarGridSpec(
        num_scalar_prefetch=4,
        grid=(n // ROUTE_TILE,),
        in_specs=[
            pl.BlockSpec((ROUTE_TILE, DW_LANES), lambda s, es, rr, ee, hh: (s, 0)),
            pl.BlockSpec((ROUTE_TILE, D_MODEL), lambda s, es, rr, ee, hh: (s, 0)),
            pl.BlockSpec((None, 1, D_MODEL), lambda s, es, rr, ee, hh: (row(s), 0, 5)),
            pl.BlockSpec((1, D_MODEL), lambda s, es, rr, ee, hh: (0, 0)),
            pl.BlockSpec((1, D_MODEL), lambda s, es, rr, ee, hh: (0, 0)),
            pl.BlockSpec(memory_space=pl.ANY),
        ],
        out_specs=pl.BlockSpec((ROUTE_TILE, D_MODEL), lambda s, es, rr, ee, hh: (s, 0)),
        scratch_shapes=[
            pltpu.VMEM((DMA_RING, COMBINE_ROWS, D_MODEL), BF16),
            pltpu.SemaphoreType.DMA((DMA_RING,)),
            pltpu.VMEM((ROUTE_TILE, D_MODEL), F32),
            pltpu.VMEM((2, ROUTE_TILE, 128), jnp.int32),
            pltpu.VMEM((2, ROUTE_TILE, 128), F32),
        ],
    )
    return pl.pallas_call(
        _combine_kernel,
        grid_spec=spec,
        out_shape=jax.ShapeDtypeStruct((n, D_MODEL), F32),
        compiler_params=_params(("arbitrary",)),
        name="moe_combine",
    )(*lists, dw, x1, mod3, ln_g, ln_b, y)


def _moe(t, x1, mod3, wr, tri, w1, w3, w2, ln_g, ln_b, n_lat, seq, batch):
    n = t.shape[0]
    n_tiles = n // ROUTE_TILE
    n_blocks = -(-(2 * n + N_EXPERTS * (EXPERT_ROWS - 1)) // EXPERT_ROWS) + 1
    i32 = jnp.int32

    route, before, total = _router(t, wr, tri)

    e12 = route[:, 0:2].astype(i32)
    rank = route[:, 4:6].astype(i32)
    counts = total[0, :N_EXPERTS].astype(i32)
    padded = (counts + EXPERT_ROWS - 1) // EXPERT_ROWS * EXPERT_ROWS
    pad_end = jnp.cumsum(padded)
    pad_start = pad_end - padded
    dest = pad_start[e12] + rank
    dst = dest.reshape(n_tiles, ROUTE_TILE, 2).transpose(0, 2, 1)
    dw = jnp.concatenate([dest.astype(F32), route[:, 2:4]], axis=1)
    block_start = jnp.arange(n_blocks, dtype=i32) * EXPERT_ROWS
    block_e = jnp.minimum(jnp.searchsorted(pad_end, block_start, side="right"), N_EXPERTS - 1).astype(i32)
    block_used = (block_start < pad_end[-1]).astype(i32)

    cb = before[:, 0, :N_EXPERTS].astype(i32)
    ca = jnp.concatenate([cb[1:], counts[None, :]], axis=0)
    lo = pad_start[None, :] + cb
    hi = pad_start[None, :] + ca
    some = hi > lo
    tile_id = jnp.broadcast_to(jnp.arange(n_tiles, dtype=i32)[:, None, None], (n_tiles, N_EXPERTS, 2))

    b_lo = lo // EXPERT_ROWS
    b_hi = (hi - 1) // EXPERT_ROWS
    off = jnp.minimum((lo - b_lo * EXPERT_ROWS) // 16 * 16, EXPERT_ROWS - GATHER_ROWS)
    g_block = jnp.stack([b_lo, b_lo + 1], axis=-1)
    g_off = jnp.stack([off, jnp.zeros_like(off)], axis=-1)
    g_ok = jnp.stack([some, some & (b_hi > b_lo)], axis=-1)
    key = jnp.where(g_ok, g_block, n_blocks).reshape(-1)
    order = jnp.argsort(key, stable=True)
    g_start = jnp.searchsorted(key[order], jnp.arange(n_blocks + 1, dtype=i32), side="left").astype(i32)
    lists = (g_start, tile_id.reshape(-1)[order], g_off.reshape(-1)[order].astype(i32))
    xs = _gather(lists, t, dst, n_blocks)

    y = _experts(block_e, block_used, xs, w1, w3, w2)

    r0 = lo // 16 * 16
    c_row = jnp.stack([r0, r0 + COMBINE_ROWS], axis=-1)
    c_ok = jnp.stack([some, some & (hi > r0 + COMBINE_ROWS)], axis=-1)
    key = jnp.where(c_ok, tile_id, n_tiles).reshape(-1)
    order = jnp.argsort(key, stable=True)
    c_start = jnp.searchsorted(key[order], jnp.arange(n_tiles + 1, dtype=i32), side="left").astype(i32)
    run_lo = jnp.broadcast_to(lo[:, :, None], c_row.shape).reshape(-1)[order].astype(i32)
    run_hi = jnp.broadcast_to(hi[:, :, None], c_row.shape).reshape(-1)[order].astype(i32)
    lists = (c_start, c_row.reshape(-1)[order].astype(i32), run_lo, run_hi)
    return _combine(lists, y, dw, x1, mod3, ln_g, ln_b, n_lat, seq, batch)


def _window_counts(n, w):
    t = np.arange(n)
    left = w // 2
    right = w - 1 - left
    return (np.minimum(t + right + 1, n) - np.maximum(t - left, 0)).astype(np.float32)


def _window_matrix(n, w):
    left = w // 2
    right = w - 1 - left
    t = np.arange(n)
    return ((t[None, :] >= t[:, None] - left) & (t[None, :] <= t[:, None] + right)).astype(np.float32)


def _pool_tables(seq, n_ctx):
    rows = seq // GRID_W
    per_tile = 256 // GRID_W
    mc = np.stack([np.kron(np.eye(per_tile, dtype=np.float32), _window_matrix(GRID_W, w)) for w in POOL_WINDOWS])
    inv = np.stack([1.0 / np.outer(_window_counts(rows, w), _window_counts(GRID_W, w)).reshape(seq)
                    for w in POOL_WINDOWS])
    inv = np.broadcast_to(inv[:, :, None], (4, seq, POOL_GROUP)).astype(np.float32)
    m1d = np.stack([_window_matrix(n_ctx, w) for w in POOL_WINDOWS])
    inv1d = np.stack([1.0 / _window_counts(n_ctx, w) for w in POOL_WINDOWS])
    inv1d = np.broadcast_to(inv1d[:, :, None], (4, n_ctx, POOL_GROUP)).astype(np.float32)
    return (jnp.asarray(mc, BF16), jnp.asarray(inv), jnp.asarray(m1d, BF16), jnp.asarray(inv1d))


def _rope_tables(seq):
    t = jnp.arange(seq)
    row = (t // GRID_W).astype(F32)
    col = (t % GRID_W).astype(F32)
    n_freq = RET_QK_DIM // 4
    inv = jnp.exp(-jnp.log(ROPE_BASE) * jnp.arange(n_freq, dtype=F32) / n_freq)
    ang = jnp.concatenate([row[:, None] * inv, col[:, None] * inv], -1)
    cos, sin = jnp.cos(ang), jnp.sin(ang)
    cos2 = jnp.concatenate([cos, cos], -1)
    sin2 = jnp.concatenate([-sin, sin], -1)
    cos2 = jnp.concatenate([cos2, jnp.ones((TOKEN_TILE, RET_QK_DIM), F32)], 0)
    sin2 = jnp.concatenate([sin2, jnp.zeros((TOKEN_TILE, RET_QK_DIM), F32)], 0)
    return cos2, sin2


def kernel(x, c, ctx, c_ctx, ada_w, ada_b, w_in, pool_w, pool_scale, w_pool_out, w_ret_out, ret_decay_logit,
           w_out, ln_mix_g, ln_mix_b, ln_ffn_g, ln_ffn_b, ffn_w1, ffn_w3, ffn_w2, moe_router, moe_w1, moe_w3,
           moe_w2):
    batch, seq, d = x.shape
    n_ctx = ctx.shape[1]
    n_lat = batch * seq
    depth = ada_w.shape[0]
    assert d == D_MODEL and depth == DEPTH and batch < MOD_ROWS
    assert seq % TOKEN_TILE == 0 and (batch * n_ctx) % TOKEN_TILE == 0 and n_lat % n_ctx == 0

    s_in = jnp.zeros((MOD_ROWS, d), F32).at[:batch].set(c).at[batch].set(c_ctx)
    mod = _modulation(s_in, ada_w, ada_b)
    cos2, sin2 = _rope_tables(seq)
    mc, inv, m1d, inv1d = _pool_tables(seq, n_ctx)
    tri = jnp.asarray(np.tril(np.ones((ROUTE_TILE, ROUTE_TILE), np.float32), -1), BF16)

    xs = jnp.concatenate([x.reshape(n_lat, d), ctx.reshape(batch * n_ctx, d)], axis=0)
    for l in range(depth):
        last = l == depth - 1
        mod3 = mod[l].reshape(MOD_ROWS, 1, 6 * d)
        w_l = w_in[l].astype(BF16)
        pa, pb = _in_proj(xs, mod3, cos2, sin2, w_l[:, :OFF_G], w_l[:, OFF_G:], n_lat, seq, batch)
        pw = pool_w[l].astype(BF16)
        ps = pool_scale[l].reshape(1, POOL_WIDTH)
        yp_lat = _pool_lat(pa, mc, inv, pw, ps, batch, seq)
        yp_ctx = _pool_ctx(pa, m1d, inv1d, pw, ps, batch, n_lat, n_ctx)
        dl = jnp.broadcast_to(ret_decay_logit[l].astype(F32)[:, :, None, None],
                              (2, RET_HEADS, RET_CHUNK, RET_V_DIM))
        z_lat, z_ctx = _retention(pa, pb, dl, batch, seq, n_ctx)
        n_out = n_lat if last else xs.shape[0]
        x1, t = _merge(xs, mod3, yp_lat, yp_ctx, z_lat, z_ctx, pb,
                       w_pool_out[l].astype(BF16), w_ret_out[l].astype(BF16), w_out[l].astype(BF16),
                       ln_mix_g[l].reshape(1, d), ln_mix_b[l].reshape(1, d), n_out, n_lat, seq, batch)
        lg = ln_ffn_g[l].reshape(1, d)
        lb = ln_ffn_b[l].reshape(1, d)
        if l % 2 == 0:
            j = l // 2
            xs = _ffn(t, x1, mod3, ffn_w1[j].astype(BF16), ffn_w3[j].astype(BF16), ffn_w2[j].astype(BF16),
                      lg, lb, n_lat, seq, batch)
        else:
            j = l // 2
            wr = jnp.zeros((d, ROUTE_LANES), BF16).at[:, :N_EXPERTS].set(moe_router[j].astype(BF16))
            xs = _moe(t, x1, mod3, wr, tri, moe_w1[j].astype(BF16), moe_w3[j].astype(BF16),
                      moe_w2[j].astype(BF16), lg, lb, n_lat, seq, batch)
    return xs[:n_lat].reshape(batch, seq, d)
```

```python
import functools

import jax
import jax.numpy as jnp
import numpy as np
from jax import lax
from jax.experimental import pallas as pl
from jax.experimental.pallas import tpu as pltpu

F32 = jnp.float32
BF16 = jnp.bfloat16

D_MODEL = 1024
DEPTH = 4
GRID_W = 64
POOL_WINDOWS = (2, 4, 8, 16)
POOL_GROUP = 128
POOL_WIDTH = POOL_GROUP * len(POOL_WINDOWS)
RET_HEADS = 4
RET_QK_DIM = 128
RET_V_DIM = 256
RET_CHUNK = 128
ROPE_BASE = 10000.0
OFF_Q = POOL_WIDTH
OFF_K = OFF_Q + RET_HEADS * RET_QK_DIM
OFF_V = OFF_K + RET_HEADS * RET_QK_DIM
OFF_G = OFF_V + RET_HEADS * RET_V_DIM
WIDTH_B = 3 * D_MODEL
D_FF = 2816
N_EXPERTS = 8
EXPERT_FF = 3584
DEEPNORM_ALPHA = (2 * DEPTH) ** 0.25
LN_EPS = 1e-5
K_SCALE = RET_QK_DIM ** -0.5

MOD_ROWS = 24
TOKEN_TILE = 512
ROUTE_TILE = 256
ROUTER_STEP = 512
EXPERT_ROWS = 512
FFN_CHUNK = D_FF // 2
EXPERT_CHUNK = EXPERT_FF // 2
VMEM_LIMIT = 56 * 1024 * 1024


def _dot(a, b):
    return jnp.dot(a, b, preferred_element_type=F32)


def _sigmoid(x):
    return 1.0 / (1.0 + jnp.exp(-x))


def _split_bf16(a):
    hi = a.astype(BF16)
    lo = (a - hi.astype(F32)).astype(BF16)
    return hi, lo


def _layer_norm(v, g, b):
    mean = jnp.mean(v, axis=-1, keepdims=True)
    vc = v - mean
    var = jnp.mean(vc * vc, axis=-1, keepdims=True)
    return vc * lax.rsqrt(var + LN_EPS) * g + b


def _params(sem, vmem=VMEM_LIMIT):
    return pltpu.CompilerParams(dimension_semantics=sem, vmem_limit_bytes=vmem)


def _mod_kernel(s_ref, w_ref, b_ref, o_ref):
    s = s_ref[...]
    s = s * _sigmoid(s)
    s_hi, s_lo = _split_bf16(s)
    w_hi, w_lo = _split_bf16(w_ref[...])
    o_ref[...] = _dot(s_hi, w_hi) + (_dot(s_hi, w_lo) + _dot(s_lo, w_hi)) + b_ref[...]


def _modulation(s_in, ada_w, ada_b):
    depth, d, width = ada_w.shape
    tn = 1536
    return pl.pallas_call(
        _mod_kernel,
        grid=(depth, width // tn),
        in_specs=[
            pl.BlockSpec((MOD_ROWS, d), lambda l, j: (0, 0)),
            pl.BlockSpec((None, d, tn), lambda l, j: (l, 0, j)),
            pl.BlockSpec((None, 1, tn), lambda l, j: (l, 0, j)),
        ],
        out_specs=pl.BlockSpec((None, MOD_ROWS, tn), lambda l, j: (l, 0, j)),
        out_shape=jax.ShapeDtypeStruct((depth, MOD_ROWS, width), F32),
        compiler_params=_params(("arbitrary", "arbitrary")),
        name="adaln_mod",
    )(s_in, ada_w, ada_b.reshape(depth, 1, width))


IN_COLS = 512
IN_BLOCKS = (OFF_G + WIDTH_B) // IN_COLS


def _in_kernel(xl_ref, xc_ref, sh_ref, sc_ref, cos_ref, sin_ref, *refs, n_lat_tiles):
    w_refs = refs[:IN_BLOCKS]
    pa_ref, pb_ref = refs[IN_BLOCKS:]
    x = jnp.where(pl.program_id(0) < n_lat_tiles, xl_ref[...], xc_ref[...])
    h = (x * (1.0 + sc_ref[...]) + sh_ref[...]).astype(BF16)
    cos = cos_ref[...]
    sin = sin_ref[...]
    for blk in range(IN_BLOCKS):
        c0 = blk * IN_COLS
        acc = _dot(h, w_refs[blk][...])
        if c0 in (OFF_Q, OFF_K):
            for hh in range(RET_HEADS):
                t = acc[:, hh * RET_QK_DIM:(hh + 1) * RET_QK_DIM]
                r = t * cos + pltpu.roll(t, RET_QK_DIM // 2, 1) * sin
                if c0 == OFF_K:
                    r = r * K_SCALE
                pa_ref[:, c0 + hh * RET_QK_DIM:c0 + (hh + 1) * RET_QK_DIM] = r.astype(BF16)
        elif c0 < OFF_G:
            pa_ref[:, c0:c0 + IN_COLS] = acc.astype(BF16)
        else:
            pb_ref[:, c0 - OFF_G:c0 - OFF_G + IN_COLS] = acc.astype(BF16)


def _tile_maps(n_lat, seq, batch, tile, ctx_offset):
    nl = n_lat // tile
    tpb = seq // tile

    def row(i):
        return jnp.where(i < nl, i // tpb, batch)

    def lat(i):
        return jnp.minimum(i, nl - 1)

    def ctx(i):
        return jnp.maximum(i - nl, 0) + ctx_offset

    return nl, tpb, row, lat, ctx


def _in_proj(x_lat, x_ctx, ctx_offset, mod3, cos2, sin2, w_in, layer, n_tok, n_lat, seq, batch):
    tm = TOKEN_TILE
    nl, tpb, row, lat, ctx = _tile_maps(n_lat, seq, batch, tm, ctx_offset)

    def rope(i):
        return jnp.where(i < nl, i % tpb, tpb)

    def w_spec(blk):
        return pl.BlockSpec((None, D_MODEL, IN_COLS), lambda i: (layer, 0, blk))

    return pl.pallas_call(
        functools.partial(_in_kernel, n_lat_tiles=nl),
        grid=(n_tok // tm,),
        in_specs=[
            pl.BlockSpec((tm, D_MODEL), lambda i: (lat(i), 0)),
            pl.BlockSpec((tm, D_MODEL), lambda i: (ctx(i), 0)),
            pl.BlockSpec((None, 1, D_MODEL), lambda i: (row(i), 0, 0)),
            pl.BlockSpec((None, 1, D_MODEL), lambda i: (row(i), 0, 1)),
            pl.BlockSpec((tm, RET_QK_DIM), lambda i: (rope(i), 0)),
            pl.BlockSpec((tm, RET_QK_DIM), lambda i: (rope(i), 0)),
        ] + [w_spec(blk) for blk in range(IN_BLOCKS)],
        out_specs=[
            pl.BlockSpec((tm, OFF_G), lambda i: (i, 0)),
            pl.BlockSpec((tm, WIDTH_B), lambda i: (i, 0)),
        ],
        out_shape=[
            jax.ShapeDtypeStruct((n_tok, OFF_G), BF16),
            jax.ShapeDtypeStruct((n_tok, WIDTH_B), BF16),
        ],
        compiler_params=_params(("arbitrary",)),
        name="in_proj",
    )(x_lat, x_ctx, mod3, mod3, cos2, sin2, *([w_in] * IN_BLOCKS))


POOL_PAD_ROWS = 8
POOL_ROW_CHUNK = 8


def _pool_kernel(u_ref, mc_ref, inv_ref, pw_ref, ps_ref, o_ref, zp_ref, *, rows):
    seq = rows * GRID_W
    pad = POOL_PAD_ROWS * GRID_W
    chunk = POOL_ROW_CHUNK * GRID_W
    zeros = jnp.zeros((pad, POOL_GROUP), F32)
    for g, w in enumerate(POOL_WINDOWS):
        lanes = slice(g * POOL_GROUP, (g + 1) * POOL_GROUP)
        zp_ref[0:pad, :] = zeros
        zp_ref[pad + seq:pad + seq + pad, :] = zeros
        for c in range(seq // 256):
            zp_ref[pad + c * 256:pad + (c + 1) * 256, :] = _dot(mc_ref[g], u_ref[c * 256:(c + 1) * 256, lanes])
        left = w // 2

        def body(rc, carry, g=g, w=w, left=left, lanes=lanes):
            tok = pl.multiple_of(rc * chunk, chunk)
            acc = zp_ref[pl.ds(tok + pad - left * GRID_W, chunk), :]
            for k in range(1, w):
                acc = acc + zp_ref[pl.ds(tok + pad + (k - left) * GRID_W, chunk), :]
            pooled = acc * inv_ref[g, pl.ds(tok, chunk), :]
            d = (pooled - u_ref[pl.ds(tok, chunk), lanes].astype(F32)).astype(BF16)
            y = _dot(d, pw_ref[g]) * ps_ref[:, lanes]
            o_ref[pl.ds(tok, chunk), lanes] = y.astype(BF16)
            return carry

        lax.fori_loop(0, rows // POOL_ROW_CHUNK, body, 0)


def _pool_lat(pa, mc, inv, pw, ps, layer, batch, seq):
    rows = seq // GRID_W
    return pl.pallas_call(
        functools.partial(_pool_kernel, rows=rows),
        grid=(batch,),
        in_specs=[
            pl.BlockSpec((seq, POOL_WIDTH), lambda b: (b, 0)),
            pl.BlockSpec((4, 256, 256), lambda b: (0, 0, 0)),
            pl.BlockSpec((4, seq, POOL_GROUP), lambda b: (0, 0, 0)),
            pl.BlockSpec((None, 4, POOL_GROUP, POOL_GROUP), lambda b: (layer, 0, 0, 0)),
            pl.BlockSpec((None, 1, POOL_WIDTH), lambda b: (layer, 0, 0)),
        ],
        out_specs=pl.BlockSpec((seq, POOL_WIDTH), lambda b: (b, 0)),
        out_shape=jax.ShapeDtypeStruct((batch * seq, POOL_WIDTH), BF16),
        scratch_shapes=[pltpu.VMEM(((rows + 2 * POOL_PAD_ROWS) * GRID_W, POOL_GROUP), F32)],
        compiler_params=_params(("arbitrary",)),
        name="pool_lat",
    )(pa, mc, inv, pw, ps)


def _pool_ctx_kernel(u_ref, m_ref, inv_ref, pw_ref, ps_ref, o_ref):
    for g in range(len(POOL_WINDOWS)):
        lanes = slice(g * POOL_GROUP, (g + 1) * POOL_GROUP)
        ug = u_ref[:, lanes]
        pooled = _dot(m_ref[g], ug) * inv_ref[g]
        d = (pooled - ug.astype(F32)).astype(BF16)
        o_ref[:, lanes] = (_dot(d, pw_ref[g]) * ps_ref[:, lanes]).astype(BF16)


def _pool_ctx(pa, m1d, inv1d, pw, ps, layer, batch, n_lat, n_ctx):
    first = n_lat // n_ctx
    return pl.pallas_call(
        _pool_ctx_kernel,
        grid=(batch,),
        in_specs=[
            pl.BlockSpec((n_ctx, POOL_WIDTH), lambda b: (first + b, 0)),
            pl.BlockSpec((4, n_ctx, n_ctx), lambda b: (0, 0, 0)),
            pl.BlockSpec((4, n_ctx, POOL_GROUP), lambda b: (0, 0, 0)),
            pl.BlockSpec((None, 4, POOL_GROUP, POOL_GROUP), lambda b: (layer, 0, 0, 0)),
            pl.BlockSpec((None, 1, POOL_WIDTH), lambda b: (layer, 0, 0)),
        ],
        out_specs=pl.BlockSpec((n_ctx, POOL_WIDTH), lambda b: (b, 0)),
        out_shape=jax.ShapeDtypeStruct((batch * n_ctx, POOL_WIDTH), BF16),
        compiler_params=_params(("arbitrary",)),
        name="pool_ctx",
    )(pa, m1d, inv1d, pw, ps)


def _log_sigmoid(x):
    return jnp.minimum(x, 0.0) - jnp.log1p(jnp.exp(-jnp.abs(x)))


def _decays(dl, backward):
    c = RET_CHUNK
    lg = _log_sigmoid(dl)
    lgq = lg[:, :RET_QK_DIM]
    ii = lax.broadcasted_iota(jnp.int32, (c, c), 0)
    jj = lax.broadcasted_iota(jnp.int32, (c, c), 1)
    pos = lax.broadcasted_iota(jnp.int32, (c, RET_QK_DIM), 0).astype(F32)
    if backward:
        diff = (jj - ii).astype(F32)
        qdec = jnp.exp(lgq * (c - pos))
        kdec = jnp.exp(lgq * pos)
    else:
        diff = (ii - jj).astype(F32)
        qdec = jnp.exp(lgq * (pos + 1.0))
        kdec = jnp.exp(lgq * (c - 1.0 - pos))
    inner = jnp.where(diff >= 0, jnp.exp(lgq * jnp.maximum(diff, 0.0)), 0.0)
    cdec = jnp.exp(lg * float(c))
    return inner, qdec, kdec, cdec


def _chunk_kv(kc, vc, kdec):
    kd = (kc.astype(F32) * kdec).astype(BF16)
    return lax.dot_general(kd, vc, (((0,), (0,)), ((), ())), preferred_element_type=F32)


def _chunk_out(qc, kc, vc, gc, states, mask, qdec2):
    scores = lax.dot_general(qc, kc, (((1,), (1,)), ((), ())), preferred_element_type=F32) * mask
    qf = qc.astype(F32)
    qd = (jnp.concatenate([qf, qf], axis=1) * qdec2).astype(BF16)
    o = _dot(scores.astype(BF16), vc) + _dot(qd, states)
    mean = jnp.mean(o, axis=-1, keepdims=True)
    oc = o - mean
    var = jnp.mean(oc * oc, axis=-1, keepdims=True)
    on = oc * lax.rsqrt(var + LN_EPS)
    gf = gc.astype(F32)
    return (gf * _sigmoid(gf) * on).astype(BF16)


def _ret_kernel(q_ref, k_ref, v_ref, g_ref, qc_ref, kc_ref, vc_ref, gc_ref, dl_ref,
                z_ref, zc_ref, st_ref, stc_ref, run_ref, *, n_chunks, n_ctx_chunks):
    c = RET_CHUNK
    dk = RET_QK_DIM
    inner_f, qdec_f, kdec_f, cdec_f = _decays(dl_ref[0], False)
    inner_b, qdec_b, kdec_b, cdec_b = _decays(dl_ref[1], True)
    mask = inner_f + inner_b
    qdec2 = jnp.concatenate([qdec_f, qdec_b], axis=1)

    def rows(j):
        return slice(j * c, (j + 1) * c)

    s = jnp.zeros((dk, RET_V_DIM), F32)
    for j in range(n_ctx_chunks):
        stc_ref[j, 0:dk, :] = s.astype(BF16)
        s = s * cdec_f + _chunk_kv(kc_ref[rows(j), :], vc_ref[rows(j), :], kdec_f)
    run_ref[0] = s
    s = jnp.zeros((dk, RET_V_DIM), F32)
    for j in reversed(range(n_ctx_chunks)):
        stc_ref[j, dk:2 * dk, :] = s.astype(BF16)
        s = s * cdec_b + _chunk_kv(kc_ref[rows(j), :], vc_ref[rows(j), :], kdec_b)
    run_ref[1] = s
    for j in range(n_ctx_chunks):
        zc_ref[rows(j), :] = _chunk_out(qc_ref[rows(j), :], kc_ref[rows(j), :], vc_ref[rows(j), :],
                                        gc_ref[rows(j), :], stc_ref[j], mask, qdec2)

    def scan(t, carry):
        rf = pl.ds(pl.multiple_of(t * c, c), c)
        sf = run_ref[0]
        st_ref[t, 0:dk, :] = sf.astype(BF16)
        run_ref[0] = sf * cdec_f + _chunk_kv(k_ref[rf, :], v_ref[rf, :], kdec_f)
        tb = n_chunks - 1 - t
        rb = pl.ds(pl.multiple_of(tb * c, c), c)
        sb = run_ref[1]
        st_ref[tb, dk:2 * dk, :] = sb.astype(BF16)
        run_ref[1] = sb * cdec_b + _chunk_kv(k_ref[rb, :], v_ref[rb, :], kdec_b)
        return carry

    lax.fori_loop(0, n_chunks, scan, 0, unroll=4)

    def emit(t, carry):
        r = pl.ds(pl.multiple_of(t * c, c), c)
        z_ref[r, :] = _chunk_out(q_ref[r, :], k_ref[r, :], v_ref[r, :], g_ref[r, :], st_ref[t], mask, qdec2)
        return carry

    lax.fori_loop(0, n_chunks, emit, 0, unroll=8)


def _retention(pa, pb, dl, batch, seq, n_ctx):
    n_lat = batch * seq
    first = n_lat // n_ctx
    qk0 = OFF_Q // RET_QK_DIM
    kk0 = OFF_K // RET_QK_DIM
    v0 = OFF_V // RET_V_DIM
    return pl.pallas_call(
        functools.partial(_ret_kernel, n_chunks=seq // RET_CHUNK, n_ctx_chunks=n_ctx // RET_CHUNK),
        grid=(batch, RET_HEADS),
        in_specs=[
            pl.BlockSpec((seq, RET_QK_DIM), lambda b, h: (b, qk0 + h)),
            pl.BlockSpec((seq, RET_QK_DIM), lambda b, h: (b, kk0 + h)),
            pl.BlockSpec((seq, RET_V_DIM), lambda b, h: (b, v0 + h)),
            pl.BlockSpec((seq, RET_V_DIM), lambda b, h: (b, h)),
            pl.BlockSpec((n_ctx, RET_QK_DIM), lambda b, h: (first + b, qk0 + h)),
            pl.BlockSpec((n_ctx, RET_QK_DIM), lambda b, h: (first + b, kk0 + h)),
            pl.BlockSpec((n_ctx, RET_V_DIM), lambda b, h: (first + b, v0 + h)),
            pl.BlockSpec((n_ctx, RET_V_DIM), lambda b, h: (first + b, h)),
            pl.BlockSpec((2, None, RET_CHUNK, RET_V_DIM), lambda b, h: (0, h, 0, 0)),
        ],
        out_specs=[
            pl.BlockSpec((seq, RET_V_DIM), lambda b, h: (b, h)),
            pl.BlockSpec((n_ctx, RET_V_DIM), lambda b, h: (b, h)),
        ],
        out_shape=[
            jax.ShapeDtypeStruct((n_lat, RET_HEADS * RET_V_DIM), BF16),
            jax.ShapeDtypeStruct((batch * n_ctx, RET_HEADS * RET_V_DIM), BF16),
        ],
        scratch_shapes=[
            pltpu.VMEM((seq // RET_CHUNK, 2 * RET_QK_DIM, RET_V_DIM), BF16),
            pltpu.VMEM((n_ctx // RET_CHUNK, 2 * RET_QK_DIM, RET_V_DIM), BF16),
            pltpu.VMEM((2, RET_QK_DIM, RET_V_DIM), F32),
        ],
        compiler_params=_params(("arbitrary", "arbitrary")),
        name="retention",
    )(pa, pa, pa, pb, pa, pa, pa, pb, dl)


MIX_COLS = 256


def _merge_kernel(xl_ref, xc_ref, gm_ref, sh_ref, sc_ref, ypl_ref, ypc_ref, zl_ref, zc_ref, gp_ref, gr_ref,
                  wp_ref, wr_ref, wo_ref, lg_ref, lb_ref, x1_ref, t_ref, mix_ref, *, n_lat_tiles):
    is_lat = pl.program_id(0) < n_lat_tiles
    yp = jnp.where(is_lat, ypl_ref[...], ypc_ref[...])
    z = jnp.where(is_lat, zl_ref[...], zc_ref[...])
    x = jnp.where(is_lat, xl_ref[...], xc_ref[...])
    for c0 in range(0, D_MODEL, MIX_COLS):
        cols = slice(c0, c0 + MIX_COLS)
        y_pool = _dot(yp, wp_ref[:, cols])
        y_ret = _dot(z, wr_ref[:, cols])
        mix = _sigmoid(gp_ref[:, cols].astype(F32)) * y_pool + _sigmoid(gr_ref[:, cols].astype(F32)) * y_ret
        mix_ref[:, cols] = mix.astype(BF16)
    y = _dot(mix_ref[...], wo_ref[...])
    x1 = _layer_norm(DEEPNORM_ALPHA * x + gm_ref[...] * y, lg_ref[...], lb_ref[...])
    x1_ref[...] = x1
    t_ref[...] = (x1 * (1.0 + sc_ref[...]) + sh_ref[...]).astype(BF16)


def _merge(x_lat, x_ctx, ctx_offset, mod3, yp_lat, yp_ctx, z_lat, z_ctx, pb, wp, wr, wo, ln_g, ln_b, layer,
           n_out, n_lat, seq, batch):
    tm = TOKEN_TILE
    nl, _, row, lat, ctx = _tile_maps(n_lat, seq, batch, tm, 0)
    _, _, _, _, xctx = _tile_maps(n_lat, seq, batch, tm, ctx_offset)

    return pl.pallas_call(
        functools.partial(_merge_kernel, n_lat_tiles=nl),
        grid=(n_out // tm,),
        in_specs=[
            pl.BlockSpec((tm, D_MODEL), lambda i: (lat(i), 0)),
            pl.BlockSpec((tm, D_MODEL), lambda i: (xctx(i), 0)),
            pl.BlockSpec((None, 1, D_MODEL), lambda i: (row(i), 0, 2)),
            pl.BlockSpec((None, 1, D_MODEL), lambda i: (row(i), 0, 3)),
            pl.BlockSpec((None, 1, D_MODEL), lambda i: (row(i), 0, 4)),
            pl.BlockSpec((tm, POOL_WIDTH), lambda i: (lat(i), 0)),
            pl.BlockSpec((tm, POOL_WIDTH), lambda i: (ctx(i), 0)),
            pl.BlockSpec((tm, D_MODEL), lambda i: (lat(i), 0)),
            pl.BlockSpec((tm, D_MODEL), lambda i: (ctx(i), 0)),
            pl.BlockSpec((tm, D_MODEL), lambda i: (i, 1)),
            pl.BlockSpec((tm, D_MODEL), lambda i: (i, 2)),
            pl.BlockSpec((None, POOL_WIDTH, D_MODEL), lambda i: (layer, 0, 0)),
            pl.BlockSpec((None, D_MODEL, D_MODEL), lambda i: (layer, 0, 0)),
            pl.BlockSpec((None, D_MODEL, D_MODEL), lambda i: (layer, 0, 0)),
            pl.BlockSpec((None, 1, D_MODEL), lambda i: (layer, 0, 0)),
            pl.BlockSpec((None, 1, D_MODEL), lambda i: (layer, 0, 0)),
        ],
        out_specs=[
            pl.BlockSpec((tm, D_MODEL), lambda i: (i, 0)),
            pl.BlockSpec((tm, D_MODEL), lambda i: (i, 0)),
        ],
        out_shape=[
            jax.ShapeDtypeStruct((n_out, D_MODEL), F32),
            jax.ShapeDtypeStruct((n_out, D_MODEL), BF16),
        ],
        scratch_shapes=[pltpu.VMEM((tm, D_MODEL), BF16)],
        compiler_params=_params(("arbitrary",)),
        name="merge",
    )(x_lat, x_ctx, mod3, mod3, mod3, yp_lat, yp_ctx, z_lat, z_ctx, pb, pb, wp, wr, wo, ln_g, ln_b)


SWIGLU_COLS = 256


def _swiglu_hidden(t, w1_ref, w3_ref, h_ref):
    width = h_ref.shape[1]
    for c0 in range(0, width, SWIGLU_COLS):
        cols = slice(c0, min(c0 + SWIGLU_COLS, width))
        a = _dot(t, w1_ref[:, cols])
        h_ref[:, cols] = (a * _sigmoid(a) * _dot(t, w3_ref[:, cols])).astype(BF16)


def _ffn_kernel(t_ref, x1_ref, gm_ref, w1_ref, w3_ref, w2_ref, lg_ref, lb_ref, o_ref, acc_ref, h_ref):
    f = pl.program_id(1)
    _swiglu_hidden(t_ref[...], w1_ref, w3_ref, h_ref)
    part = _dot(h_ref[...], w2_ref[...])

    @pl.when(f == 0)
    def _():
        acc_ref[...] = part

    @pl.when(f > 0)
    def _():
        acc_ref[...] = acc_ref[...] + part

    @pl.when(f == pl.num_programs(1) - 1)
    def _():
        o_ref[...] = _layer_norm(DEEPNORM_ALPHA * x1_ref[...] + gm_ref[...] * acc_ref[...],
                                 lg_ref[...], lb_ref[...])


def _ffn(t, x1, mod3, w1, w3, w2, ln_g, ln_b, layer, j, n_lat, seq, batch):
    n = t.shape[0]
    tm = TOKEN_TILE
    _, _, row, _, _ = _tile_maps(n_lat, seq, batch, tm, 0)

    return pl.pallas_call(
        _ffn_kernel,
        grid=(n // tm, D_FF // FFN_CHUNK),
        in_specs=[
            pl.BlockSpec((tm, D_MODEL), lambda i, f: (i, 0)),
            pl.BlockSpec((tm, D_MODEL), lambda i, f: (i, 0)),
            pl.BlockSpec((None, 1, D_MODEL), lambda i, f: (row(i), 0, 5)),
            pl.BlockSpec((None, D_MODEL, FFN_CHUNK), lambda i, f: (j, 0, f)),
            pl.BlockSpec((None, D_MODEL, FFN_CHUNK), lambda i, f: (j, 0, f)),
            pl.BlockSpec((None, FFN_CHUNK, D_MODEL), lambda i, f: (j, f, 0)),
            pl.BlockSpec((None, 1, D_MODEL), lambda i, f: (layer, 0, 0)),
            pl.BlockSpec((None, 1, D_MODEL), lambda i, f: (layer, 0, 0)),
        ],
        out_specs=pl.BlockSpec((tm, D_MODEL), lambda i, f: (i, 0)),
        out_shape=jax.ShapeDtypeStruct((n, D_MODEL), F32),
        scratch_shapes=[pltpu.VMEM((tm, D_MODEL), F32), pltpu.VMEM((tm, FFN_CHUNK), BF16)],
        compiler_params=_params(("arbitrary", "arbitrary")),
        name="ffn_dense",
    )(t, x1, mod3, w1, w3, w2, ln_g, ln_b)


ROUTE_LANES = 128


def _router_kernel(t_ref, wr_ref, tri_ref, route_ref, before_ref, total_ref, run_ref):
    @pl.when(pl.program_id(0) == 0)
    def _():
        run_ref[...] = jnp.zeros_like(run_ref)

    tt = t_ref.shape[0]
    logits = _dot(t_ref[...], wr_ref[...])
    lane = lax.broadcasted_iota(jnp.int32, (tt, ROUTE_LANES), 1)
    neg = jnp.float32(-jnp.inf)
    lg = jnp.where(lane < N_EXPERTS, logits, neg)
    m1 = jnp.max(lg, axis=1, keepdims=True)
    i1 = jnp.min(jnp.where(lg == m1, lane, ROUTE_LANES), axis=1, keepdims=True)
    lg2 = jnp.where(lane == i1, neg, lg)
    m2 = jnp.max(lg2, axis=1, keepdims=True)
    i2 = jnp.min(jnp.where(lg2 == m2, lane, ROUTE_LANES), axis=1, keepdims=True)
    e = jnp.exp(m2 - m1)
    w1 = 1.0 / (1.0 + e)
    w2 = e / (1.0 + e)
    hit1 = lane == i1
    hit2 = lane == i2
    onehot = jnp.where(hit1 | hit2, 1.0, 0.0)
    run = run_ref[...]
    prefix = _dot(tri_ref[...], onehot.astype(BF16)) + run
    r1 = jnp.sum(jnp.where(hit1, prefix, 0.0), axis=1, keepdims=True)
    r2 = jnp.sum(jnp.where(hit2, prefix, 0.0), axis=1, keepdims=True)
    for k in range(tt // ROUTE_TILE):
        before_ref[k] = prefix[k * ROUTE_TILE:k * ROUTE_TILE + 1, :]
    run = run + jnp.sum(onehot, axis=0, keepdims=True)
    run_ref[...] = run
    total_ref[...] = run
    out = jnp.where(lane == 0, i1.astype(F32), 0.0)
    out = jnp.where(lane == 1, i2.astype(F32), out)
    out = jnp.where(lane == 2, w1, out)
    out = jnp.where(lane == 3, w2, out)
    out = jnp.where(lane == 4, r1, out)
    out = jnp.where(lane == 5, r2, out)
    route_ref[...] = out


def _router(t, wr, tri):
    n = t.shape[0]
    tt = ROUTER_STEP
    sub = tt // ROUTE_TILE
    return pl.pallas_call(
        _router_kernel,
        grid=(n // tt,),
        in_specs=[
            pl.BlockSpec((tt, D_MODEL), lambda i: (i, 0)),
            pl.BlockSpec((D_MODEL, ROUTE_LANES), lambda i: (0, 0)),
            pl.BlockSpec((tt, tt), lambda i: (0, 0)),
        ],
        out_specs=[
            pl.BlockSpec((tt, ROUTE_LANES), lambda i: (i, 0)),
            pl.BlockSpec((sub, 1, ROUTE_LANES), lambda i: (i, 0, 0)),
            pl.BlockSpec((1, ROUTE_LANES), lambda i: (0, 0)),
        ],
        out_shape=[
            jax.ShapeDtypeStruct((n, ROUTE_LANES), F32),
            jax.ShapeDtypeStruct((n // ROUTE_TILE, 1, ROUTE_LANES), F32),
            jax.ShapeDtypeStruct((1, ROUTE_LANES), F32),
        ],
        scratch_shapes=[pltpu.VMEM((1, ROUTE_LANES), F32)],
        compiler_params=_params(("arbitrary",)),
        name="moe_router",
    )(t, wr, tri)


DMA_RING = 8
GATHER_ROWS = ROUTE_TILE + 16


def _gather_kernel(es_ref, s_ref, a_ref, dst_ref, t_hbm, o_ref, buf, sem):
    b = pl.program_id(0)
    total = es_ref[pl.num_programs(0)]
    o_ref[...] = jnp.zeros_like(o_ref)

    def tile_copy(j):
        slot = j & (DMA_RING - 1)
        start = pl.multiple_of(s_ref[j] * ROUTE_TILE, ROUTE_TILE)
        return pltpu.make_async_copy(t_hbm.at[pl.ds(start, ROUTE_TILE)], buf.at[slot], sem.at[slot])

    @pl.when(b == 0)
    def _():
        for i in range(DMA_RING - 1):
            @pl.when(i < total)
            def _(i=i):
                tile_copy(i).start()

    def body(j, carry):
        tile_copy(j).wait()

        @pl.when(j + (DMA_RING - 1) < total)
        def _():
            tile_copy(j + (DMA_RING - 1)).start()

        a = pl.multiple_of(a_ref[j], 16)
        rows = b * EXPERT_ROWS + a + lax.broadcasted_iota(jnp.int32, (GATHER_ROWS, ROUTE_TILE), 0)
        d = dst_ref[s_ref[j]]
        sel = (d[0:1, :] == rows) | (d[1:2, :] == rows)
        win = pl.ds(a, GATHER_ROWS)
        picked = _dot(jnp.where(sel, 1.0, 0.0).astype(BF16), buf[j & (DMA_RING - 1)])
        o_ref[win, :] = o_ref[win, :] + picked.astype(BF16)
        return carry

    lax.fori_loop(es_ref[b], es_ref[b + 1], body, 0)


def _gather(lists, t, dst, n_blocks):
    n_tiles = dst.shape[0]
    spec = pltpu.PrefetchScalarGridSpec(
        num_scalar_prefetch=3,
        grid=(n_blocks,),
        in_specs=[
            pl.BlockSpec((n_tiles, 2, ROUTE_TILE), lambda b, es, ss, aa: (0, 0, 0)),
            pl.BlockSpec(memory_space=pl.ANY),
        ],
        out_specs=pl.BlockSpec((EXPERT_ROWS, D_MODEL), lambda b, es, ss, aa: (b, 0)),
        scratch_shapes=[
            pltpu.VMEM((DMA_RING, ROUTE_TILE, D_MODEL), BF16),
            pltpu.SemaphoreType.DMA((DMA_RING,)),
        ],
    )
    return pl.pallas_call(
        _gather_kernel,
        grid_spec=spec,
        out_shape=jax.ShapeDtypeStruct((n_blocks * EXPERT_ROWS, D_MODEL), BF16),
        compiler_params=_params(("arbitrary",)),
        name="moe_gather",
    )(*lists, dst, t)


def _expert_kernel(be_ref, bv_ref, x_ref, w1_ref, w3_ref, w2_ref, o_ref, acc_ref, h_ref):
    b = pl.program_id(0)
    f = pl.program_id(1)
    last = pl.num_programs(1) - 1
    used = bv_ref[b] == 1

    @pl.when(used)
    def _():
        _swiglu_hidden(x_ref[...], w1_ref, w3_ref, h_ref)
        part = _dot(h_ref[...], w2_ref[...])

        @pl.when(f == 0)
        def _():
            acc_ref[...] = part

        @pl.when(f > 0)
        def _():
            acc_ref[...] = acc_ref[...] + part

        @pl.when(f == last)
        def _():
            o_ref[...] = acc_ref[...].astype(BF16)

    @pl.when(jnp.logical_not(used) & (f == last))
    def _():
        o_ref[...] = jnp.zeros_like(o_ref)


def _experts(block_e, block_used, xs, w1, w3, w2, j):
    n_blocks = block_e.shape[0]
    spec = pltpu.PrefetchScalarGridSpec(
        num_scalar_prefetch=2,
        grid=(n_blocks, EXPERT_FF // EXPERT_CHUNK),
        in_specs=[
            pl.BlockSpec((EXPERT_ROWS, D_MODEL), lambda b, f, be, bv: (b, 0)),
            pl.BlockSpec((None, None, D_MODEL, EXPERT_CHUNK), lambda b, f, be, bv: (j, be[b], 0, f)),
            pl.BlockSpec((None, None, D_MODEL, EXPERT_CHUNK), lambda b, f, be, bv: (j, be[b], 0, f)),
            pl.BlockSpec((None, None, EXPERT_CHUNK, D_MODEL), lambda b, f, be, bv: (j, be[b], f, 0)),
        ],
        out_specs=pl.BlockSpec((EXPERT_ROWS, D_MODEL), lambda b, f, be, bv: (b, 0)),
        scratch_shapes=[pltpu.VMEM((EXPERT_ROWS, D_MODEL), F32), pltpu.VMEM((EXPERT_ROWS, EXPERT_CHUNK), BF16)],
    )
    return pl.pallas_call(
        _expert_kernel,
        grid_spec=spec,
        out_shape=jax.ShapeDtypeStruct(xs.shape, BF16),
        compiler_params=_params(("arbitrary", "arbitrary")),
        name="moe_experts",
    )(block_e, block_used, xs, w1, w3, w2)


COMBINE_ROWS = ROUTE_TILE
DW_LANES = 4


def _combine_kernel(es_ref, r_ref, e_ref, h_ref, dw_ref, x1_ref, gm_ref, lg_ref, lb_ref, y_hbm, o_ref,
                    buf, sem, acc_ref, d_ref, w_ref):
    s = pl.program_id(0)
    total = es_ref[pl.num_programs(0)]
    acc_ref[...] = jnp.zeros_like(acc_ref)
    for k in range(2):
        d_ref[k] = jnp.broadcast_to(dw_ref[:, k:k + 1].astype(jnp.int32), d_ref.shape[1:])
        w_ref[k] = jnp.broadcast_to(dw_ref[:, 2 + k:3 + k], w_ref.shape[1:])

    def win_copy(j):
        slot = j & (DMA_RING - 1)
        start = pl.multiple_of(r_ref[j], 16)
        return pltpu.make_async_copy(y_hbm.at[pl.ds(start, COMBINE_ROWS)], buf.at[slot], sem.at[slot])

    @pl.when(s == 0)
    def _():
        for i in range(DMA_RING - 1):
            @pl.when(i < total)
            def _(i=i):
                win_copy(i).start()

    def body(j, carry):
        win_copy(j).wait()

        @pl.when(j + (DMA_RING - 1) < total)
        def _():
            win_copy(j + (DMA_RING - 1)).start()

        lo = jnp.maximum(e_ref[j], r_ref[j])
        hi = jnp.minimum(h_ref[j], r_ref[j] + COMBINE_ROWS)
        d1 = d_ref[0]
        d2 = d_ref[1]
        in1 = (d1 >= lo) & (d1 < hi)
        in2 = (d2 >= lo) & (d2 < hi)
        wsel = jnp.where(in1, w_ref[0], 0.0) + jnp.where(in2, w_ref[1], 0.0)
        hit = jnp.where(in1, d1, jnp.where(in2, d2, -1)) - r_ref[j]
        hit = jnp.concatenate([hit] * (COMBINE_ROWS // 128), axis=1)
        col = lax.broadcasted_iota(jnp.int32, (ROUTE_TILE, COMBINE_ROWS), 1)
        picked = _dot(jnp.where(hit == col, 1.0, 0.0).astype(BF16), buf[j & (DMA_RING - 1)])
        acc_ref[...] = acc_ref[...] + jnp.concatenate([wsel] * (D_MODEL // 128), axis=1) * picked
        return carry

    lax.fori_loop(es_ref[s], es_ref[s + 1], body, 0)
    o_ref[...] = _layer_norm(DEEPNORM_ALPHA * x1_ref[...] + gm_ref[...] * acc_ref[...], lg_ref[...], lb_ref[...])


def _combine(lists, y, dw, x1, mod3, ln_g, ln_b, layer, n_lat, seq, batch):
    n = x1.shape[0]
    nl = n_lat // ROUTE_TILE
    tpb = seq // ROUTE_TILE

    def row(s):
        return jnp.where(s < nl, s // tpb, batch)

    spec = pltpu.PrefetchScalarGridSpec(
        num_scalar_prefetch=4,
        grid=(n // ROUTE_TILE,),
        in_specs=[
            pl.BlockSpec((ROUTE_TILE, DW_LANES), lambda s, es, rr, ee, hh: (s, 0)),
            pl.BlockSpec((ROUTE_TILE, D_MODEL), lambda s, es, rr, ee, hh: (s, 0)),
            pl.BlockSpec((None, 1, D_MODEL), lambda s, es, rr, ee, hh: (row(s), 0, 5)),
            pl.BlockSpec((None, 1, D_MODEL), lambda s, es, rr, ee, hh: (layer, 0, 0)),
            pl.BlockSpec((None, 1, D_MODEL), lambda s, es, rr, ee, hh: (layer, 0, 0)),
            pl.BlockSpec(memory_space=pl.ANY),
        ],
        out_specs=pl.BlockSpec((ROUTE_TILE, D_MODEL), lambda s, es, rr, ee, hh: (s, 0)),
        scratch_shapes=[
            pltpu.VMEM((DMA_RING, COMBINE_ROWS, D_MODEL), BF16),
            pltpu.SemaphoreType.DMA((DMA_RING,)),
            pltpu.VMEM((ROUTE_TILE, D_MODEL), F32),
            pltpu.VMEM((2, ROUTE_TILE, 128), jnp.int32),
            pltpu.VMEM((2, ROUTE_TILE, 128), F32),
        ],
    )
    return pl.pallas_call(
        _combine_kernel,
        grid_spec=spec,
        out_shape=jax.ShapeDtypeStruct((n, D_MODEL), F32),
        compiler_params=_params(("arbitrary",)),
        name="moe_combine",
    )(*lists, dw, x1, mod3, ln_g, ln_b, y)


def _moe(t, x1, mod3, wr, tri, w1, w3, w2, ln_g, ln_b, layer, j, n_lat, seq, batch):
    n = t.shape[0]
    n_tiles = n // ROUTE_TILE
    n_blocks = -(-(2 * n + N_EXPERTS * (EXPERT_ROWS - 1)) // EXPERT_ROWS) + 1
    i32 = jnp.int32

    route, before, total = _router(t, wr, tri)

    e12 = route[:, 0:2].astype(i32)
    rank = route[:, 4:6].astype(i32)
    counts = total[0, :N_EXPERTS].astype(i32)
    padded = (counts + EXPERT_ROWS - 1) // EXPERT_ROWS * EXPERT_ROWS
    pad_end = jnp.cumsum(padded)
    pad_start = pad_end - padded
    dest = pad_start[e12] + rank
    dst = dest.reshape(n_tiles, ROUTE_TILE, 2).transpose(0, 2, 1)
    dw = jnp.concatenate([dest.astype(F32), route[:, 2:4]], axis=1)
    block_start = jnp.arange(n_blocks, dtype=i32) * EXPERT_ROWS
    block_e = jnp.minimum(jnp.searchsorted(pad_end, block_start, side="right"), N_EXPERTS - 1).astype(i32)
    block_used = (block_start < pad_end[-1]).astype(i32)

    cb = before[:, 0, :N_EXPERTS].astype(i32)
    ca = jnp.concatenate([cb[1:], counts[None, :]], axis=0)
    lo = pad_start[None, :] + cb
    hi = pad_start[None, :] + ca
    some = hi > lo
    tile_id = jnp.broadcast_to(jnp.arange(n_tiles, dtype=i32)[:, None, None], (n_tiles, N_EXPERTS, 2))

    b_lo = lo // EXPERT_ROWS
    b_hi = (hi - 1) // EXPERT_ROWS
    off = jnp.minimum((lo - b_lo * EXPERT_ROWS) // 16 * 16, EXPERT_ROWS - GATHER_ROWS)
    g_block = jnp.stack([b_lo, b_lo + 1], axis=-1)
    g_off = jnp.stack([off, jnp.zeros_like(off)], axis=-1)
    g_ok = jnp.stack([some, some & (b_hi > b_lo)], axis=-1)
    key = jnp.where(g_ok, g_block, n_blocks).reshape(-1)
    order = jnp.argsort(key, stable=True)
    g_start = jnp.searchsorted(key[order], jnp.arange(n_blocks + 1, dtype=i32), side="left").astype(i32)
    lists = (g_start, tile_id.reshape(-1)[order], g_off.reshape(-1)[order].astype(i32))
    xs = _gather(lists, t, dst, n_blocks)

    y = _experts(block_e, block_used, xs, w1, w3, w2, j)

    r0 = lo // 16 * 16
    c_row = jnp.stack([r0, r0 + COMBINE_ROWS], axis=-1)
    c_ok = jnp.stack([some, some & (hi > r0 + COMBINE_ROWS)], axis=-1)
    key = jnp.where(c_ok, tile_id, n_tiles).reshape(-1)
    order = jnp.argsort(key, stable=True)
    c_start = jnp.searchsorted(key[order], jnp.arange(n_tiles + 1, dtype=i32), side="left").astype(i32)
    run_lo = jnp.broadcast_to(lo[:, :, None], c_row.shape).reshape(-1)[order].astype(i32)
    run_hi = jnp.broadcast_to(hi[:, :, None], c_row.shape).reshape(-1)[order].astype(i32)
    lists = (c_start, c_row.reshape(-1)[order].astype(i32), run_lo, run_hi)
    return _combine(lists, y, dw, x1, mod3, ln_g, ln_b, layer, n_lat, seq, batch)


def _window_counts(n, w):
    t = np.arange(n)
    left = w // 2
    right = w - 1 - left
    return (np.minimum(t + right + 1, n) - np.maximum(t - left, 0)).astype(np.float32)


def _window_matrix(n, w):
    left = w // 2
    right = w - 1 - left
    t = np.arange(n)
    return ((t[None, :] >= t[:, None] - left) & (t[None, :] <= t[:, None] + right)).astype(np.float32)


def _pool_tables(seq, n_ctx):
    rows = seq // GRID_W
    per_tile = 256 // GRID_W
    mc = np.stack([np.kron(np.eye(per_tile, dtype=np.float32), _window_matrix(GRID_W, w)) for w in POOL_WINDOWS])
    inv = np.stack([1.0 / np.outer(_window_counts(rows, w), _window_counts(GRID_W, w)).reshape(seq)
                    for w in POOL_WINDOWS])
    inv = np.broadcast_to(inv[:, :, None], (4, seq, POOL_GROUP)).astype(np.float32)
    m1d = np.stack([_window_matrix(n_ctx, w) for w in POOL_WINDOWS])
    inv1d = np.stack([1.0 / _window_counts(n_ctx, w) for w in POOL_WINDOWS])
    inv1d = np.broadcast_to(inv1d[:, :, None], (4, n_ctx, POOL_GROUP)).astype(np.float32)
    return (jnp.asarray(mc, BF16), jnp.asarray(inv), jnp.asarray(m1d, BF16), jnp.asarray(inv1d))


def _rope_tables(seq):
    t = jnp.arange(seq)
    row = (t // GRID_W).astype(F32)
    col = (t % GRID_W).astype(F32)
    n_freq = RET_QK_DIM // 4
    inv = jnp.exp(-jnp.log(ROPE_BASE) * jnp.arange(n_freq, dtype=F32) / n_freq)
    ang = jnp.concatenate([row[:, None] * inv, col[:, None] * inv], -1)
    cos, sin = jnp.cos(ang), jnp.sin(ang)
    cos2 = jnp.concatenate([cos, cos], -1)
    sin2 = jnp.concatenate([-sin, sin], -1)
    cos2 = jnp.concatenate([cos2, jnp.ones((TOKEN_TILE, RET_QK_DIM), F32)], 0)
    sin2 = jnp.concatenate([sin2, jnp.zeros((TOKEN_TILE, RET_QK_DIM), F32)], 0)
    return cos2, sin2


def kernel(x, c, ctx, c_ctx, ada_w, ada_b, w_in, pool_w, pool_scale, w_pool_out, w_ret_out, ret_decay_logit,
           w_out, ln_mix_g, ln_mix_b, ln_ffn_g, ln_ffn_b, ffn_w1, ffn_w3, ffn_w2, moe_router, moe_w1, moe_w3,
           moe_w2):
    batch, seq, d = x.shape
    n_ctx = ctx.shape[1]
    n_lat = batch * seq
    depth = ada_w.shape[0]
    assert d == D_MODEL and depth == DEPTH and batch < MOD_ROWS
    assert seq % TOKEN_TILE == 0 and (batch * n_ctx) % TOKEN_TILE == 0 and n_lat % n_ctx == 0
    assert TOKEN_TILE % ROUTER_STEP == 0 and ROUTER_STEP % ROUTE_TILE == 0

    s_in = jnp.zeros((MOD_ROWS, d), F32).at[:batch].set(c).at[batch].set(c_ctx)
    mod = _modulation(s_in, ada_w, ada_b)
    cos2, sin2 = _rope_tables(seq)
    mc, inv, m1d, inv1d = _pool_tables(seq, n_ctx)
    tri = jnp.asarray(np.tril(np.ones((ROUTER_STEP, ROUTER_STEP), np.float32), -1), BF16)

    w_in_b = w_in.astype(BF16)
    pool_w_b = pool_w.astype(BF16)
    pool_s = pool_scale.reshape(depth, 1, POOL_WIDTH)
    wp_b, wr_b, wo_b = w_pool_out.astype(BF16), w_ret_out.astype(BF16), w_out.astype(BF16)
    ffn_b = ffn_w1.astype(BF16), ffn_w3.astype(BF16), ffn_w2.astype(BF16)
    moe_b = moe_w1.astype(BF16), moe_w3.astype(BF16), moe_w2.astype(BF16)
    ln_mix = ln_mix_g.reshape(depth, 1, d), ln_mix_b.reshape(depth, 1, d)
    ln_ffn = ln_ffn_g.reshape(depth, 1, d), ln_ffn_b.reshape(depth, 1, d)

    n_tok = n_lat + batch * n_ctx
    x_lat, x_ctx, ctx_offset = x.reshape(n_lat, d), ctx.reshape(batch * n_ctx, d), 0
    for l in range(depth):
        last = l == depth - 1
        mod3 = mod[l].reshape(MOD_ROWS, 1, 6 * d)
        pa, pb = _in_proj(x_lat, x_ctx, ctx_offset, mod3, cos2, sin2, w_in_b, l, n_tok, n_lat, seq, batch)
        yp_lat = _pool_lat(pa, mc, inv, pool_w_b, pool_s, l, batch, seq)
        yp_ctx = _pool_ctx(pa, m1d, inv1d, pool_w_b, pool_s, l, batch, n_lat, n_ctx)
        dl = jnp.broadcast_to(ret_decay_logit[l].astype(F32)[:, :, None, None],
                              (2, RET_HEADS, RET_CHUNK, RET_V_DIM))
        z_lat, z_ctx = _retention(pa, pb, dl, batch, seq, n_ctx)
        n_out = n_lat if last else n_tok
        x1, t = _merge(x_lat, x_ctx, ctx_offset, mod3, yp_lat, yp_ctx, z_lat, z_ctx, pb, wp_b, wr_b, wo_b,
                       *ln_mix, l, n_out, n_lat, seq, batch)
        j = l // 2
        if l % 2 == 0:
            xs = _ffn(t, x1, mod3, *ffn_b, *ln_ffn, l, j, n_lat, seq, batch)
        else:
            wr = jnp.zeros((d, ROUTE_LANES), BF16).at[:, :N_EXPERTS].set(moe_router[j].astype(BF16))
            xs = _moe(t, x1, mod3, wr, tri, *moe_b, *ln_ffn, l, j, n_lat, seq, batch)
        x_lat, x_ctx, ctx_offset = xs, xs, n_lat // TOKEN_TILE
    return xs[:n_lat].reshape(batch, seq, d)
```

```python
import functools

import jax
import jax.numpy as jnp
import numpy as np
from jax import lax
from jax.experimental import pallas as pl
from jax.experimental.pallas import tpu as pltpu

F32 = jnp.float32
BF16 = jnp.bfloat16

D_MODEL = 1024
DEPTH = 4
GRID_W = 64
POOL_WINDOWS = (2, 4, 8, 16)
POOL_GROUP = 128
POOL_WIDTH = POOL_GROUP * len(POOL_WINDOWS)
RET_HEADS = 4
RET_QK_DIM = 128
RET_V_DIM = 256
RET_CHUNK = 128
ROPE_BASE = 10000.0
OFF_Q = POOL_WIDTH
OFF_K = OFF_Q + RET_HEADS * RET_QK_DIM
OFF_V = OFF_K + RET_HEADS * RET_QK_DIM
OFF_G = OFF_V + RET_HEADS * RET_V_DIM
WIDTH_B = 3 * D_MODEL
D_FF = 2816
N_EXPERTS = 8
EXPERT_FF = 3584
DEEPNORM_ALPHA = (2 * DEPTH) ** 0.25
LN_EPS = 1e-5
K_SCALE = RET_QK_DIM ** -0.5

MOD_ROWS = 24
TOKEN_TILE = 512
ROUTE_TILE = 256
ROUTER_STEP = 512
EXPERT_ROWS = 512
FFN_ROWS = 2 * TOKEN_TILE
VMEM_LIMIT = 56 * 1024 * 1024


def _dot(a, b):
    return jnp.dot(a, b, preferred_element_type=F32)


def _sigmoid(x):
    return 1.0 / (1.0 + jnp.exp(-x))


def _split_bf16(a):
    hi = a.astype(BF16)
    lo = (a - hi.astype(F32)).astype(BF16)
    return hi, lo


def _layer_norm(v, g, b):
    mean = jnp.mean(v, axis=-1, keepdims=True)
    vc = v - mean
    var = jnp.mean(vc * vc, axis=-1, keepdims=True)
    return vc * lax.rsqrt(var + LN_EPS) * g + b


def _params(sem, vmem=VMEM_LIMIT):
    return pltpu.CompilerParams(dimension_semantics=sem, vmem_limit_bytes=vmem)


def _mod_kernel(s_ref, w_ref, b_ref, o_ref):
    s = s_ref[...]
    s = s * _sigmoid(s)
    s_hi, s_lo = _split_bf16(s)
    w_hi, w_lo = _split_bf16(w_ref[...])
    o_ref[...] = _dot(s_hi, w_hi) + (_dot(s_hi, w_lo) + _dot(s_lo, w_hi)) + b_ref[...]


def _modulation(s_in, ada_w, ada_b):
    depth, d, width = ada_w.shape
    tn = 1536
    return pl.pallas_call(
        _mod_kernel,
        grid=(depth, width // tn),
        in_specs=[
            pl.BlockSpec((MOD_ROWS, d), lambda l, j: (0, 0)),
            pl.BlockSpec((None, d, tn), lambda l, j: (l, 0, j)),
            pl.BlockSpec((None, 1, tn), lambda l, j: (l, 0, j)),
        ],
        out_specs=pl.BlockSpec((None, MOD_ROWS, tn), lambda l, j: (l, 0, j)),
        out_shape=jax.ShapeDtypeStruct((depth, MOD_ROWS, width), F32),
        compiler_params=_params(("arbitrary", "arbitrary")),
        name="adaln_mod",
    )(s_in, ada_w, ada_b.reshape(depth, 1, width))


IN_COLS = 512
IN_BLOCKS = (OFF_G + WIDTH_B) // IN_COLS


def _pick_x(x_refs, rows, is_lat):
    if len(x_refs) == 1:
        return x_refs[0][rows, :]
    return jnp.where(is_lat, x_refs[0][rows, :], x_refs[1][rows, :])


def _in_kernel(*refs, n_lat_tiles, n_x):
    x_refs = refs[:n_x]
    sh_ref, sc_ref, cos_ref, sin_ref = refs[n_x:n_x + 4]
    w_refs = refs[n_x + 4:n_x + 4 + IN_BLOCKS]
    pa_ref, pb_ref = refs[n_x + 4 + IN_BLOCKS:]
    x = _pick_x(x_refs, slice(None), pl.program_id(0) < n_lat_tiles)
    h = (x * (1.0 + sc_ref[...]) + sh_ref[...]).astype(BF16)
    cos = cos_ref[...]
    sin = sin_ref[...]
    for blk in range(IN_BLOCKS):
        c0 = blk * IN_COLS
        acc = _dot(h, w_refs[blk][...])
        if c0 in (OFF_Q, OFF_K):
            for hh in range(RET_HEADS):
                t = acc[:, hh * RET_QK_DIM:(hh + 1) * RET_QK_DIM]
                r = t * cos + pltpu.roll(t, RET_QK_DIM // 2, 1) * sin
                if c0 == OFF_K:
                    r = r * K_SCALE
                pa_ref[:, c0 + hh * RET_QK_DIM:c0 + (hh + 1) * RET_QK_DIM] = r.astype(BF16)
        elif c0 < OFF_G:
            pa_ref[:, c0:c0 + IN_COLS] = acc.astype(BF16)
        else:
            pb_ref[:, c0 - OFF_G:c0 - OFF_G + IN_COLS] = acc.astype(BF16)


def _tile_maps(n_lat, seq, batch, tile):
    nl = n_lat // tile
    tpb = seq // tile

    def row(i):
        return jnp.where(i < nl, i // tpb, batch)

    def lat(i):
        return jnp.minimum(i, nl - 1)

    def ctx(i):
        return jnp.maximum(i - nl, 0)

    return nl, tpb, row, lat, ctx


def _x_operands(x_parts, n_lat, seq, batch, tile):
    _, _, _, lat, ctx = _tile_maps(n_lat, seq, batch, tile)
    if len(x_parts) == 1:
        return [pl.BlockSpec((tile, D_MODEL), lambda i: (i, 0))], list(x_parts)
    return ([pl.BlockSpec((tile, D_MODEL), lambda i: (lat(i), 0)),
             pl.BlockSpec((tile, D_MODEL), lambda i: (ctx(i), 0))], list(x_parts))


def _in_proj(x_parts, mod3, cos2, sin2, w_in, layer, n_tok, n_lat, seq, batch):
    tm = TOKEN_TILE
    nl, tpb, row, _, _ = _tile_maps(n_lat, seq, batch, tm)
    x_specs, x_args = _x_operands(x_parts, n_lat, seq, batch, tm)

    def rope(i):
        return jnp.where(i < nl, i % tpb, tpb)

    def w_spec(blk):
        return pl.BlockSpec((None, D_MODEL, IN_COLS), lambda i: (layer, 0, blk))

    return pl.pallas_call(
        functools.partial(_in_kernel, n_lat_tiles=nl, n_x=len(x_args)),
        grid=(n_tok // tm,),
        in_specs=x_specs + [
            pl.BlockSpec((None, 1, D_MODEL), lambda i: (row(i), 0, 0)),
            pl.BlockSpec((None, 1, D_MODEL), lambda i: (row(i), 0, 1)),
            pl.BlockSpec((tm, RET_QK_DIM), lambda i: (rope(i), 0)),
            pl.BlockSpec((tm, RET_QK_DIM), lambda i: (rope(i), 0)),
        ] + [w_spec(blk) for blk in range(IN_BLOCKS)],
        out_specs=[
            pl.BlockSpec((tm, OFF_G), lambda i: (i, 0)),
            pl.BlockSpec((tm, WIDTH_B), lambda i: (i, 0)),
        ],
        out_shape=[
            jax.ShapeDtypeStruct((n_tok, OFF_G), BF16),
            jax.ShapeDtypeStruct((n_tok, WIDTH_B), BF16),
        ],
        compiler_params=_params(("arbitrary",)),
        name="in_proj",
    )(*x_args, mod3, mod3, cos2, sin2, *([w_in] * IN_BLOCKS))


POOL_PAD_ROWS = 8
POOL_ROW_CHUNK = 8


def _pool_kernel(u_ref, mc_ref, inv_ref, pw_ref, ps_ref, o_ref, zp_ref, *, rows):
    seq = rows * GRID_W
    pad = POOL_PAD_ROWS * GRID_W
    chunk = POOL_ROW_CHUNK * GRID_W
    zeros = jnp.zeros((pad, POOL_GROUP), F32)
    for g, w in enumerate(POOL_WINDOWS):
        lanes = slice(g * POOL_GROUP, (g + 1) * POOL_GROUP)
        zp_ref[0:pad, :] = zeros
        zp_ref[pad + seq:pad + seq + pad, :] = zeros
        for c in range(seq // 256):
            zp_ref[pad + c * 256:pad + (c + 1) * 256, :] = _dot(mc_ref[g], u_ref[c * 256:(c + 1) * 256, lanes])
        left = w // 2

        def body(rc, carry, g=g, w=w, left=left, lanes=lanes):
            tok = pl.multiple_of(rc * chunk, chunk)
            acc = zp_ref[pl.ds(tok + pad - left * GRID_W, chunk), :]
            for k in range(1, w):
                acc = acc + zp_ref[pl.ds(tok + pad + (k - left) * GRID_W, chunk), :]
            pooled = acc * inv_ref[g, pl.ds(tok, chunk), :]
            d = (pooled - u_ref[pl.ds(tok, chunk), lanes].astype(F32)).astype(BF16)
            y = _dot(d, pw_ref[g]) * ps_ref[:, lanes]
            o_ref[pl.ds(tok, chunk), lanes] = y.astype(BF16)
            return carry

        lax.fori_loop(0, rows // POOL_ROW_CHUNK, body, 0)


def _pool_lat(pa, mc, inv, pw, ps, layer, batch, seq):
    rows = seq // GRID_W
    return pl.pallas_call(
        functools.partial(_pool_kernel, rows=rows),
        grid=(batch,),
        in_specs=[
            pl.BlockSpec((seq, POOL_WIDTH), lambda b: (b, 0)),
            pl.BlockSpec((4, 256, 256), lambda b: (0, 0, 0)),
            pl.BlockSpec((4, seq, POOL_GROUP), lambda b: (0, 0, 0)),
            pl.BlockSpec((None, 4, POOL_GROUP, POOL_GROUP), lambda b: (layer, 0, 0, 0)),
            pl.BlockSpec((None, 1, POOL_WIDTH), lambda b: (layer, 0, 0)),
        ],
        out_specs=pl.BlockSpec((seq, POOL_WIDTH), lambda b: (b, 0)),
        out_shape=jax.ShapeDtypeStruct((batch * seq, POOL_WIDTH), BF16),
        scratch_shapes=[pltpu.VMEM(((rows + 2 * POOL_PAD_ROWS) * GRID_W, POOL_GROUP), F32)],
        compiler_params=_params(("arbitrary",)),
        name="pool_lat",
    )(pa, mc, inv, pw, ps)


def _pool_ctx_kernel(u_ref, m_ref, inv_ref, pw_ref, ps_ref, o_ref):
    for g in range(len(POOL_WINDOWS)):
        lanes = slice(g * POOL_GROUP, (g + 1) * POOL_GROUP)
        ug = u_ref[:, lanes]
        pooled = _dot(m_ref[g], ug) * inv_ref[g]
        d = (pooled - ug.astype(F32)).astype(BF16)
        o_ref[:, lanes] = (_dot(d, pw_ref[g]) * ps_ref[:, lanes]).astype(BF16)


def _pool_ctx(pa, m1d, inv1d, pw, ps, layer, batch, n_lat, n_ctx):
    first = n_lat // n_ctx
    return pl.pallas_call(
        _pool_ctx_kernel,
        grid=(batch,),
        in_specs=[
            pl.BlockSpec((n_ctx, POOL_WIDTH), lambda b: (first + b, 0)),
            pl.BlockSpec((4, n_ctx, n_ctx), lambda b: (0, 0, 0)),
            pl.BlockSpec((4, n_ctx, POOL_GROUP), lambda b: (0, 0, 0)),
            pl.BlockSpec((None, 4, POOL_GROUP, POOL_GROUP), lambda b: (layer, 0, 0, 0)),
            pl.BlockSpec((None, 1, POOL_WIDTH), lambda b: (layer, 0, 0)),
        ],
        out_specs=pl.BlockSpec((n_ctx, POOL_WIDTH), lambda b: (b, 0)),
        out_shape=jax.ShapeDtypeStruct((batch * n_ctx, POOL_WIDTH), BF16),
        compiler_params=_params(("arbitrary",)),
        name="pool_ctx",
    )(pa, m1d, inv1d, pw, ps)


def _log_sigmoid(x):
    return jnp.minimum(x, 0.0) - jnp.log1p(jnp.exp(-jnp.abs(x)))


def _decays(dl, backward):
    c = RET_CHUNK
    lg = _log_sigmoid(dl)
    lgq = lg[:, :RET_QK_DIM]
    ii = lax.broadcasted_iota(jnp.int32, (c, c), 0)
    jj = lax.broadcasted_iota(jnp.int32, (c, c), 1)
    pos = lax.broadcasted_iota(jnp.int32, (c, RET_QK_DIM), 0).astype(F32)
    if backward:
        diff = (jj - ii).astype(F32)
        qdec = jnp.exp(lgq * (c - pos))
        kdec = jnp.exp(lgq * pos)
    else:
        diff = (ii - jj).astype(F32)
        qdec = jnp.exp(lgq * (pos + 1.0))
        kdec = jnp.exp(lgq * (c - 1.0 - pos))
    inner = jnp.where(diff >= 0, jnp.exp(lgq * jnp.maximum(diff, 0.0)), 0.0)
    cdec = jnp.exp(lg * float(c))
    return inner, qdec, kdec, cdec


def _chunk_kv(kc, vc, kdec):
    kd = (kc.astype(F32) * kdec).astype(BF16)
    return lax.dot_general(kd, vc, (((0,), (0,)), ((), ())), preferred_element_type=F32)


def _chunk_out(qc, kc, vc, gc, states, mask, qdec2):
    scores = lax.dot_general(qc, kc, (((1,), (1,)), ((), ())), preferred_element_type=F32) * mask
    qf = qc.astype(F32)
    qd = (jnp.concatenate([qf, qf], axis=1) * qdec2).astype(BF16)
    o = _dot(scores.astype(BF16), vc) + _dot(qd, states)
    mean = jnp.mean(o, axis=-1, keepdims=True)
    oc = o - mean
    var = jnp.mean(oc * oc, axis=-1, keepdims=True)
    on = oc * lax.rsqrt(var + LN_EPS)
    gf = gc.astype(F32)
    return (gf * _sigmoid(gf) * on).astype(BF16)


def _ret_kernel(q_ref, k_ref, v_ref, g_ref, qc_ref, kc_ref, vc_ref, gc_ref, dl_ref,
                z_ref, zc_ref, st_ref, stc_ref, run_ref, *, n_chunks, n_ctx_chunks):
    c = RET_CHUNK
    dk = RET_QK_DIM
    inner_f, qdec_f, kdec_f, cdec_f = _decays(dl_ref[0], False)
    inner_b, qdec_b, kdec_b, cdec_b = _decays(dl_ref[1], True)
    mask = inner_f + inner_b
    qdec2 = jnp.concatenate([qdec_f, qdec_b], axis=1)

    def rows(j):
        return slice(j * c, (j + 1) * c)

    s = jnp.zeros((dk, RET_V_DIM), F32)
    for j in range(n_ctx_chunks):
        stc_ref[j, 0:dk, :] = s.astype(BF16)
        s = s * cdec_f + _chunk_kv(kc_ref[rows(j), :], vc_ref[rows(j), :], kdec_f)
    run_ref[0] = s
    s = jnp.zeros((dk, RET_V_DIM), F32)
    for j in reversed(range(n_ctx_chunks)):
        stc_ref[j, dk:2 * dk, :] = s.astype(BF16)
        s = s * cdec_b + _chunk_kv(kc_ref[rows(j), :], vc_ref[rows(j), :], kdec_b)
    run_ref[1] = s
    for j in range(n_ctx_chunks):
        zc_ref[rows(j), :] = _chunk_out(qc_ref[rows(j), :], kc_ref[rows(j), :], vc_ref[rows(j), :],
                                        gc_ref[rows(j), :], stc_ref[j], mask, qdec2)

    def scan(t, carry):
        rf = pl.ds(pl.multiple_of(t * c, c), c)
        sf = run_ref[0]
        st_ref[t, 0:dk, :] = sf.astype(BF16)
        run_ref[0] = sf * cdec_f + _chunk_kv(k_ref[rf, :], v_ref[rf, :], kdec_f)
        tb = n_chunks - 1 - t
        rb = pl.ds(pl.multiple_of(tb * c, c), c)
        sb = run_ref[1]
        st_ref[tb, dk:2 * dk, :] = sb.astype(BF16)
        run_ref[1] = sb * cdec_b + _chunk_kv(k_ref[rb, :], v_ref[rb, :], kdec_b)
        return carry

    lax.fori_loop(0, n_chunks, scan, 0, unroll=4)

    def emit(t, carry):
        r = pl.ds(pl.multiple_of(t * c, c), c)
        z_ref[r, :] = _chunk_out(q_ref[r, :], k_ref[r, :], v_ref[r, :], g_ref[r, :], st_ref[t], mask, qdec2)
        return carry

    lax.fori_loop(0, n_chunks, emit, 0, unroll=8)


def _retention(pa, pb, dl, batch, seq, n_ctx):
    n_lat = batch * seq
    first = n_lat // n_ctx
    qk0 = OFF_Q // RET_QK_DIM
    kk0 = OFF_K // RET_QK_DIM
    v0 = OFF_V // RET_V_DIM
    return pl.pallas_call(
        functools.partial(_ret_kernel, n_chunks=seq // RET_CHUNK, n_ctx_chunks=n_ctx // RET_CHUNK),
        grid=(batch, RET_HEADS),
        in_specs=[
            pl.BlockSpec((seq, RET_QK_DIM), lambda b, h: (b, qk0 + h)),
            pl.BlockSpec((seq, RET_QK_DIM), lambda b, h: (b, kk0 + h)),
            pl.BlockSpec((seq, RET_V_DIM), lambda b, h: (b, v0 + h)),
            pl.BlockSpec((seq, RET_V_DIM), lambda b, h: (b, h)),
            pl.BlockSpec((n_ctx, RET_QK_DIM), lambda b, h: (first + b, qk0 + h)),
            pl.BlockSpec((n_ctx, RET_QK_DIM), lambda b, h: (first + b, kk0 + h)),
            pl.BlockSpec((n_ctx, RET_V_DIM), lambda b, h: (first + b, v0 + h)),
            pl.BlockSpec((n_ctx, RET_V_DIM), lambda b, h: (first + b, h)),
            pl.BlockSpec((2, None, RET_CHUNK, RET_V_DIM), lambda b, h: (0, h, 0, 0)),
        ],
        out_specs=[
            pl.BlockSpec((seq, RET_V_DIM), lambda b, h: (b, h)),
            pl.BlockSpec((n_ctx, RET_V_DIM), lambda b, h: (b, h)),
        ],
        out_shape=[
            jax.ShapeDtypeStruct((n_lat, RET_HEADS * RET_V_DIM), BF16),
            jax.ShapeDtypeStruct((batch * n_ctx, RET_HEADS * RET_V_DIM), BF16),
        ],
        scratch_shapes=[
            pltpu.VMEM((seq // RET_CHUNK, 2 * RET_QK_DIM, RET_V_DIM), BF16),
            pltpu.VMEM((n_ctx // RET_CHUNK, 2 * RET_QK_DIM, RET_V_DIM), BF16),
            pltpu.VMEM((2, RET_QK_DIM, RET_V_DIM), F32),
        ],
        compiler_params=_params(("arbitrary", "arbitrary")),
        name="retention",
    )(pa, pa, pa, pb, pa, pa, pa, pb, dl)


MIX_COLS = 256


def _merge_kernel(*refs, n_lat_tiles, n_x):
    x_refs = refs[:n_x]
    (gm_ref, sh_ref, sc_ref, ypl_ref, ypc_ref, zl_ref, zc_ref, gp_ref, gr_ref,
     wp_ref, wr_ref, wo_ref, lg_ref, lb_ref, x1_ref, t_ref, mix_ref) = refs[n_x:]
    is_lat = pl.program_id(0) < n_lat_tiles
    yp = jnp.where(is_lat, ypl_ref[...], ypc_ref[...])
    z = jnp.where(is_lat, zl_ref[...], zc_ref[...])
    x = _pick_x(x_refs, slice(None), is_lat)
    for c0 in range(0, D_MODEL, MIX_COLS):
        cols = slice(c0, c0 + MIX_COLS)
        y_pool = _dot(yp, wp_ref[:, cols])
        y_ret = _dot(z, wr_ref[:, cols])
        mix = _sigmoid(gp_ref[:, cols].astype(F32)) * y_pool + _sigmoid(gr_ref[:, cols].astype(F32)) * y_ret
        mix_ref[:, cols] = mix.astype(BF16)
    y = _dot(mix_ref[...], wo_ref[...])
    x1 = _layer_norm(DEEPNORM_ALPHA * x + gm_ref[...] * y, lg_ref[...], lb_ref[...])
    x1_ref[...] = x1
    t_ref[...] = (x1 * (1.0 + sc_ref[...]) + sh_ref[...]).astype(BF16)


def _merge(x_parts, mod3, yp_lat, yp_ctx, z_lat, z_ctx, pb, wp, wr, wo, ln_g, ln_b, layer,
           n_out, n_lat, seq, batch):
    tm = TOKEN_TILE
    nl, _, row, lat, ctx = _tile_maps(n_lat, seq, batch, tm)
    x_specs, x_args = _x_operands(x_parts, n_lat, seq, batch, tm)

    return pl.pallas_call(
        functools.partial(_merge_kernel, n_lat_tiles=nl, n_x=len(x_args)),
        grid=(n_out // tm,),
        in_specs=x_specs + [
            pl.BlockSpec((None, 1, D_MODEL), lambda i: (row(i), 0, 2)),
            pl.BlockSpec((None, 1, D_MODEL), lambda i: (row(i), 0, 3)),
            pl.BlockSpec((None, 1, D_MODEL), lambda i: (row(i), 0, 4)),
            pl.BlockSpec((tm, POOL_WIDTH), lambda i: (lat(i), 0)),
            pl.BlockSpec((tm, POOL_WIDTH), lambda i: (ctx(i), 0)),
            pl.BlockSpec((tm, D_MODEL), lambda i: (lat(i), 0)),
            pl.BlockSpec((tm, D_MODEL), lambda i: (ctx(i), 0)),
            pl.BlockSpec((tm, D_MODEL), lambda i: (i, 1)),
            pl.BlockSpec((tm, D_MODEL), lambda i: (i, 2)),
            pl.BlockSpec((None, POOL_WIDTH, D_MODEL), lambda i: (layer, 0, 0)),
            pl.BlockSpec((None, D_MODEL, D_MODEL), lambda i: (layer, 0, 0)),
            pl.BlockSpec((None, D_MODEL, D_MODEL), lambda i: (layer, 0, 0)),
            pl.BlockSpec((None, 1, D_MODEL), lambda i: (layer, 0, 0)),
            pl.BlockSpec((None, 1, D_MODEL), lambda i: (layer, 0, 0)),
        ],
        out_specs=[
            pl.BlockSpec((tm, D_MODEL), lambda i: (i, 0)),
            pl.BlockSpec((tm, D_MODEL), lambda i: (i, 0)),
        ],
        out_shape=[
            jax.ShapeDtypeStruct((n_out, D_MODEL), F32),
            jax.ShapeDtypeStruct((n_out, D_MODEL), BF16),
        ],
        scratch_shapes=[pltpu.VMEM((tm, D_MODEL), BF16)],
        compiler_params=_params(("arbitrary",)),
        name="merge",
    )(*x_args, mod3, mod3, mod3, yp_lat, yp_ctx, z_lat, z_ctx, pb, pb, wp, wr, wo, ln_g, ln_b)


SWIGLU_COLS = 256


def _swiglu_hidden(t, w1_ref, w3_ref, h_ref):
    width = h_ref.shape[1]
    for c0 in range(0, width, SWIGLU_COLS):
        cols = slice(c0, min(c0 + SWIGLU_COLS, width))
        a = _dot(t, w1_ref[:, cols])
        h_ref[:, cols] = (a * _sigmoid(a) * _dot(t, w3_ref[:, cols])).astype(BF16)


def _ffn_kernel(t_ref, x1_ref, gm_ref, w1_ref, w3_ref, w2_ref, lg_ref, lb_ref, o_ref, h_ref):
    for r in range(FFN_ROWS // TOKEN_TILE):
        rows = slice(r * TOKEN_TILE, (r + 1) * TOKEN_TILE)
        _swiglu_hidden(t_ref[rows, :], w1_ref, w3_ref, h_ref.at[r])
        f = _dot(h_ref[r], w2_ref[...])
        o_ref[rows, :] = _layer_norm(DEEPNORM_ALPHA * x1_ref[rows, :] + gm_ref[...] * f, lg_ref[...], lb_ref[...])


def _ffn(t, x1, mod3, w1, w3, w2, ln_g, ln_b, layer, j, n_lat, seq, batch):
    n = t.shape[0]
    tm = FFN_ROWS
    _, _, row, _, _ = _tile_maps(n_lat, seq, batch, tm)
    resident = pl.Buffered(1)

    return pl.pallas_call(
        _ffn_kernel,
        grid=(n // tm,),
        in_specs=[
            pl.BlockSpec((tm, D_MODEL), lambda i: (i, 0)),
            pl.BlockSpec((tm, D_MODEL), lambda i: (i, 0)),
            pl.BlockSpec((None, 1, D_MODEL), lambda i: (row(i), 0, 5)),
            pl.BlockSpec((None, D_MODEL, D_FF), lambda i: (j, 0, 0), pipeline_mode=resident),
            pl.BlockSpec((None, D_MODEL, D_FF), lambda i: (j, 0, 0), pipeline_mode=resident),
            pl.BlockSpec((None, D_FF, D_MODEL), lambda i: (j, 0, 0), pipeline_mode=resident),
            pl.BlockSpec((None, 1, D_MODEL), lambda i: (layer, 0, 0)),
            pl.BlockSpec((None, 1, D_MODEL), lambda i: (layer, 0, 0)),
        ],
        out_specs=pl.BlockSpec((tm, D_MODEL), lambda i: (i, 0)),
        out_shape=jax.ShapeDtypeStruct((n, D_MODEL), F32),
        scratch_shapes=[pltpu.VMEM((tm // TOKEN_TILE, TOKEN_TILE, D_FF), BF16)],
        compiler_params=_params(("arbitrary",)),
        name="ffn_dense",
    )(t, x1, mod3, w1, w3, w2, ln_g, ln_b)


ROUTE_LANES = 128


def _router_kernel(t_ref, wr_ref, tri_ref, route_ref, before_ref, total_ref, run_ref):
    @pl.when(pl.program_id(0) == 0)
    def _():
        run_ref[...] = jnp.zeros_like(run_ref)

    tt = t_ref.shape[0]
    logits = _dot(t_ref[...], wr_ref[...])
    lane = lax.broadcasted_iota(jnp.int32, (tt, ROUTE_LANES), 1)
    neg = jnp.float32(-jnp.inf)
    lg = jnp.where(lane < N_EXPERTS, logits, neg)
    m1 = jnp.max(lg, axis=1, keepdims=True)
    i1 = jnp.min(jnp.where(lg == m1, lane, ROUTE_LANES), axis=1, keepdims=True)
    lg2 = jnp.where(lane == i1, neg, lg)
    m2 = jnp.max(lg2, axis=1, keepdims=True)
    i2 = jnp.min(jnp.where(lg2 == m2, lane, ROUTE_LANES), axis=1, keepdims=True)
    e = jnp.exp(m2 - m1)
    w1 = 1.0 / (1.0 + e)
    w2 = e / (1.0 + e)
    hit1 = lane == i1
    hit2 = lane == i2
    onehot = jnp.where(hit1 | hit2, 1.0, 0.0)
    run = run_ref[...]
    prefix = _dot(tri_ref[...], onehot.astype(BF16)) + run
    r1 = jnp.sum(jnp.where(hit1, prefix, 0.0), axis=1, keepdims=True)
    r2 = jnp.sum(jnp.where(hit2, prefix, 0.0), axis=1, keepdims=True)
    for k in range(tt // ROUTE_TILE):
        before_ref[k] = prefix[k * ROUTE_TILE:k * ROUTE_TILE + 1, :]
    run = run + jnp.sum(onehot, axis=0, keepdims=True)
    run_ref[...] = run
    total_ref[...] = run
    out = jnp.where(lane == 0, i1.astype(F32), 0.0)
    out = jnp.where(lane == 1, i2.astype(F32), out)
    out = jnp.where(lane == 2, w1, out)
    out = jnp.where(lane == 3, w2, out)
    out = jnp.where(lane == 4, r1, out)
    out = jnp.where(lane == 5, r2, out)
    route_ref[...] = out


def _router(t, wr, tri):
    n = t.shape[0]
    tt = ROUTER_STEP
    sub = tt // ROUTE_TILE
    return pl.pallas_call(
        _router_kernel,
        grid=(n // tt,),
        in_specs=[
            pl.BlockSpec((tt, D_MODEL), lambda i: (i, 0)),
            pl.BlockSpec((D_MODEL, ROUTE_LANES), lambda i: (0, 0)),
            pl.BlockSpec((tt, tt), lambda i: (0, 0)),
        ],
        out_specs=[
            pl.BlockSpec((tt, ROUTE_LANES), lambda i: (i, 0)),
            pl.BlockSpec((sub, 1, ROUTE_LANES), lambda i: (i, 0, 0)),
            pl.BlockSpec((1, ROUTE_LANES), lambda i: (0, 0)),
        ],
        out_shape=[
            jax.ShapeDtypeStruct((n, ROUTE_LANES), F32),
            jax.ShapeDtypeStruct((n // ROUTE_TILE, 1, ROUTE_LANES), F32),
            jax.ShapeDtypeStruct((1, ROUTE_LANES), F32),
        ],
        scratch_shapes=[pltpu.VMEM((1, ROUTE_LANES), F32)],
        compiler_params=_params(("arbitrary",)),
        name="moe_router",
    )(t, wr, tri)


DMA_RING = 8
GATHER_ROWS = 144
GATHER_SPLIT = 4


def _gather_kernel(es_ref, sa_ref, rr_ref, dst_ref, t_hbm, o_ref, buf, sem):
    b = pl.program_id(0)
    total = es_ref[pl.num_programs(0)]
    o_ref[...] = jnp.zeros_like(o_ref)

    def tile_copy(j):
        slot = j & (DMA_RING - 1)
        start = pl.multiple_of((sa_ref[j] & 0xFFFF) * ROUTE_TILE, ROUTE_TILE)
        return pltpu.make_async_copy(t_hbm.at[pl.ds(start, ROUTE_TILE)], buf.at[slot], sem.at[slot])

    @pl.when(b == 0)
    def _():
        for i in range(DMA_RING - 1):
            @pl.when(i < total)
            def _(i=i):
                tile_copy(i).start()

    def body(j, carry):
        tile_copy(j).wait()

        @pl.when(j + (DMA_RING - 1) < total)
        def _():
            tile_copy(j + (DMA_RING - 1)).start()

        tile = sa_ref[j] & 0xFFFF
        a = pl.multiple_of(lax.shift_right_logical(sa_ref[j], 16), 16)
        lo = rr_ref[j] & 0xFFFF
        hi = lax.shift_right_logical(rr_ref[j], 16)
        local = a + lax.broadcasted_iota(jnp.int32, (GATHER_ROWS, ROUTE_TILE), 0)
        rows = jnp.where((local >= lo) & (local < hi), local + b * EXPERT_ROWS, -1)
        d = dst_ref[tile]
        sel = (d[0:1, :] == rows) | (d[1:2, :] == rows)
        win = pl.ds(a, GATHER_ROWS)
        picked = _dot(jnp.where(sel, 1.0, 0.0).astype(BF16), buf[j & (DMA_RING - 1)])
        o_ref[win, :] = o_ref[win, :] + picked.astype(BF16)
        return carry

    lax.fori_loop(es_ref[b], es_ref[b + 1], body, 0)


def _gather(lists, t, dst, n_blocks):
    n_tiles = dst.shape[0]
    spec = pltpu.PrefetchScalarGridSpec(
        num_scalar_prefetch=3,
        grid=(n_blocks,),
        in_specs=[
            pl.BlockSpec((n_tiles, 2, ROUTE_TILE), lambda b, es, sa, rr: (0, 0, 0)),
            pl.BlockSpec(memory_space=pl.ANY),
        ],
        out_specs=pl.BlockSpec((EXPERT_ROWS, D_MODEL), lambda b, es, sa, rr: (b, 0)),
        scratch_shapes=[
            pltpu.VMEM((DMA_RING, ROUTE_TILE, D_MODEL), BF16),
            pltpu.SemaphoreType.DMA((DMA_RING,)),
        ],
    )
    return pl.pallas_call(
        _gather_kernel,
        grid_spec=spec,
        out_shape=jax.ShapeDtypeStruct((n_blocks * EXPERT_ROWS, D_MODEL), BF16),
        compiler_params=_params(("arbitrary",)),
        name="moe_gather",
    )(*lists, dst, t)


def _expert_kernel(be_ref, bv_ref, x_ref, w1_ref, w3_ref, w2_ref, o_ref, h_ref):
    used = bv_ref[pl.program_id(0)] == 1

    @pl.when(used)
    def _():
        _swiglu_hidden(x_ref[...], w1_ref, w3_ref, h_ref)
        o_ref[...] = _dot(h_ref[...], w2_ref[...]).astype(BF16)

    @pl.when(jnp.logical_not(used))
    def _():
        o_ref[...] = jnp.zeros_like(o_ref)


def _experts(block_e, block_used, xs, w1, w3, w2, j):
    n_blocks = block_e.shape[0]
    resident = pl.Buffered(1)
    spec = pltpu.PrefetchScalarGridSpec(
        num_scalar_prefetch=2,
        grid=(n_blocks,),
        in_specs=[
            pl.BlockSpec((EXPERT_ROWS, D_MODEL), lambda b, be, bv: (b, 0)),
            pl.BlockSpec((None, None, D_MODEL, EXPERT_FF), lambda b, be, bv: (j, be[b], 0, 0), pipeline_mode=resident),
            pl.BlockSpec((None, None, D_MODEL, EXPERT_FF), lambda b, be, bv: (j, be[b], 0, 0), pipeline_mode=resident),
            pl.BlockSpec((None, None, EXPERT_FF, D_MODEL), lambda b, be, bv: (j, be[b], 0, 0), pipeline_mode=resident),
        ],
        out_specs=pl.BlockSpec((EXPERT_ROWS, D_MODEL), lambda b, be, bv: (b, 0)),
        scratch_shapes=[pltpu.VMEM((EXPERT_ROWS, EXPERT_FF), BF16)],
    )
    return pl.pallas_call(
        _expert_kernel,
        grid_spec=spec,
        out_shape=jax.ShapeDtypeStruct(xs.shape, BF16),
        compiler_params=_params(("arbitrary",)),
        name="moe_experts",
    )(block_e, block_used, xs, w1, w3, w2)


COMBINE_ROWS = ROUTE_TILE
DW_LANES = 4


def _combine_kernel(es_ref, r_ref, e_ref, h_ref, dw_ref, x1_ref, gm_ref, lg_ref, lb_ref, y_hbm, o_ref,
                    buf, sem, acc_ref, d_ref, w_ref):
    s = pl.program_id(0)
    total = es_ref[pl.num_programs(0)]
    acc_ref[...] = jnp.zeros_like(acc_ref)
    for k in range(2):
        d_ref[k] = jnp.broadcast_to(dw_ref[:, k:k + 1].astype(jnp.int32), d_ref.shape[1:])
        w_ref[k] = jnp.broadcast_to(dw_ref[:, 2 + k:3 + k], w_ref.shape[1:])

    def win_copy(j):
        slot = j & (DMA_RING - 1)
        start = pl.multiple_of(r_ref[j], 16)
        return pltpu.make_async_copy(y_hbm.at[pl.ds(start, COMBINE_ROWS)], buf.at[slot], sem.at[slot])

    @pl.when(s == 0)
    def _():
        for i in range(DMA_RING - 1):
            @pl.when(i < total)
            def _(i=i):
                win_copy(i).start()

    def body(j, carry):
        win_copy(j).wait()

        @pl.when(j + (DMA_RING - 1) < total)
        def _():
            win_copy(j + (DMA_RING - 1)).start()

        lo = jnp.maximum(e_ref[j], r_ref[j])
        hi = jnp.minimum(h_ref[j], r_ref[j] + COMBINE_ROWS)
        d1 = d_ref[0]
        d2 = d_ref[1]
        in1 = (d1 >= lo) & (d1 < hi)
        in2 = (d2 >= lo) & (d2 < hi)
        wsel = jnp.where(in1, w_ref[0], 0.0) + jnp.where(in2, w_ref[1], 0.0)
        hit = jnp.where(in1, d1, jnp.where(in2, d2, -1)) - r_ref[j]
        hit = jnp.concatenate([hit] * (COMBINE_ROWS // 128), axis=1)
        col = lax.broadcasted_iota(jnp.int32, (ROUTE_TILE, COMBINE_ROWS), 1)
        picked = _dot(jnp.where(hit == col, 1.0, 0.0).astype(BF16), buf[j & (DMA_RING - 1)])
        acc_ref[...] = acc_ref[...] + jnp.concatenate([wsel] * (D_MODEL // 128), axis=1) * picked
        return carry

    lax.fori_loop(es_ref[s], es_ref[s + 1], body, 0)
    o_ref[...] = _layer_norm(DEEPNORM_ALPHA * x1_ref[...] + gm_ref[...] * acc_ref[...], lg_ref[...], lb_ref[...])


def _combine(lists, y, dw, x1, mod3, ln_g, ln_b, layer, n_lat, seq, batch):
    n = x1.shape[0]
    nl = n_lat // ROUTE_TILE
    tpb = seq // ROUTE_TILE

    def row(s):
        return jnp.where(s < nl, s // tpb, batch)

    spec = pltpu.PrefetchScalarGridSpec(
        num_scalar_prefetch=4,
        grid=(n // ROUTE_TILE,),
        in_specs=[
            pl.BlockSpec((ROUTE_TILE, DW_LANES), lambda s, es, rr, ee, hh: (s, 0)),
            pl.BlockSpec((ROUTE_TILE, D_MODEL), lambda s, es, rr, ee, hh: (s, 0)),
            pl.BlockSpec((None, 1, D_MODEL), lambda s, es, rr, ee, hh: (row(s), 0, 5)),
            pl.BlockSpec((None, 1, D_MODEL), lambda s, es, rr, ee, hh: (layer, 0, 0)),
            pl.BlockSpec((None, 1, D_MODEL), lambda s, es, rr, ee, hh: (layer, 0, 0)),
            pl.BlockSpec(memory_space=pl.ANY),
        ],
        out_specs=pl.BlockSpec((ROUTE_TILE, D_MODEL), lambda s, es, rr, ee, hh: (s, 0)),
        scratch_shapes=[
            pltpu.VMEM((DMA_RING, COMBINE_ROWS, D_MODEL), BF16),
            pltpu.SemaphoreType.DMA((DMA_RING,)),
            pltpu.VMEM((ROUTE_TILE, D_MODEL), F32),
            pltpu.VMEM((2, ROUTE_TILE, 128), jnp.int32),
            pltpu.VMEM((2, ROUTE_TILE, 128), F32),
        ],
    )
    return pl.pallas_call(
        _combine_kernel,
        grid_spec=spec,
        out_shape=jax.ShapeDtypeStruct((n, D_MODEL), F32),
        compiler_params=_params(("arbitrary",)),
        name="moe_combine",
    )(*lists, dw, x1, mod3, ln_g, ln_b, y)


def _moe(t, x1, mod3, wr, tri, w1, w3, w2, ln_g, ln_b, layer, j, n_lat, seq, batch):
    n = t.shape[0]
    n_tiles = n // ROUTE_TILE
    n_blocks = -(-(2 * n + N_EXPERTS * (EXPERT_ROWS - 1)) // EXPERT_ROWS) + 1
    i32 = jnp.int32

    route, before, total = _router(t, wr, tri)

    e12 = route[:, 0:2].astype(i32)
    rank = route[:, 4:6].astype(i32)
    counts = total[0, :N_EXPERTS].astype(i32)
    padded = (counts + EXPERT_ROWS - 1) // EXPERT_ROWS * EXPERT_ROWS
    pad_end = jnp.cumsum(padded)
    pad_start = pad_end - padded
    dest = pad_start[e12] + rank
    dst = dest.reshape(n_tiles, ROUTE_TILE, 2).transpose(0, 2, 1)
    dw = jnp.concatenate([dest.astype(F32), route[:, 2:4]], axis=1)
    block_start = jnp.arange(n_blocks, dtype=i32) * EXPERT_ROWS
    block_e = jnp.minimum(jnp.sum(block_start[:, None] >= pad_end[None, :], axis=1), N_EXPERTS - 1).astype(i32)
    block_used = (block_start < pad_end[-1]).astype(i32)

    cb = before[:, 0, :N_EXPERTS].astype(i32)
    ca = jnp.concatenate([cb[1:], counts[None, :]], axis=0)
    lo = pad_start[None, :] + cb
    hi = pad_start[None, :] + ca
    some = hi > lo
    tile_id = jnp.arange(n_tiles, dtype=i32)[:, None, None]

    def grouped(ok, group, n_groups, *values):
        key = jnp.where(ok, group, n_groups).reshape(-1)
        order = jnp.argsort(key, stable=True)
        starts = jnp.sum(key[None, :] < jnp.arange(n_groups + 1, dtype=i32)[:, None], axis=1, dtype=i32)
        return (starts,) + tuple(jnp.broadcast_to(v, ok.shape).reshape(-1)[order].astype(i32) for v in values)

    last_off = EXPERT_ROWS - GATHER_ROWS
    b_lo = lo // EXPERT_ROWS
    parts = []
    for blk, p_lo, p_hi in ((b_lo, lo, jnp.minimum(hi, (b_lo + 1) * EXPERT_ROWS)),
                            (b_lo + 1, (b_lo + 1) * EXPERT_ROWS, hi)):
        l_lo = p_lo - blk * EXPERT_ROWS
        l_hi = p_hi - blk * EXPERT_ROWS
        a0 = jnp.minimum(l_lo // 16 * 16, last_off)
        cut = a0 + GATHER_ROWS
        a1 = jnp.minimum(cut, last_off)
        parts.append((blk, a0, l_lo, jnp.minimum(l_hi, cut)))
        parts.append((blk, a1, jnp.maximum(l_lo, cut), l_hi))
    g_block, g_off, g_lo, g_hi = (jnp.stack(v, axis=-1) for v in zip(*parts))
    g_ok = some[:, :, None] & (g_hi > g_lo)
    g_start, g_tile, g_off, g_lo, g_hi = grouped(g_ok, g_block, n_blocks, tile_id, g_off, g_lo, g_hi)
    xs = _gather((g_start, g_tile | (g_off << 16), g_lo | (g_hi << 16)), t, dst, n_blocks)

    y = _experts(block_e, block_used, xs, w1, w3, w2, j)

    r0 = lo // 16 * 16
    c_row = jnp.stack([r0, r0 + COMBINE_ROWS], axis=-1)
    c_ok = jnp.stack([some, some & (hi > r0 + COMBINE_ROWS)], axis=-1)
    lists = grouped(c_ok, tile_id, n_tiles, c_row, lo[:, :, None], hi[:, :, None])
    return _combine(lists, y, dw, x1, mod3, ln_g, ln_b, layer, n_lat, seq, batch)


def _window_counts(n, w):
    t = np.arange(n)
    left = w // 2
    right = w - 1 - left
    return (np.minimum(t + right + 1, n) - np.maximum(t - left, 0)).astype(np.float32)


def _window_matrix(n, w):
    left = w // 2
    right = w - 1 - left
    t = np.arange(n)
    return ((t[None, :] >= t[:, None] - left) & (t[None, :] <= t[:, None] + right)).astype(np.float32)


def _pool_tables(seq, n_ctx):
    rows = seq // GRID_W
    per_tile = 256 // GRID_W
    mc = np.stack([np.kron(np.eye(per_tile, dtype=np.float32), _window_matrix(GRID_W, w)) for w in POOL_WINDOWS])
    inv = np.stack([1.0 / np.outer(_window_counts(rows, w), _window_counts(GRID_W, w)).reshape(seq)
                    for w in POOL_WINDOWS])
    inv = np.broadcast_to(inv[:, :, None], (4, seq, POOL_GROUP)).astype(np.float32)
    m1d = np.stack([_window_matrix(n_ctx, w) for w in POOL_WINDOWS])
    inv1d = np.stack([1.0 / _window_counts(n_ctx, w) for w in POOL_WINDOWS])
    inv1d = np.broadcast_to(inv1d[:, :, None], (4, n_ctx, POOL_GROUP)).astype(np.float32)
    return (jnp.asarray(mc, BF16), jnp.asarray(inv), jnp.asarray(m1d, BF16), jnp.asarray(inv1d))


def _rope_tables(seq):
    t = jnp.arange(seq)
    row = (t // GRID_W).astype(F32)
    col = (t % GRID_W).astype(F32)
    n_freq = RET_QK_DIM // 4
    inv = jnp.exp(-jnp.log(ROPE_BASE) * jnp.arange(n_freq, dtype=F32) / n_freq)
    ang = jnp.concatenate([row[:, None] * inv, col[:, None] * inv], -1)
    cos, sin = jnp.cos(ang), jnp.sin(ang)
    cos2 = jnp.concatenate([cos, cos], -1)
    sin2 = jnp.concatenate([-sin, sin], -1)
    cos2 = jnp.concatenate([cos2, jnp.ones((TOKEN_TILE, RET_QK_DIM), F32)], 0)
    sin2 = jnp.concatenate([sin2, jnp.zeros((TOKEN_TILE, RET_QK_DIM), F32)], 0)
    return cos2, sin2


def kernel(x, c, ctx, c_ctx, ada_w, ada_b, w_in, pool_w, pool_scale, w_pool_out, w_ret_out, ret_decay_logit,
           w_out, ln_mix_g, ln_mix_b, ln_ffn_g, ln_ffn_b, ffn_w1, ffn_w3, ffn_w2, moe_router, moe_w1, moe_w3,
           moe_w2):
    batch, seq, d = x.shape
    n_ctx = ctx.shape[1]
    n_lat = batch * seq
    depth = ada_w.shape[0]
    assert d == D_MODEL and depth == DEPTH and batch < MOD_ROWS
    assert seq % TOKEN_TILE == 0 and (batch * n_ctx) % TOKEN_TILE == 0 and n_lat % n_ctx == 0
    assert TOKEN_TILE % ROUTER_STEP == 0 and ROUTER_STEP % ROUTE_TILE == 0
    assert seq % FFN_ROWS == 0 and (batch * n_ctx) % FFN_ROWS == 0

    s_in = jnp.zeros((MOD_ROWS, d), F32).at[:batch].set(c).at[batch].set(c_ctx)
    mod = _modulation(s_in, ada_w, ada_b)
    cos2, sin2 = _rope_tables(seq)
    mc, inv, m1d, inv1d = _pool_tables(seq, n_ctx)
    tri = jnp.asarray(np.tril(np.ones((ROUTER_STEP, ROUTER_STEP), np.float32), -1), BF16)

    w_in_b = w_in.astype(BF16)
    pool_w_b = pool_w.astype(BF16)
    pool_s = pool_scale.reshape(depth, 1, POOL_WIDTH)
    wp_b, wr_b, wo_b = w_pool_out.astype(BF16), w_ret_out.astype(BF16), w_out.astype(BF16)
    ffn_b = ffn_w1.astype(BF16), ffn_w3.astype(BF16), ffn_w2.astype(BF16)
    moe_b = moe_w1.astype(BF16), moe_w3.astype(BF16), moe_w2.astype(BF16)
    ln_mix = ln_mix_g.reshape(depth, 1, d), ln_mix_b.reshape(depth, 1, d)
    ln_ffn = ln_ffn_g.reshape(depth, 1, d), ln_ffn_b.reshape(depth, 1, d)

    n_tok = n_lat + batch * n_ctx
    x_parts = (x.reshape(n_lat, d), ctx.reshape(batch * n_ctx, d))
    for l in range(depth):
        last = l == depth - 1
        mod3 = mod[l].reshape(MOD_ROWS, 1, 6 * d)
        pa, pb = _in_proj(x_parts, mod3, cos2, sin2, w_in_b, l, n_tok, n_lat, seq, batch)
        yp_lat = _pool_lat(pa, mc, inv, pool_w_b, pool_s, l, batch, seq)
        yp_ctx = _pool_ctx(pa, m1d, inv1d, pool_w_b, pool_s, l, batch, n_lat, n_ctx)
        dl = jnp.broadcast_to(ret_decay_logit[l].astype(F32)[:, :, None, None],
                              (2, RET_HEADS, RET_CHUNK, RET_V_DIM))
        z_lat, z_ctx = _retention(pa, pb, dl, batch, seq, n_ctx)
        n_out = n_lat if last else n_tok
        x1, t = _merge(x_parts, mod3, yp_lat, yp_ctx, z_lat, z_ctx, pb, wp_b, wr_b, wo_b,
                       *ln_mix, l, n_out, n_lat, seq, batch)
        j = l // 2
        if l % 2 == 0:
            xs = _ffn(t, x1, mod3, *ffn_b, *ln_ffn, l, j, n_lat, seq, batch)
        else:
            wr = jnp.zeros((d, ROUTE_LANES), BF16).at[:, :N_EXPERTS].set(moe_router[j].astype(BF16))
            xs = _moe(t, x1, mod3, wr, tri, *moe_b, *ln_ffn, l, j, n_lat, seq, batch)
        x_parts = (xs,)
    return xs[:n_lat].reshape(batch, seq, d)
```

```python
import functools

import jax
import jax.numpy as jnp
import numpy as np
from jax import lax
from jax.experimental import pallas as pl
from jax.experimental.pallas import tpu as pltpu

F32 = jnp.float32
BF16 = jnp.bfloat16

D_MODEL = 1024
DEPTH = 4
GRID_W = 64
POOL_WINDOWS = (2, 4, 8, 16)
POOL_GROUP = 128
POOL_WIDTH = POOL_GROUP * len(POOL_WINDOWS)
RET_HEADS = 4
RET_QK_DIM = 128
RET_V_DIM = 256
RET_CHUNK = 128
ROPE_BASE = 10000.0
OFF_Q = POOL_WIDTH
OFF_K = OFF_Q + RET_HEADS * RET_QK_DIM
OFF_V = OFF_K + RET_HEADS * RET_QK_DIM
OFF_G = OFF_V + RET_HEADS * RET_V_DIM
WIDTH_B = 3 * D_MODEL
D_FF = 2816
N_EXPERTS = 8
EXPERT_FF = 3584
DEEPNORM_ALPHA = (2 * DEPTH) ** 0.25
LN_EPS = 1e-5
K_SCALE = RET_QK_DIM ** -0.5

MOD_ROWS = 24
TOKEN_TILE = 512
ROUTE_TILE = 256
ROUTER_STEP = 512
EXPERT_ROWS = 512
FFN_ROWS = 2 * TOKEN_TILE
VMEM_LIMIT = 56 * 1024 * 1024


def _dot(a, b):
    return jnp.dot(a, b, preferred_element_type=F32)


def _sigmoid(x):
    return 1.0 / (1.0 + jnp.exp(-x))


def _split_bf16(a):
    hi = a.astype(BF16)
    lo = (a - hi.astype(F32)).astype(BF16)
    return hi, lo


def _layer_norm(v, g, b):
    mean = jnp.mean(v, axis=-1, keepdims=True)
    vc = v - mean
    var = jnp.mean(vc * vc, axis=-1, keepdims=True)
    return vc * lax.rsqrt(var + LN_EPS) * g + b


def _params(sem, vmem=VMEM_LIMIT):
    return pltpu.CompilerParams(dimension_semantics=sem, vmem_limit_bytes=vmem)


def _mod_kernel(s_ref, w_ref, b_ref, o_ref):
    s = s_ref[...]
    s = s * _sigmoid(s)
    s_hi, s_lo = _split_bf16(s)
    w_hi, w_lo = _split_bf16(w_ref[...])
    o_ref[...] = _dot(s_hi, w_hi) + (_dot(s_hi, w_lo) + _dot(s_lo, w_hi)) + b_ref[...]


def _modulation(s_in, ada_w, ada_b):
    depth, d, width = ada_w.shape
    tn = 1536
    return pl.pallas_call(
        _mod_kernel,
        grid=(depth, width // tn),
        in_specs=[
            pl.BlockSpec((MOD_ROWS, d), lambda l, j: (0, 0)),
            pl.BlockSpec((None, d, tn), lambda l, j: (l, 0, j)),
            pl.BlockSpec((None, 1, tn), lambda l, j: (l, 0, j)),
        ],
        out_specs=pl.BlockSpec((None, MOD_ROWS, tn), lambda l, j: (l, 0, j)),
        out_shape=jax.ShapeDtypeStruct((depth, MOD_ROWS, width), F32),
        compiler_params=_params(("arbitrary", "arbitrary")),
        name="adaln_mod",
    )(s_in, ada_w, ada_b.reshape(depth, 1, width))


IN_COLS = 512
IN_BLOCKS = (OFF_G + WIDTH_B) // IN_COLS


def _pick_x(x_refs, rows, is_lat):
    if len(x_refs) == 1:
        return x_refs[0][rows, :]
    return jnp.where(is_lat, x_refs[0][rows, :], x_refs[1][rows, :])


def _in_kernel(*refs, n_lat_tiles, n_x):
    x_refs = refs[:n_x]
    sh_ref, sc_ref, cos_ref, sin_ref = refs[n_x:n_x + 4]
    w_refs = refs[n_x + 4:n_x + 4 + IN_BLOCKS]
    pa_ref, pb_ref = refs[n_x + 4 + IN_BLOCKS:]
    x = _pick_x(x_refs, slice(None), pl.program_id(0) < n_lat_tiles)
    h = (x * (1.0 + sc_ref[...]) + sh_ref[...]).astype(BF16)
    cos = cos_ref[...]
    sin = sin_ref[...]
    for blk in range(IN_BLOCKS):
        c0 = blk * IN_COLS
        acc = _dot(h, w_refs[blk][...])
        if c0 in (OFF_Q, OFF_K):
            for hh in range(RET_HEADS):
                t = acc[:, hh * RET_QK_DIM:(hh + 1) * RET_QK_DIM]
                r = t * cos + pltpu.roll(t, RET_QK_DIM // 2, 1) * sin
                if c0 == OFF_K:
                    r = r * K_SCALE
                pa_ref[:, c0 + hh * RET_QK_DIM:c0 + (hh + 1) * RET_QK_DIM] = r.astype(BF16)
        elif c0 < OFF_G:
            pa_ref[:, c0:c0 + IN_COLS] = acc.astype(BF16)
        else:
            pb_ref[:, c0 - OFF_G:c0 - OFF_G + IN_COLS] = acc.astype(BF16)


def _tile_maps(n_lat, seq, batch, tile):
    nl = n_lat // tile
    tpb = seq // tile

    def row(i):
        return jnp.where(i < nl, i // tpb, batch)

    def lat(i):
        return jnp.minimum(i, nl - 1)

    def ctx(i):
        return jnp.maximum(i - nl, 0)

    return nl, tpb, row, lat, ctx


def _x_operands(x_parts, n_lat, seq, batch, tile):
    _, _, _, lat, ctx = _tile_maps(n_lat, seq, batch, tile)
    if len(x_parts) == 1:
        return [pl.BlockSpec((tile, D_MODEL), lambda i: (i, 0))], list(x_parts)
    return ([pl.BlockSpec((tile, D_MODEL), lambda i: (lat(i), 0)),
             pl.BlockSpec((tile, D_MODEL), lambda i: (ctx(i), 0))], list(x_parts))


def _in_proj(x_parts, mod3, cos2, sin2, w_in, layer, n_tok, n_lat, seq, batch):
    tm = TOKEN_TILE
    nl, tpb, row, _, _ = _tile_maps(n_lat, seq, batch, tm)
    x_specs, x_args = _x_operands(x_parts, n_lat, seq, batch, tm)

    def rope(i):
        return jnp.where(i < nl, i % tpb, tpb)

    def w_spec(blk):
        return pl.BlockSpec((None, D_MODEL, IN_COLS), lambda i: (layer, 0, blk))

    return pl.pallas_call(
        functools.partial(_in_kernel, n_lat_tiles=nl, n_x=len(x_args)),
        grid=(n_tok // tm,),
        in_specs=x_specs + [
            pl.BlockSpec((None, 1, D_MODEL), lambda i: (row(i), 0, 0)),
            pl.BlockSpec((None, 1, D_MODEL), lambda i: (row(i), 0, 1)),
            pl.BlockSpec((tm, RET_QK_DIM), lambda i: (rope(i), 0)),
            pl.BlockSpec((tm, RET_QK_DIM), lambda i: (rope(i), 0)),
        ] + [w_spec(blk) for blk in range(IN_BLOCKS)],
        out_specs=[
            pl.BlockSpec((tm, OFF_G), lambda i: (i, 0)),
            pl.BlockSpec((tm, WIDTH_B), lambda i: (i, 0)),
        ],
        out_shape=[
            jax.ShapeDtypeStruct((n_tok, OFF_G), BF16),
            jax.ShapeDtypeStruct((n_tok, WIDTH_B), BF16),
        ],
        compiler_params=_params(("arbitrary",)),
        name="in_proj",
    )(*x_args, mod3, mod3, cos2, sin2, *([w_in] * IN_BLOCKS))


POOL_PAD_ROWS = 8
POOL_ROW_CHUNK = 8


def _pool_kernel(u_ref, mc_ref, inv_ref, pw_ref, ps_ref, o_ref, zp_ref, *, rows):
    seq = rows * GRID_W
    pad = POOL_PAD_ROWS * GRID_W
    chunk = POOL_ROW_CHUNK * GRID_W
    zeros = jnp.zeros((pad, POOL_GROUP), F32)
    for g, w in enumerate(POOL_WINDOWS):
        lanes = slice(g * POOL_GROUP, (g + 1) * POOL_GROUP)
        zp_ref[0:pad, :] = zeros
        zp_ref[pad + seq:pad + seq + pad, :] = zeros
        for c in range(seq // 256):
            zp_ref[pad + c * 256:pad + (c + 1) * 256, :] = _dot(mc_ref[g], u_ref[c * 256:(c + 1) * 256, lanes])
        left = w // 2

        def body(rc, carry, g=g, w=w, left=left, lanes=lanes):
            tok = pl.multiple_of(rc * chunk, chunk)
            acc = zp_ref[pl.ds(tok + pad - left * GRID_W, chunk), :]
            for k in range(1, w):
                acc = acc + zp_ref[pl.ds(tok + pad + (k - left) * GRID_W, chunk), :]
            pooled = acc * inv_ref[g, pl.ds(tok, chunk), :]
            d = (pooled - u_ref[pl.ds(tok, chunk), lanes].astype(F32)).astype(BF16)
            y = _dot(d, pw_ref[g]) * ps_ref[:, lanes]
            o_ref[pl.ds(tok, chunk), lanes] = y.astype(BF16)
            return carry

        lax.fori_loop(0, rows // POOL_ROW_CHUNK, body, 0)


def _pool_lat(pa, mc, inv, pw, ps, layer, batch, seq):
    rows = seq // GRID_W
    return pl.pallas_call(
        functools.partial(_pool_kernel, rows=rows),
        grid=(batch,),
        in_specs=[
            pl.BlockSpec((seq, POOL_WIDTH), lambda b: (b, 0)),
            pl.BlockSpec((4, 256, 256), lambda b: (0, 0, 0)),
            pl.BlockSpec((4, seq, POOL_GROUP), lambda b: (0, 0, 0)),
            pl.BlockSpec((None, 4, POOL_GROUP, POOL_GROUP), lambda b: (layer, 0, 0, 0)),
            pl.BlockSpec((None, 1, POOL_WIDTH), lambda b: (layer, 0, 0)),
        ],
        out_specs=pl.BlockSpec((seq, POOL_WIDTH), lambda b: (b, 0)),
        out_shape=jax.ShapeDtypeStruct((batch * seq, POOL_WIDTH), BF16),
        scratch_shapes=[pltpu.VMEM(((rows + 2 * POOL_PAD_ROWS) * GRID_W, POOL_GROUP), F32)],
        compiler_params=_params(("arbitrary",)),
        name="pool_lat",
    )(pa, mc, inv, pw, ps)


def _pool_ctx_kernel(u_ref, m_ref, inv_ref, pw_ref, ps_ref, o_ref):
    for g in range(len(POOL_WINDOWS)):
        lanes = slice(g * POOL_GROUP, (g + 1) * POOL_GROUP)
        ug = u_ref[:, lanes]
        pooled = _dot(m_ref[g], ug) * inv_ref[g]
        d = (pooled - ug.astype(F32)).astype(BF16)
        o_ref[:, lanes] = (_dot(d, pw_ref[g]) * ps_ref[:, lanes]).astype(BF16)


def _pool_ctx(pa, m1d, inv1d, pw, ps, layer, batch, n_lat, n_ctx):
    first = n_lat // n_ctx
    return pl.pallas_call(
        _pool_ctx_kernel,
        grid=(batch,),
        in_specs=[
            pl.BlockSpec((n_ctx, POOL_WIDTH), lambda b: (first + b, 0)),
            pl.BlockSpec((4, n_ctx, n_ctx), lambda b: (0, 0, 0)),
            pl.BlockSpec((4, n_ctx, POOL_GROUP), lambda b: (0, 0, 0)),
            pl.BlockSpec((None, 4, POOL_GROUP, POOL_GROUP), lambda b: (layer, 0, 0, 0)),
            pl.BlockSpec((None, 1, POOL_WIDTH), lambda b: (layer, 0, 0)),
        ],
        out_specs=pl.BlockSpec((n_ctx, POOL_WIDTH), lambda b: (b, 0)),
        out_shape=jax.ShapeDtypeStruct((batch * n_ctx, POOL_WIDTH), BF16),
        compiler_params=_params(("arbitrary",)),
        name="pool_ctx",
    )(pa, m1d, inv1d, pw, ps)


def _log_sigmoid(x):
    return jnp.minimum(x, 0.0) - jnp.log1p(jnp.exp(-jnp.abs(x)))


def _decays(dl, backward):
    c = RET_CHUNK
    lg = _log_sigmoid(dl)
    lgq = lg[:, :RET_QK_DIM]
    ii = lax.broadcasted_iota(jnp.int32, (c, c), 0)
    jj = lax.broadcasted_iota(jnp.int32, (c, c), 1)
    pos = lax.broadcasted_iota(jnp.int32, (c, RET_QK_DIM), 0).astype(F32)
    if backward:
        diff = (jj - ii).astype(F32)
        qdec = jnp.exp(lgq * (c - pos))
        kdec = jnp.exp(lgq * pos)
    else:
        diff = (ii - jj).astype(F32)
        qdec = jnp.exp(lgq * (pos + 1.0))
        kdec = jnp.exp(lgq * (c - 1.0 - pos))
    inner = jnp.where(diff >= 0, jnp.exp(lgq * jnp.maximum(diff, 0.0)), 0.0)
    cdec = jnp.exp(lg * float(c))
    return inner, qdec, kdec, cdec


def _chunk_kv(kc, vc, kdec):
    kd = (kc.astype(F32) * kdec).astype(BF16)
    return lax.dot_general(kd, vc, (((0,), (0,)), ((), ())), preferred_element_type=F32)


def _chunk_out(qc, kc, vc, gc, states, mask, qdec2):
    scores = lax.dot_general(qc, kc, (((1,), (1,)), ((), ())), preferred_element_type=F32) * mask
    qf = qc.astype(F32)
    qd = (jnp.concatenate([qf, qf], axis=1) * qdec2).astype(BF16)
    o = _dot(scores.astype(BF16), vc) + _dot(qd, states)
    mean = jnp.mean(o, axis=-1, keepdims=True)
    oc = o - mean
    var = jnp.mean(oc * oc, axis=-1, keepdims=True)
    on = oc * lax.rsqrt(var + LN_EPS)
    gf = gc.astype(F32)
    return (gf * _sigmoid(gf) * on).astype(BF16)


def _ret_kernel(q_ref, k_ref, v_ref, g_ref, qc_ref, kc_ref, vc_ref, gc_ref, dl_ref,
                z_ref, zc_ref, st_ref, stc_ref, run_ref, *, n_chunks, n_ctx_chunks):
    c = RET_CHUNK
    dk = RET_QK_DIM
    inner_f, qdec_f, kdec_f, cdec_f = _decays(dl_ref[0], False)
    inner_b, qdec_b, kdec_b, cdec_b = _decays(dl_ref[1], True)
    mask = inner_f + inner_b
    qdec2 = jnp.concatenate([qdec_f, qdec_b], axis=1)

    def rows(j):
        return slice(j * c, (j + 1) * c)

    s = jnp.zeros((dk, RET_V_DIM), F32)
    for j in range(n_ctx_chunks):
        stc_ref[j, 0:dk, :] = s.astype(BF16)
        s = s * cdec_f + _chunk_kv(kc_ref[rows(j), :], vc_ref[rows(j), :], kdec_f)
    run_ref[0] = s
    s = jnp.zeros((dk, RET_V_DIM), F32)
    for j in reversed(range(n_ctx_chunks)):
        stc_ref[j, dk:2 * dk, :] = s.astype(BF16)
        s = s * cdec_b + _chunk_kv(kc_ref[rows(j), :], vc_ref[rows(j), :], kdec_b)
    run_ref[1] = s
    for j in range(n_ctx_chunks):
        zc_ref[rows(j), :] = _chunk_out(qc_ref[rows(j), :], kc_ref[rows(j), :], vc_ref[rows(j), :],
                                        gc_ref[rows(j), :], stc_ref[j], mask, qdec2)

    def scan(t, carry):
        rf = pl.ds(pl.multiple_of(t * c, c), c)
        sf = run_ref[0]
        st_ref[t, 0:dk, :] = sf.astype(BF16)
        run_ref[0] = sf * cdec_f + _chunk_kv(k_ref[rf, :], v_ref[rf, :], kdec_f)
        tb = n_chunks - 1 - t
        rb = pl.ds(pl.multiple_of(tb * c, c), c)
        sb = run_ref[1]
        st_ref[tb, dk:2 * dk, :] = sb.astype(BF16)
        run_ref[1] = sb * cdec_b + _chunk_kv(k_ref[rb, :], v_ref[rb, :], kdec_b)
        return carry

    lax.fori_loop(0, n_chunks, scan, 0, unroll=4)

    def emit(t, carry):
        r = pl.ds(pl.multiple_of(t * c, c), c)
        z_ref[r, :] = _chunk_out(q_ref[r, :], k_ref[r, :], v_ref[r, :], g_ref[r, :], st_ref[t], mask, qdec2)
        return carry

    lax.fori_loop(0, n_chunks, emit, 0, unroll=8)


def _retention(pa, pb, dl, batch, seq, n_ctx):
    n_lat = batch * seq
    first = n_lat // n_ctx
    qk0 = OFF_Q // RET_QK_DIM
    kk0 = OFF_K // RET_QK_DIM
    v0 = OFF_V // RET_V_DIM
    return pl.pallas_call(
        functools.partial(_ret_kernel, n_chunks=seq // RET_CHUNK, n_ctx_chunks=n_ctx // RET_CHUNK),
        grid=(batch, RET_HEADS),
        in_specs=[
            pl.BlockSpec((seq, RET_QK_DIM), lambda b, h: (b, qk0 + h)),
            pl.BlockSpec((seq, RET_QK_DIM), lambda b, h: (b, kk0 + h)),
            pl.BlockSpec((seq, RET_V_DIM), lambda b, h: (b, v0 + h)),
            pl.BlockSpec((seq, RET_V_DIM), lambda b, h: (b, h)),
            pl.BlockSpec((n_ctx, RET_QK_DIM), lambda b, h: (first + b, qk0 + h)),
            pl.BlockSpec((n_ctx, RET_QK_DIM), lambda b, h: (first + b, kk0 + h)),
            pl.BlockSpec((n_ctx, RET_V_DIM), lambda b, h: (first + b, v0 + h)),
            pl.BlockSpec((n_ctx, RET_V_DIM), lambda b, h: (first + b, h)),
            pl.BlockSpec((2, None, RET_CHUNK, RET_V_DIM), lambda b, h: (0, h, 0, 0)),
        ],
        out_specs=[
            pl.BlockSpec((seq, RET_V_DIM), lambda b, h: (b, h)),
            pl.BlockSpec((n_ctx, RET_V_DIM), lambda b, h: (b, h)),
        ],
        out_shape=[
            jax.ShapeDtypeStruct((n_lat, RET_HEADS * RET_V_DIM), BF16),
            jax.ShapeDtypeStruct((batch * n_ctx, RET_HEADS * RET_V_DIM), BF16),
        ],
        scratch_shapes=[
            pltpu.VMEM((seq // RET_CHUNK, 2 * RET_QK_DIM, RET_V_DIM), BF16),
            pltpu.VMEM((n_ctx // RET_CHUNK, 2 * RET_QK_DIM, RET_V_DIM), BF16),
            pltpu.VMEM((2, RET_QK_DIM, RET_V_DIM), F32),
        ],
        compiler_params=_params(("arbitrary", "arbitrary")),
        name="retention",
    )(pa, pa, pa, pb, pa, pa, pa, pb, dl)


MIX_COLS = 256


def _merge_kernel(*refs, n_lat_tiles, n_x):
    x_refs = refs[:n_x]
    (gm_ref, sh_ref, sc_ref, ypl_ref, ypc_ref, zl_ref, zc_ref, gp_ref, gr_ref,
     wp_ref, wr_ref, wo_ref, lg_ref, lb_ref, x1_ref, t_ref, mix_ref) = refs[n_x:]
    is_lat = pl.program_id(0) < n_lat_tiles
    yp = jnp.where(is_lat, ypl_ref[...], ypc_ref[...])
    z = jnp.where(is_lat, zl_ref[...], zc_ref[...])
    x = _pick_x(x_refs, slice(None), is_lat)
    for c0 in range(0, D_MODEL, MIX_COLS):
        cols = slice(c0, c0 + MIX_COLS)
        y_pool = _dot(yp, wp_ref[:, cols])
        y_ret = _dot(z, wr_ref[:, cols])
        mix = _sigmoid(gp_ref[:, cols].astype(F32)) * y_pool + _sigmoid(gr_ref[:, cols].astype(F32)) * y_ret
        mix_ref[:, cols] = mix.astype(BF16)
    y = _dot(mix_ref[...], wo_ref[...])
    x1 = _layer_norm(DEEPNORM_ALPHA * x + gm_ref[...] * y, lg_ref[...], lb_ref[...])
    x1_ref[...] = x1
    t_ref[...] = (x1 * (1.0 + sc_ref[...]) + sh_ref[...]).astype(BF16)


def _merge(x_parts, mod3, yp_lat, yp_ctx, z_lat, z_ctx, pb, wp, wr, wo, ln_g, ln_b, layer,
           n_out, n_lat, seq, batch):
    tm = TOKEN_TILE
    nl, _, row, lat, ctx = _tile_maps(n_lat, seq, batch, tm)
    x_specs, x_args = _x_operands(x_parts, n_lat, seq, batch, tm)

    return pl.pallas_call(
        functools.partial(_merge_kernel, n_lat_tiles=nl, n_x=len(x_args)),
        grid=(n_out // tm,),
        in_specs=x_specs + [
            pl.BlockSpec((None, 1, D_MODEL), lambda i: (row(i), 0, 2)),
            pl.BlockSpec((None, 1, D_MODEL), lambda i: (row(i), 0, 3)),
            pl.BlockSpec((None, 1, D_MODEL), lambda i: (row(i), 0, 4)),
            pl.BlockSpec((tm, POOL_WIDTH), lambda i: (lat(i), 0)),
            pl.BlockSpec((tm, POOL_WIDTH), lambda i: (ctx(i), 0)),
            pl.BlockSpec((tm, D_MODEL), lambda i: (lat(i), 0)),
            pl.BlockSpec((tm, D_MODEL), lambda i: (ctx(i), 0)),
            pl.BlockSpec((tm, D_MODEL), lambda i: (i, 1)),
            pl.BlockSpec((tm, D_MODEL), lambda i: (i, 2)),
            pl.BlockSpec((None, POOL_WIDTH, D_MODEL), lambda i: (layer, 0, 0)),
            pl.BlockSpec((None, D_MODEL, D_MODEL), lambda i: (layer, 0, 0)),
            pl.BlockSpec((None, D_MODEL, D_MODEL), lambda i: (layer, 0, 0)),
            pl.BlockSpec((None, 1, D_MODEL), lambda i: (layer, 0, 0)),
            pl.BlockSpec((None, 1, D_MODEL), lambda i: (layer, 0, 0)),
        ],
        out_specs=[
            pl.BlockSpec((tm, D_MODEL), lambda i: (i, 0)),
            pl.BlockSpec((tm, D_MODEL), lambda i: (i, 0)),
        ],
        out_shape=[
            jax.ShapeDtypeStruct((n_out, D_MODEL), F32),
            jax.ShapeDtypeStruct((n_out, D_MODEL), BF16),
        ],
        scratch_shapes=[pltpu.VMEM((tm, D_MODEL), BF16)],
        compiler_params=_params(("arbitrary",)),
        name="merge",
    )(*x_args, mod3, mod3, mod3, yp_lat, yp_ctx, z_lat, z_ctx, pb, pb, wp, wr, wo, ln_g, ln_b)


SWIGLU_COLS = 256


def _swiglu_hidden(t, w1_ref, w3_ref, h_ref):
    width = h_ref.shape[1]
    for c0 in range(0, width, SWIGLU_COLS):
        cols = slice(c0, min(c0 + SWIGLU_COLS, width))
        a = _dot(t, w1_ref[:, cols])
        h_ref[:, cols] = (a * _sigmoid(a) * _dot(t, w3_ref[:, cols])).astype(BF16)


def _ffn_kernel(t_ref, x1_ref, gm_ref, w1_ref, w3_ref, w2_ref, lg_ref, lb_ref, o_ref, h_ref):
    for r in range(FFN_ROWS // TOKEN_TILE):
        rows = slice(r * TOKEN_TILE, (r + 1) * TOKEN_TILE)
        _swiglu_hidden(t_ref[rows, :], w1_ref, w3_ref, h_ref.at[r])
        f = _dot(h_ref[r], w2_ref[...])
        o_ref[rows, :] = _layer_norm(DEEPNORM_ALPHA * x1_ref[rows, :] + gm_ref[...] * f, lg_ref[...], lb_ref[...])


def _ffn(t, x1, mod3, w1, w3, w2, ln_g, ln_b, layer, j, n_lat, seq, batch):
    n = t.shape[0]
    tm = FFN_ROWS
    _, _, row, _, _ = _tile_maps(n_lat, seq, batch, tm)
    resident = pl.Buffered(1)

    return pl.pallas_call(
        _ffn_kernel,
        grid=(n // tm,),
        in_specs=[
            pl.BlockSpec((tm, D_MODEL), lambda i: (i, 0)),
            pl.BlockSpec((tm, D_MODEL), lambda i: (i, 0)),
            pl.BlockSpec((None, 1, D_MODEL), lambda i: (row(i), 0, 5)),
            pl.BlockSpec((None, D_MODEL, D_FF), lambda i: (j, 0, 0), pipeline_mode=resident),
            pl.BlockSpec((None, D_MODEL, D_FF), lambda i: (j, 0, 0), pipeline_mode=resident),
            pl.BlockSpec((None, D_FF, D_MODEL), lambda i: (j, 0, 0), pipeline_mode=resident),
            pl.BlockSpec((None, 1, D_MODEL), lambda i: (layer, 0, 0)),
            pl.BlockSpec((None, 1, D_MODEL), lambda i: (layer, 0, 0)),
        ],
        out_specs=pl.BlockSpec((tm, D_MODEL), lambda i: (i, 0)),
        out_shape=jax.ShapeDtypeStruct((n, D_MODEL), F32),
        scratch_shapes=[pltpu.VMEM((tm // TOKEN_TILE, TOKEN_TILE, D_FF), BF16)],
        compiler_params=_params(("arbitrary",)),
        name="ffn_dense",
    )(t, x1, mod3, w1, w3, w2, ln_g, ln_b)


ROUTE_LANES = 128


def _router_kernel(t_ref, wr_ref, tri_ref, route_ref, before_ref, total_ref, run_ref):
    @pl.when(pl.program_id(0) == 0)
    def _():
        run_ref[...] = jnp.zeros_like(run_ref)

    tt = t_ref.shape[0]
    logits = _dot(t_ref[...], wr_ref[...])
    lane = lax.broadcasted_iota(jnp.int32, (tt, ROUTE_LANES), 1)
    neg = jnp.float32(-jnp.inf)
    lg = jnp.where(lane < N_EXPERTS, logits, neg)
    m1 = jnp.max(lg, axis=1, keepdims=True)
    i1 = jnp.min(jnp.where(lg == m1, lane, ROUTE_LANES), axis=1, keepdims=True)
    lg2 = jnp.where(lane == i1, neg, lg)
    m2 = jnp.max(lg2, axis=1, keepdims=True)
    i2 = jnp.min(jnp.where(lg2 == m2, lane, ROUTE_LANES), axis=1, keepdims=True)
    e = jnp.exp(m2 - m1)
    w1 = 1.0 / (1.0 + e)
    w2 = e / (1.0 + e)
    hit1 = lane == i1
    hit2 = lane == i2
    onehot = jnp.where(hit1 | hit2, 1.0, 0.0)
    run = run_ref[...]
    prefix = _dot(tri_ref[...], onehot.astype(BF16)) + run
    r1 = jnp.sum(jnp.where(hit1, prefix, 0.0), axis=1, keepdims=True)
    r2 = jnp.sum(jnp.where(hit2, prefix, 0.0), axis=1, keepdims=True)
    for k in range(tt // ROUTE_TILE):
        before_ref[k] = prefix[k * ROUTE_TILE:k * ROUTE_TILE + 1, :]
    run = run + jnp.sum(onehot, axis=0, keepdims=True)
    run_ref[...] = run
    total_ref[...] = run
    out = jnp.where(lane == 0, i1.astype(F32), 0.0)
    out = jnp.where(lane == 1, i2.astype(F32), out)
    out = jnp.where(lane == 2, w1, out)
    out = jnp.where(lane == 3, w2, out)
    out = jnp.where(lane == 4, r1, out)
    out = jnp.where(lane == 5, r2, out)
    route_ref[...] = out


def _router(t, wr, tri):
    n = t.shape[0]
    tt = ROUTER_STEP
    sub = tt // ROUTE_TILE
    return pl.pallas_call(
        _router_kernel,
        grid=(n // tt,),
        in_specs=[
            pl.BlockSpec((tt, D_MODEL), lambda i: (i, 0)),
            pl.BlockSpec((D_MODEL, ROUTE_LANES), lambda i: (0, 0)),
            pl.BlockSpec((tt, tt), lambda i: (0, 0)),
        ],
        out_specs=[
            pl.BlockSpec((tt, ROUTE_LANES), lambda i: (i, 0)),
            pl.BlockSpec((sub, 1, ROUTE_LANES), lambda i: (i, 0, 0)),
            pl.BlockSpec((1, ROUTE_LANES), lambda i: (0, 0)),
        ],
        out_shape=[
            jax.ShapeDtypeStruct((n, ROUTE_LANES), F32),
            jax.ShapeDtypeStruct((n // ROUTE_TILE, 1, ROUTE_LANES), F32),
            jax.ShapeDtypeStruct((1, ROUTE_LANES), F32),
        ],
        scratch_shapes=[pltpu.VMEM((1, ROUTE_LANES), F32)],
        compiler_params=_params(("arbitrary",)),
        name="moe_router",
    )(t, wr, tri)


DMA_RING = 16
DMA_AHEAD = 8
GATHER_ROWS = 144
GATHER_SPLIT = 4


def _gather_kernel(es_ref, sa_ref, rr_ref, dst_ref, t_hbm, o_ref, buf, sem):
    b = pl.program_id(0)
    total = es_ref[pl.num_programs(0)]
    o_ref[...] = jnp.zeros_like(o_ref)

    def tile_copy(j):
        slot = j & (DMA_RING - 1)
        start = pl.multiple_of((sa_ref[j] & 0xFFFF) * ROUTE_TILE, ROUTE_TILE)
        return pltpu.make_async_copy(t_hbm.at[pl.ds(start, ROUTE_TILE)], buf.at[slot], sem.at[slot])

    @pl.when(b == 0)
    def _():
        for i in range(DMA_AHEAD):
            @pl.when(i < total)
            def _(i=i):
                tile_copy(i).start()

    def entry(j, slot, owned):
        tile = sa_ref[j] & 0xFFFF
        a = pl.multiple_of(lax.shift_right_logical(sa_ref[j], 16), 16)
        lo = rr_ref[j] & 0xFFFF
        hi = jnp.where(owned, lax.shift_right_logical(rr_ref[j], 16), 0)
        local = a + lax.broadcasted_iota(jnp.int32, (GATHER_ROWS, ROUTE_TILE), 0)
        rows = jnp.where((local >= lo) & (local < hi), local + b * EXPERT_ROWS, -1)
        d = dst_ref[tile]
        sel = (d[0:1, :] == rows) | (d[1:2, :] == rows)
        win = pl.ds(a, GATHER_ROWS)
        picked = _dot(jnp.where(sel, 1.0, 0.0).astype(BF16), buf[slot])
        o_ref[win, :] = o_ref[win, :] + picked.astype(BF16)

    j0 = es_ref[b]
    j1 = es_ref[b + 1]

    def body(k, carry):
        ja = j0 + 2 * k
        jb = ja + 1
        has_b = jb < j1
        tile_copy(ja).wait()

        @pl.when(ja + DMA_AHEAD < total)
        def _():
            tile_copy(ja + DMA_AHEAD).start()

        @pl.when(has_b)
        def _():
            tile_copy(jb).wait()

        @pl.when(has_b & (jb + DMA_AHEAD < total))
        def _():
            tile_copy(jb + DMA_AHEAD).start()

        entry(ja, ja & (DMA_RING - 1), True)
        entry(jb, jnp.where(has_b, jb, ja) & (DMA_RING - 1), has_b)
        return carry

    lax.fori_loop(0, (j1 - j0 + 1) // 2, body, 0)


def _gather(lists, t, dst, n_blocks):
    n_tiles = dst.shape[0]
    spec = pltpu.PrefetchScalarGridSpec(
        num_scalar_prefetch=3,
        grid=(n_blocks,),
        in_specs=[
            pl.BlockSpec((n_tiles, 2, ROUTE_TILE), lambda b, es, sa, rr: (0, 0, 0)),
            pl.BlockSpec(memory_space=pl.ANY),
        ],
        out_specs=pl.BlockSpec((EXPERT_ROWS, D_MODEL), lambda b, es, sa, rr: (b, 0)),
        scratch_shapes=[
            pltpu.VMEM((DMA_RING, ROUTE_TILE, D_MODEL), BF16),
            pltpu.SemaphoreType.DMA((DMA_RING,)),
        ],
    )
    return pl.pallas_call(
        _gather_kernel,
        grid_spec=spec,
        out_shape=jax.ShapeDtypeStruct((n_blocks * EXPERT_ROWS, D_MODEL), BF16),
        compiler_params=_params(("arbitrary",)),
        name="moe_gather",
    )(*lists, dst, t)


def _expert_kernel(be_ref, bv_ref, x_ref, w1_ref, w3_ref, w2_ref, o_ref, h_ref):
    used = bv_ref[pl.program_id(0)] == 1

    @pl.when(used)
    def _():
        _swiglu_hidden(x_ref[...], w1_ref, w3_ref, h_ref)
        o_ref[...] = _dot(h_ref[...], w2_ref[...]).astype(BF16)

    @pl.when(jnp.logical_not(used))
    def _():
        o_ref[...] = jnp.zeros_like(o_ref)


def _experts(block_e, block_used, xs, w1, w3, w2, j):
    n_blocks = block_e.shape[0]
    resident = pl.Buffered(1)
    spec = pltpu.PrefetchScalarGridSpec(
        num_scalar_prefetch=2,
        grid=(n_blocks,),
        in_specs=[
            pl.BlockSpec((EXPERT_ROWS, D_MODEL), lambda b, be, bv: (b, 0)),
            pl.BlockSpec((None, None, D_MODEL, EXPERT_FF), lambda b, be, bv: (j, be[b], 0, 0), pipeline_mode=resident),
            pl.BlockSpec((None, None, D_MODEL, EXPERT_FF), lambda b, be, bv: (j, be[b], 0, 0), pipeline_mode=resident),
            pl.BlockSpec((None, None, EXPERT_FF, D_MODEL), lambda b, be, bv: (j, be[b], 0, 0), pipeline_mode=resident),
        ],
        out_specs=pl.BlockSpec((EXPERT_ROWS, D_MODEL), lambda b, be, bv: (b, 0)),
        scratch_shapes=[pltpu.VMEM((EXPERT_ROWS, EXPERT_FF), BF16)],
    )
    return pl.pallas_call(
        _expert_kernel,
        grid_spec=spec,
        out_shape=jax.ShapeDtypeStruct(xs.shape, BF16),
        compiler_params=_params(("arbitrary",)),
        name="moe_experts",
    )(block_e, block_used, xs, w1, w3, w2)


COMBINE_ROWS = ROUTE_TILE
DW_LANES = 4


def _combine_kernel(r_ref, lo_ref, hi_ref, es_ref, xr_ref, xl_ref, xh_ref, dw_ref, x1_ref, gm_ref, lg_ref, lb_ref,
                    y_hbm, o_ref, buf, sem, xbuf, xsem, acc_ref, d_ref, w_ref):
    s = pl.program_id(0)
    half = s & 1
    for k in range(2):
        d_ref[k] = jnp.broadcast_to(dw_ref[:, k:k + 1].astype(jnp.int32), d_ref.shape[1:])
        w_ref[k] = jnp.broadcast_to(dw_ref[:, 2 + k:3 + k], w_ref.shape[1:])

    def win_copy(tile, e, which):
        start = pl.multiple_of(r_ref[tile * N_EXPERTS + e], 16)
        return pltpu.make_async_copy(y_hbm.at[pl.ds(start, COMBINE_ROWS)], buf.at[which, e], sem.at[which, e])

    @pl.when(s == 0)
    def _():
        for e in range(N_EXPERTS):
            win_copy(0, e, 0).start()

    @pl.when(s + 1 < pl.num_programs(0))
    def _():
        for e in range(N_EXPERTS):
            win_copy(s + 1, e, 1 - half).start()

    def weighted(first, lo, hi, rows):
        d1 = d_ref[0]
        d2 = d_ref[1]
        in1 = (d1 >= lo) & (d1 < hi)
        in2 = (d2 >= lo) & (d2 < hi)
        wsel = jnp.where(in1, w_ref[0], 0.0) + jnp.where(in2, w_ref[1], 0.0)
        hit = jnp.where(in1, d1, jnp.where(in2, d2, -1)) - first
        hit = jnp.concatenate([hit] * (COMBINE_ROWS // 128), axis=1)
        col = lax.broadcasted_iota(jnp.int32, (ROUTE_TILE, COMBINE_ROWS), 1)
        picked = _dot(jnp.where(hit == col, 1.0, 0.0).astype(BF16), rows)
        return jnp.concatenate([wsel] * (D_MODEL // 128), axis=1) * picked

    for e in range(N_EXPERTS):
        win_copy(s, e, half).wait()
    acc = jnp.zeros((ROUTE_TILE, D_MODEL), F32)
    for e in range(N_EXPERTS):
        i = s * N_EXPERTS + e
        first = r_ref[i]
        acc = acc + weighted(first, lo_ref[i], jnp.minimum(hi_ref[i], first + COMBINE_ROWS), buf[half, e])
    acc_ref[...] = acc

    def overflow(j, carry):
        start = pl.multiple_of(xr_ref[j], 16)
        copy = pltpu.make_async_copy(y_hbm.at[pl.ds(start, COMBINE_ROWS)], xbuf, xsem.at[0])
        copy.start()
        copy.wait()
        acc_ref[...] = acc_ref[...] + weighted(xr_ref[j], xl_ref[j], xh_ref[j], xbuf[...])
        return carry

    lax.fori_loop(es_ref[s], es_ref[s + 1], overflow, 0)
    o_ref[...] = _layer_norm(DEEPNORM_ALPHA * x1_ref[...] + gm_ref[...] * acc_ref[...], lg_ref[...], lb_ref[...])


def _combine(lists, y, dw, x1, mod3, ln_g, ln_b, layer, n_lat, seq, batch):
    n = x1.shape[0]
    nl = n_lat // ROUTE_TILE
    tpb = seq // ROUTE_TILE

    def row(s):
        return jnp.where(s < nl, s // tpb, batch)

    spec = pltpu.PrefetchScalarGridSpec(
        num_scalar_prefetch=7,
        grid=(n // ROUTE_TILE,),
        in_specs=[
            pl.BlockSpec((ROUTE_TILE, DW_LANES), lambda s, *_: (s, 0)),
            pl.BlockSpec((ROUTE_TILE, D_MODEL), lambda s, *_: (s, 0)),
            pl.BlockSpec((None, 1, D_MODEL), lambda s, *_: (row(s), 0, 5)),
            pl.BlockSpec((None, 1, D_MODEL), lambda s, *_: (layer, 0, 0)),
            pl.BlockSpec((None, 1, D_MODEL), lambda s, *_: (layer, 0, 0)),
            pl.BlockSpec(memory_space=pl.ANY),
        ],
        out_specs=pl.BlockSpec((ROUTE_TILE, D_MODEL), lambda s, *_: (s, 0)),
        scratch_shapes=[
            pltpu.VMEM((2, N_EXPERTS, COMBINE_ROWS, D_MODEL), BF16),
            pltpu.SemaphoreType.DMA((2, N_EXPERTS)),
            pltpu.VMEM((COMBINE_ROWS, D_MODEL), BF16),
            pltpu.SemaphoreType.DMA((1,)),
            pltpu.VMEM((ROUTE_TILE, D_MODEL), F32),
            pltpu.VMEM((2, ROUTE_TILE, 128), jnp.int32),
            pltpu.VMEM((2, ROUTE_TILE, 128), F32),
        ],
    )
    return pl.pallas_call(
        _combine_kernel,
        grid_spec=spec,
        out_shape=jax.ShapeDtypeStruct((n, D_MODEL), F32),
        compiler_params=_params(("arbitrary",)),
        name="moe_combine",
    )(*lists, dw, x1, mod3, ln_g, ln_b, y)


def _moe(t, x1, mod3, wr, tri, w1, w3, w2, ln_g, ln_b, layer, j, n_lat, seq, batch):
    n = t.shape[0]
    n_tiles = n // ROUTE_TILE
    n_blocks = -(-(2 * n + N_EXPERTS * (EXPERT_ROWS - 1)) // EXPERT_ROWS) + 1
    i32 = jnp.int32

    route, before, total = _router(t, wr, tri)

    e12 = route[:, 0:2].astype(i32)
    rank = route[:, 4:6].astype(i32)
    counts = total[0, :N_EXPERTS].astype(i32)
    padded = (counts + EXPERT_ROWS - 1) // EXPERT_ROWS * EXPERT_ROWS
    pad_end = jnp.cumsum(padded)
    pad_start = pad_end - padded
    dest = pad_start[e12] + rank
    dst = dest.reshape(n_tiles, ROUTE_TILE, 2).transpose(0, 2, 1)
    dw = jnp.concatenate([dest.astype(F32), route[:, 2:4]], axis=1)
    block_start = jnp.arange(n_blocks, dtype=i32) * EXPERT_ROWS
    block_e = jnp.minimum(jnp.sum(block_start[:, None] >= pad_end[None, :], axis=1), N_EXPERTS - 1).astype(i32)
    block_used = (block_start < pad_end[-1]).astype(i32)

    cb = before[:, 0, :N_EXPERTS].astype(i32)
    ca = jnp.concatenate([cb[1:], counts[None, :]], axis=0)
    lo = pad_start[None, :] + cb
    hi = pad_start[None, :] + ca
    some = hi > lo
    tile_id = jnp.arange(n_tiles, dtype=i32)[:, None, None]

    def grouped(ok, group, n_groups, *values):
        key = jnp.where(ok, group, n_groups).reshape(-1)
        order = jnp.argsort(key, stable=True)
        starts = jnp.sum(key[None, :] < jnp.arange(n_groups + 1, dtype=i32)[:, None], axis=1, dtype=i32)
        return (starts,) + tuple(jnp.broadcast_to(v, ok.shape).reshape(-1)[order].astype(i32) for v in values)

    last_off = EXPERT_ROWS - GATHER_ROWS
    b_lo = lo // EXPERT_ROWS
    parts = []
    for blk, p_lo, p_hi in ((b_lo, lo, jnp.minimum(hi, (b_lo + 1) * EXPERT_ROWS)),
                            (b_lo + 1, (b_lo + 1) * EXPERT_ROWS, hi)):
        l_lo = p_lo - blk * EXPERT_ROWS
        l_hi = p_hi - blk * EXPERT_ROWS
        a0 = jnp.minimum(l_lo // 16 * 16, last_off)
        cut = a0 + GATHER_ROWS
        a1 = jnp.minimum(cut, last_off)
        parts.append((blk, a0, l_lo, jnp.minimum(l_hi, cut)))
        parts.append((blk, a1, jnp.maximum(l_lo, cut), l_hi))
    g_block, g_off, g_lo, g_hi = (jnp.stack(v, axis=-1) for v in zip(*parts))
    g_ok = some[:, :, None] & (g_hi > g_lo)
    g_start, g_tile, g_off, g_lo, g_hi = grouped(g_ok, g_block, n_blocks, tile_id, g_off, g_lo, g_hi)
    xs = _gather((g_start, g_tile | (g_off << 16), g_lo | (g_hi << 16)), t, dst, n_blocks)

    y = _experts(block_e, block_used, xs, w1, w3, w2, j)

    r0 = lo // 16 * 16
    over = some & (hi > r0 + COMBINE_ROWS)
    x_start, x_row, x_lo, x_hi = grouped(over, tile_id[:, :, 0], n_tiles, r0 + COMBINE_ROWS, r0 + COMBINE_ROWS, hi)
    lists = (r0.reshape(-1), lo.reshape(-1), hi.reshape(-1), x_start, x_row, x_lo, x_hi)
    return _combine(lists, y, dw, x1, mod3, ln_g, ln_b, layer, n_lat, seq, batch)


def _window_counts(n, w):
    t = np.arange(n)
    left = w // 2
    right = w - 1 - left
    return (np.minimum(t + right + 1, n) - np.maximum(t - left, 0)).astype(np.float32)


def _window_matrix(n, w):
    left = w // 2
    right = w - 1 - left
    t = np.arange(n)
    return ((t[None, :] >= t[:, None] - left) & (t[None, :] <= t[:, None] + right)).astype(np.float32)


def _pool_tables(seq, n_ctx):
    rows = seq // GRID_W
    per_tile = 256 // GRID_W
    mc = np.stack([np.kron(np.eye(per_tile, dtype=np.float32), _window_matrix(GRID_W, w)) for w in POOL_WINDOWS])
    inv = np.stack([1.0 / np.outer(_window_counts(rows, w), _window_counts(GRID_W, w)).reshape(seq)
                    for w in POOL_WINDOWS])
    inv = np.broadcast_to(inv[:, :, None], (4, seq, POOL_GROUP)).astype(np.float32)
    m1d = np.stack([_window_matrix(n_ctx, w) for w in POOL_WINDOWS])
    inv1d = np.stack([1.0 / _window_counts(n_ctx, w) for w in POOL_WINDOWS])
    inv1d = np.broadcast_to(inv1d[:, :, None], (4, n_ctx, POOL_GROUP)).astype(np.float32)
    return (jnp.asarray(mc, BF16), jnp.asarray(inv), jnp.asarray(m1d, BF16), jnp.asarray(inv1d))


def _rope_tables(seq):
    t = jnp.arange(seq)
    row = (t // GRID_W).astype(F32)
    col = (t % GRID_W).astype(F32)
    n_freq = RET_QK_DIM // 4
    inv = jnp.exp(-jnp.log(ROPE_BASE) * jnp.arange(n_freq, dtype=F32) / n_freq)
    ang = jnp.concatenate([row[:, None] * inv, col[:, None] * inv], -1)
    cos, sin = jnp.cos(ang), jnp.sin(ang)
    cos2 = jnp.concatenate([cos, cos], -1)
    sin2 = jnp.concatenate([-sin, sin], -1)
    cos2 = jnp.concatenate([cos2, jnp.ones((TOKEN_TILE, RET_QK_DIM), F32)], 0)
    sin2 = jnp.concatenate([sin2, jnp.zeros((TOKEN_TILE, RET_QK_DIM), F32)], 0)
    return cos2, sin2


def kernel(x, c, ctx, c_ctx, ada_w, ada_b, w_in, pool_w, pool_scale, w_pool_out, w_ret_out, ret_decay_logit,
           w_out, ln_mix_g, ln_mix_b, ln_ffn_g, ln_ffn_b, ffn_w1, ffn_w3, ffn_w2, moe_router, moe_w1, moe_w3,
           moe_w2):
    batch, seq, d = x.shape
    n_ctx = ctx.shape[1]
    n_lat = batch * seq
    depth = ada_w.shape[0]
    assert d == D_MODEL and depth == DEPTH and batch < MOD_ROWS
    assert seq % TOKEN_TILE == 0 and (batch * n_ctx) % TOKEN_TILE == 0 and n_lat % n_ctx == 0
    assert TOKEN_TILE % ROUTER_STEP == 0 and ROUTER_STEP % ROUTE_TILE == 0
    assert seq % FFN_ROWS == 0 and (batch * n_ctx) % FFN_ROWS == 0

    s_in = jnp.zeros((MOD_ROWS, d), F32).at[:batch].set(c).at[batch].set(c_ctx)
    mod = _modulation(s_in, ada_w, ada_b)
    cos2, sin2 = _rope_tables(seq)
    mc, inv, m1d, inv1d = _pool_tables(seq, n_ctx)
    tri = jnp.asarray(np.tril(np.ones((ROUTER_STEP, ROUTER_STEP), np.float32), -1), BF16)

    w_in_b = w_in.astype(BF16)
    pool_w_b = pool_w.astype(BF16)
    pool_s = pool_scale.reshape(depth, 1, POOL_WIDTH)
    wp_b, wr_b, wo_b = w_pool_out.astype(BF16), w_ret_out.astype(BF16), w_out.astype(BF16)
    ffn_b = ffn_w1.astype(BF16), ffn_w3.astype(BF16), ffn_w2.astype(BF16)
    moe_b = moe_w1.astype(BF16), moe_w3.astype(BF16), moe_w2.astype(BF16)
    ln_mix = ln_mix_g.reshape(depth, 1, d), ln_mix_b.reshape(depth, 1, d)
    ln_ffn = ln_ffn_g.reshape(depth, 1, d), ln_ffn_b.reshape(depth, 1, d)

    n_tok = n_lat + batch * n_ctx
    x_parts = (x.reshape(n_lat, d), ctx.reshape(batch * n_ctx, d))
    for l in range(depth):
        last = l == depth - 1
        mod3 = mod[l].reshape(MOD_ROWS, 1, 6 * d)
        pa, pb = _in_proj(x_parts, mod3, cos2, sin2, w_in_b, l, n_tok, n_lat, seq, batch)
        yp_lat = _pool_lat(pa, mc, inv, pool_w_b, pool_s, l, batch, seq)
        yp_ctx = _pool_ctx(pa, m1d, inv1d, pool_w_b, pool_s, l, batch, n_lat, n_ctx)
        dl = jnp.broadcast_to(ret_decay_logit[l].astype(F32)[:, :, None, None],
                              (2, RET_HEADS, RET_CHUNK, RET_V_DIM))
        z_lat, z_ctx = _retention(pa, pb, dl, batch, seq, n_ctx)
        n_out = n_lat if last else n_tok
        x1, t = _merge(x_parts, mod3, yp_lat, yp_ctx, z_lat, z_ctx, pb, wp_b, wr_b, wo_b,
                       *ln_mix, l, n_out, n_lat, seq, batch)
        j = l // 2
        if l % 2 == 0:
            xs = _ffn(t, x1, mod3, *ffn_b, *ln_ffn, l, j, n_lat, seq, batch)
        else:
            wr = jnp.zeros((d, ROUTE_LANES), BF16).at[:, :N_EXPERTS].set(moe_router[j].astype(BF16))
            xs = _moe(t, x1, mod3, wr, tri, *moe_b, *ln_ffn, l, j, n_lat, seq, batch)
        x_parts = (xs,)
    return xs[:n_lat].reshape(batch, seq, d)
```

```python
import functools

import jax
import jax.numpy as jnp
import numpy as np
from jax import lax
from jax.experimental import pallas as pl
from jax.experimental.pallas import tpu as pltpu

F32 = jnp.float32
BF16 = jnp.bfloat16

D_MODEL = 1024
DEPTH = 4
GRID_W = 64
POOL_WINDOWS = (2, 4, 8, 16)
POOL_GROUP = 128
POOL_WIDTH = POOL_GROUP * len(POOL_WINDOWS)
RET_HEADS = 4
RET_QK_DIM = 128
RET_V_DIM = 256
RET_CHUNK = 128
ROPE_BASE = 10000.0
OFF_Q = POOL_WIDTH
OFF_K = OFF_Q + RET_HEADS * RET_QK_DIM
OFF_V = OFF_K + RET_HEADS * RET_QK_DIM
OFF_G = OFF_V + RET_HEADS * RET_V_DIM
WIDTH_B = 3 * D_MODEL
D_FF = 2816
N_EXPERTS = 8
EXPERT_FF = 3584
DEEPNORM_ALPHA = (2 * DEPTH) ** 0.25
LN_EPS = 1e-5
K_SCALE = RET_QK_DIM ** -0.5

MOD_ROWS = 24
TOKEN_TILE = 512
ROUTE_TILE = 256
ROUTER_STEP = 512
EXPERT_ROWS = 512
FFN_ROWS = 2 * TOKEN_TILE
VMEM_LIMIT = 56 * 1024 * 1024


def _dot(a, b):
    return jnp.dot(a, b, preferred_element_type=F32)


def _sigmoid(x):
    return 1.0 / (1.0 + jnp.exp(-x))


def _split_bf16(a):
    hi = a.astype(BF16)
    lo = (a - hi.astype(F32)).astype(BF16)
    return hi, lo


def _layer_norm(v, g, b):
    mean = jnp.mean(v, axis=-1, keepdims=True)
    vc = v - mean
    var = jnp.mean(vc * vc, axis=-1, keepdims=True)
    return vc * lax.rsqrt(var + LN_EPS) * g + b


def _params(sem, vmem=VMEM_LIMIT):
    return pltpu.CompilerParams(dimension_semantics=sem, vmem_limit_bytes=vmem)


def _mod_kernel(s_ref, w_ref, b_ref, o_ref):
    s = s_ref[...]
    s = s * _sigmoid(s)
    s_hi, s_lo = _split_bf16(s)
    w_hi, w_lo = _split_bf16(w_ref[...])
    o_ref[...] = _dot(s_hi, w_hi) + (_dot(s_hi, w_lo) + _dot(s_lo, w_hi)) + b_ref[...]


def _modulation(s_in, ada_w, ada_b):
    depth, d, width = ada_w.shape
    tn = 1536
    return pl.pallas_call(
        _mod_kernel,
        grid=(depth, width // tn),
        in_specs=[
            pl.BlockSpec((MOD_ROWS, d), lambda l, j: (0, 0)),
            pl.BlockSpec((None, d, tn), lambda l, j: (l, 0, j)),
            pl.BlockSpec((None, 1, tn), lambda l, j: (l, 0, j)),
        ],
        out_specs=pl.BlockSpec((None, MOD_ROWS, tn), lambda l, j: (l, 0, j)),
        out_shape=jax.ShapeDtypeStruct((depth, MOD_ROWS, width), F32),
        compiler_params=_params(("arbitrary", "arbitrary")),
        name="adaln_mod",
    )(s_in, ada_w, ada_b.reshape(depth, 1, width))


IN_COLS = 512
IN_BLOCKS = (OFF_G + WIDTH_B) // IN_COLS


def _pick_x(x_refs, rows, is_lat):
    if len(x_refs) == 1:
        return x_refs[0][rows, :]
    return jnp.where(is_lat, x_refs[0][rows, :], x_refs[1][rows, :])


def _in_kernel(*refs, n_lat_tiles, n_x):
    x_refs = refs[:n_x]
    sh_ref, sc_ref, cos_ref, sin_ref = refs[n_x:n_x + 4]
    w_refs = refs[n_x + 4:n_x + 4 + IN_BLOCKS]
    pa_ref, pb_ref = refs[n_x + 4 + IN_BLOCKS:]
    x = _pick_x(x_refs, slice(None), pl.program_id(0) < n_lat_tiles)
    h = (x * (1.0 + sc_ref[...]) + sh_ref[...]).astype(BF16)
    cos = cos_ref[...]
    sin = sin_ref[...]
    for blk in range(IN_BLOCKS):
        c0 = blk * IN_COLS
        acc = _dot(h, w_refs[blk][...])
        if c0 in (OFF_Q, OFF_K):
            for hh in range(RET_HEADS):
                t = acc[:, hh * RET_QK_DIM:(hh + 1) * RET_QK_DIM]
                r = t * cos + pltpu.roll(t, RET_QK_DIM // 2, 1) * sin
                if c0 == OFF_K:
                    r = r * K_SCALE
                pa_ref[:, c0 + hh * RET_QK_DIM:c0 + (hh + 1) * RET_QK_DIM] = r.astype(BF16)
        elif c0 < OFF_G:
            pa_ref[:, c0:c0 + IN_COLS] = acc.astype(BF16)
        else:
            pb_ref[:, c0 - OFF_G:c0 - OFF_G + IN_COLS] = acc.astype(BF16)


def _tile_maps(n_lat, seq, batch, tile):
    nl = n_lat // tile
    tpb = seq // tile

    def row(i):
        return jnp.where(i < nl, i // tpb, batch)

    def lat(i):
        return jnp.minimum(i, nl - 1)

    def ctx(i):
        return jnp.maximum(i - nl, 0)

    return nl, tpb, row, lat, ctx


def _x_operands(x_parts, n_lat, seq, batch, tile):
    _, _, _, lat, ctx = _tile_maps(n_lat, seq, batch, tile)
    if len(x_parts) == 1:
        return [pl.BlockSpec((tile, D_MODEL), lambda i: (i, 0))], list(x_parts)
    return ([pl.BlockSpec((tile, D_MODEL), lambda i: (lat(i), 0)),
             pl.BlockSpec((tile, D_MODEL), lambda i: (ctx(i), 0))], list(x_parts))


def _in_proj(x_parts, mod3, cos2, sin2, w_in, layer, n_tok, n_lat, seq, batch):
    tm = TOKEN_TILE
    nl, tpb, row, _, _ = _tile_maps(n_lat, seq, batch, tm)
    x_specs, x_args = _x_operands(x_parts, n_lat, seq, batch, tm)

    def rope(i):
        return jnp.where(i < nl, i % tpb, tpb)

    def w_spec(blk):
        return pl.BlockSpec((None, D_MODEL, IN_COLS), lambda i: (layer, 0, blk))

    return pl.pallas_call(
        functools.partial(_in_kernel, n_lat_tiles=nl, n_x=len(x_args)),
        grid=(n_tok // tm,),
        in_specs=x_specs + [
            pl.BlockSpec((None, 1, D_MODEL), lambda i: (row(i), 0, 0)),
            pl.BlockSpec((None, 1, D_MODEL), lambda i: (row(i), 0, 1)),
            pl.BlockSpec((tm, RET_QK_DIM), lambda i: (rope(i), 0)),
            pl.BlockSpec((tm, RET_QK_DIM), lambda i: (rope(i), 0)),
        ] + [w_spec(blk) for blk in range(IN_BLOCKS)],
        out_specs=[
            pl.BlockSpec((tm, OFF_G), lambda i: (i, 0)),
            pl.BlockSpec((tm, WIDTH_B), lambda i: (i, 0)),
        ],
        out_shape=[
            jax.ShapeDtypeStruct((n_tok, OFF_G), BF16),
            jax.ShapeDtypeStruct((n_tok, WIDTH_B), BF16),
        ],
        compiler_params=_params(("arbitrary",)),
        name="in_proj",
    )(*x_args, mod3, mod3, cos2, sin2, *([w_in] * IN_BLOCKS))


POOL_PAD_ROWS = 8
POOL_ROW_CHUNK = 8


def _pool_kernel(u_ref, mc_ref, inv_ref, pw_ref, ps_ref, o_ref, zp_ref, *, rows):
    seq = rows * GRID_W
    pad = POOL_PAD_ROWS * GRID_W
    chunk = POOL_ROW_CHUNK * GRID_W
    zeros = jnp.zeros((pad, POOL_GROUP), F32)
    for g, w in enumerate(POOL_WINDOWS):
        lanes = slice(g * POOL_GROUP, (g + 1) * POOL_GROUP)
        zp_ref[0:pad, :] = zeros
        zp_ref[pad + seq:pad + seq + pad, :] = zeros
        for c in range(seq // 256):
            zp_ref[pad + c * 256:pad + (c + 1) * 256, :] = _dot(mc_ref[g], u_ref[c * 256:(c + 1) * 256, lanes])
        left = w // 2

        def body(rc, carry, g=g, w=w, left=left, lanes=lanes):
            tok = pl.multiple_of(rc * chunk, chunk)
            acc = zp_ref[pl.ds(tok + pad - left * GRID_W, chunk), :]
            for k in range(1, w):
                acc = acc + zp_ref[pl.ds(tok + pad + (k - left) * GRID_W, chunk), :]
            pooled = acc * inv_ref[g, pl.ds(tok, chunk), :]
            d = (pooled - u_ref[pl.ds(tok, chunk), lanes].astype(F32)).astype(BF16)
            y = _dot(d, pw_ref[g]) * ps_ref[:, lanes]
            o_ref[pl.ds(tok, chunk), lanes] = y.astype(BF16)
            return carry

        lax.fori_loop(0, rows // POOL_ROW_CHUNK, body, 0, unroll=4)


def _pool_lat(pa, mc, inv, pw, ps, layer, batch, seq):
    rows = seq // GRID_W
    return pl.pallas_call(
        functools.partial(_pool_kernel, rows=rows),
        grid=(batch,),
        in_specs=[
            pl.BlockSpec((seq, POOL_WIDTH), lambda b: (b, 0)),
            pl.BlockSpec((4, 256, 256), lambda b: (0, 0, 0)),
            pl.BlockSpec((4, seq, POOL_GROUP), lambda b: (0, 0, 0)),
            pl.BlockSpec((None, 4, POOL_GROUP, POOL_GROUP), lambda b: (layer, 0, 0, 0)),
            pl.BlockSpec((None, 1, POOL_WIDTH), lambda b: (layer, 0, 0)),
        ],
        out_specs=pl.BlockSpec((seq, POOL_WIDTH), lambda b: (b, 0)),
        out_shape=jax.ShapeDtypeStruct((batch * seq, POOL_WIDTH), BF16),
        scratch_shapes=[pltpu.VMEM(((rows + 2 * POOL_PAD_ROWS) * GRID_W, POOL_GROUP), F32)],
        compiler_params=_params(("arbitrary",)),
        name="pool_lat",
    )(pa, mc, inv, pw, ps)


def _pool_ctx_kernel(u_ref, m_ref, inv_ref, pw_ref, ps_ref, o_ref):
    for g in range(len(POOL_WINDOWS)):
        lanes = slice(g * POOL_GROUP, (g + 1) * POOL_GROUP)
        ug = u_ref[:, lanes]
        pooled = _dot(m_ref[g], ug) * inv_ref[g]
        d = (pooled - ug.astype(F32)).astype(BF16)
        o_ref[:, lanes] = (_dot(d, pw_ref[g]) * ps_ref[:, lanes]).astype(BF16)


def _pool_ctx(pa, m1d, inv1d, pw, ps, layer, batch, n_lat, n_ctx):
    first = n_lat // n_ctx
    return pl.pallas_call(
        _pool_ctx_kernel,
        grid=(batch,),
        in_specs=[
            pl.BlockSpec((n_ctx, POOL_WIDTH), lambda b: (first + b, 0)),
            pl.BlockSpec((4, n_ctx, n_ctx), lambda b: (0, 0, 0)),
            pl.BlockSpec((4, n_ctx, POOL_GROUP), lambda b: (0, 0, 0)),
            pl.BlockSpec((None, 4, POOL_GROUP, POOL_GROUP), lambda b: (layer, 0, 0, 0)),
            pl.BlockSpec((None, 1, POOL_WIDTH), lambda b: (layer, 0, 0)),
        ],
        out_specs=pl.BlockSpec((n_ctx, POOL_WIDTH), lambda b: (b, 0)),
        out_shape=jax.ShapeDtypeStruct((batch * n_ctx, POOL_WIDTH), BF16),
        compiler_params=_params(("arbitrary",)),
        name="pool_ctx",
    )(pa, m1d, inv1d, pw, ps)


def _log_sigmoid(x):
    return jnp.minimum(x, 0.0) - jnp.log1p(jnp.exp(-jnp.abs(x)))


def _decays(dl, backward):
    c = RET_CHUNK
    lg = _log_sigmoid(dl)
    lgq = lg[:, :RET_QK_DIM]
    ii = lax.broadcasted_iota(jnp.int32, (c, c), 0)
    jj = lax.broadcasted_iota(jnp.int32, (c, c), 1)
    pos = lax.broadcasted_iota(jnp.int32, (c, RET_QK_DIM), 0).astype(F32)
    if backward:
        diff = (jj - ii).astype(F32)
        qdec = jnp.exp(lgq * (c - pos))
        kdec = jnp.exp(lgq * pos)
    else:
        diff = (ii - jj).astype(F32)
        qdec = jnp.exp(lgq * (pos + 1.0))
        kdec = jnp.exp(lgq * (c - 1.0 - pos))
    inner = jnp.where(diff >= 0, jnp.exp(lgq * jnp.maximum(diff, 0.0)), 0.0)
    cdec = jnp.exp(lg * float(c))
    return inner, qdec, kdec, cdec


def _chunk_kv(kc, vc, kdec):
    kd = (kc.astype(F32) * kdec).astype(BF16)
    return lax.dot_general(kd, vc, (((0,), (0,)), ((), ())), preferred_element_type=F32)


def _chunk_out(qc, kc, vc, gc, states, mask, qdec2):
    scores = lax.dot_general(qc, kc, (((1,), (1,)), ((), ())), preferred_element_type=F32) * mask
    qf = qc.astype(F32)
    qd = (jnp.concatenate([qf, qf], axis=1) * qdec2).astype(BF16)
    o = _dot(scores.astype(BF16), vc) + _dot(qd, states)
    mean = jnp.mean(o, axis=-1, keepdims=True)
    oc = o - mean
    var = jnp.mean(oc * oc, axis=-1, keepdims=True)
    on = oc * lax.rsqrt(var + LN_EPS)
    gf = gc.astype(F32)
    return (gf * _sigmoid(gf) * on).astype(BF16)


def _ret_kernel(q_ref, k_ref, v_ref, g_ref, qc_ref, kc_ref, vc_ref, gc_ref, dl_ref,
                z_ref, zc_ref, st_ref, stc_ref, run_ref, *, n_chunks, n_ctx_chunks):
    c = RET_CHUNK
    dk = RET_QK_DIM
    inner_f, qdec_f, kdec_f, cdec_f = _decays(dl_ref[0], False)
    inner_b, qdec_b, kdec_b, cdec_b = _decays(dl_ref[1], True)
    mask = inner_f + inner_b
    qdec2 = jnp.concatenate([qdec_f, qdec_b], axis=1)

    def rows(j):
        return slice(j * c, (j + 1) * c)

    s = jnp.zeros((dk, RET_V_DIM), F32)
    for j in range(n_ctx_chunks):
        stc_ref[j, 0:dk, :] = s.astype(BF16)
        s = s * cdec_f + _chunk_kv(kc_ref[rows(j), :], vc_ref[rows(j), :], kdec_f)
    run_ref[0] = s
    s = jnp.zeros((dk, RET_V_DIM), F32)
    for j in reversed(range(n_ctx_chunks)):
        stc_ref[j, dk:2 * dk, :] = s.astype(BF16)
        s = s * cdec_b + _chunk_kv(kc_ref[rows(j), :], vc_ref[rows(j), :], kdec_b)
    run_ref[1] = s
    for j in range(n_ctx_chunks):
        zc_ref[rows(j), :] = _chunk_out(qc_ref[rows(j), :], kc_ref[rows(j), :], vc_ref[rows(j), :],
                                        gc_ref[rows(j), :], stc_ref[j], mask, qdec2)

    def scan(t, carry):
        rf = pl.ds(pl.multiple_of(t * c, c), c)
        sf = run_ref[0]
        st_ref[t, 0:dk, :] = sf.astype(BF16)
        run_ref[0] = sf * cdec_f + _chunk_kv(k_ref[rf, :], v_ref[rf, :], kdec_f)
        tb = n_chunks - 1 - t
        rb = pl.ds(pl.multiple_of(tb * c, c), c)
        sb = run_ref[1]
        st_ref[tb, dk:2 * dk, :] = sb.astype(BF16)
        run_ref[1] = sb * cdec_b + _chunk_kv(k_ref[rb, :], v_ref[rb, :], kdec_b)
        return carry

    lax.fori_loop(0, n_chunks, scan, 0, unroll=4)

    def emit(t, carry):
        r = pl.ds(pl.multiple_of(t * c, c), c)
        z_ref[r, :] = _chunk_out(q_ref[r, :], k_ref[r, :], v_ref[r, :], g_ref[r, :], st_ref[t], mask, qdec2)
        return carry

    lax.fori_loop(0, n_chunks, emit, 0, unroll=8)


def _retention(pa, pb, dl, batch, seq, n_ctx):
    n_lat = batch * seq
    first = n_lat // n_ctx
    qk0 = OFF_Q // RET_QK_DIM
    kk0 = OFF_K // RET_QK_DIM
    v0 = OFF_V // RET_V_DIM
    return pl.pallas_call(
        functools.partial(_ret_kernel, n_chunks=seq // RET_CHUNK, n_ctx_chunks=n_ctx // RET_CHUNK),
        grid=(batch, RET_HEADS),
        in_specs=[
            pl.BlockSpec((seq, RET_QK_DIM), lambda b, h: (b, qk0 + h)),
            pl.BlockSpec((seq, RET_QK_DIM), lambda b, h: (b, kk0 + h)),
            pl.BlockSpec((seq, RET_V_DIM), lambda b, h: (b, v0 + h)),
            pl.BlockSpec((seq, RET_V_DIM), lambda b, h: (b, h)),
            pl.BlockSpec((n_ctx, RET_QK_DIM), lambda b, h: (first + b, qk0 + h)),
            pl.BlockSpec((n_ctx, RET_QK_DIM), lambda b, h: (first + b, kk0 + h)),
            pl.BlockSpec((n_ctx, RET_V_DIM), lambda b, h: (first + b, v0 + h)),
            pl.BlockSpec((n_ctx, RET_V_DIM), lambda b, h: (first + b, h)),
            pl.BlockSpec((2, None, RET_CHUNK, RET_V_DIM), lambda b, h: (0, h, 0, 0)),
        ],
        out_specs=[
            pl.BlockSpec((seq, RET_V_DIM), lambda b, h: (b, h)),
            pl.BlockSpec((n_ctx, RET_V_DIM), lambda b, h: (b, h)),
        ],
        out_shape=[
            jax.ShapeDtypeStruct((n_lat, RET_HEADS * RET_V_DIM), BF16),
            jax.ShapeDtypeStruct((batch * n_ctx, RET_HEADS * RET_V_DIM), BF16),
        ],
        scratch_shapes=[
            pltpu.VMEM((seq // RET_CHUNK, 2 * RET_QK_DIM, RET_V_DIM), BF16),
            pltpu.VMEM((n_ctx // RET_CHUNK, 2 * RET_QK_DIM, RET_V_DIM), BF16),
            pltpu.VMEM((2, RET_QK_DIM, RET_V_DIM), F32),
        ],
        compiler_params=_params(("arbitrary", "arbitrary")),
        name="retention",
    )(pa, pa, pa, pb, pa, pa, pa, pb, dl)


MIX_COLS = 256


def _merge_kernel(*refs, n_lat_tiles, n_x):
    x_refs = refs[:n_x]
    (gm_ref, sh_ref, sc_ref, ypl_ref, ypc_ref, zl_ref, zc_ref, gp_ref, gr_ref,
     wp_ref, wr_ref, wo_ref, lg_ref, lb_ref, x1_ref, t_ref, mix_ref) = refs[n_x:]
    is_lat = pl.program_id(0) < n_lat_tiles
    yp = jnp.where(is_lat, ypl_ref[...], ypc_ref[...])
    z = jnp.where(is_lat, zl_ref[...], zc_ref[...])
    x = _pick_x(x_refs, slice(None), is_lat)
    for c0 in range(0, D_MODEL, MIX_COLS):
        cols = slice(c0, c0 + MIX_COLS)
        y_pool = _dot(yp, wp_ref[:, cols])
        y_ret = _dot(z, wr_ref[:, cols])
        mix = _sigmoid(gp_ref[:, cols].astype(F32)) * y_pool + _sigmoid(gr_ref[:, cols].astype(F32)) * y_ret
        mix_ref[:, cols] = mix.astype(BF16)
    y = _dot(mix_ref[...], wo_ref[...])
    x1 = _layer_norm(DEEPNORM_ALPHA * x + gm_ref[...] * y, lg_ref[...], lb_ref[...])
    x1_ref[...] = x1
    t_ref[...] = (x1 * (1.0 + sc_ref[...]) + sh_ref[...]).astype(BF16)


def _merge(x_parts, mod3, yp_lat, yp_ctx, z_lat, z_ctx, pb, wp, wr, wo, ln_g, ln_b, layer,
           n_out, n_lat, seq, batch):
    tm = TOKEN_TILE
    nl, _, row, lat, ctx = _tile_maps(n_lat, seq, batch, tm)
    x_specs, x_args = _x_operands(x_parts, n_lat, seq, batch, tm)

    return pl.pallas_call(
        functools.partial(_merge_kernel, n_lat_tiles=nl, n_x=len(x_args)),
        grid=(n_out // tm,),
        in_specs=x_specs + [
            pl.BlockSpec((None, 1, D_MODEL), lambda i: (row(i), 0, 2)),
            pl.BlockSpec((None, 1, D_MODEL), lambda i: (row(i), 0, 3)),
            pl.BlockSpec((None, 1, D_MODEL), lambda i: (row(i), 0, 4)),
            pl.BlockSpec((tm, POOL_WIDTH), lambda i: (lat(i), 0)),
            pl.BlockSpec((tm, POOL_WIDTH), lambda i: (ctx(i), 0)),
            pl.BlockSpec((tm, D_MODEL), lambda i: (lat(i), 0)),
            pl.BlockSpec((tm, D_MODEL), lambda i: (ctx(i), 0)),
            pl.BlockSpec((tm, D_MODEL), lambda i: (i, 1)),
            pl.BlockSpec((tm, D_MODEL), lambda i: (i, 2)),
            pl.BlockSpec((None, POOL_WIDTH, D_MODEL), lambda i: (layer, 0, 0)),
            pl.BlockSpec((None, D_MODEL, D_MODEL), lambda i: (layer, 0, 0)),
            pl.BlockSpec((None, D_MODEL, D_MODEL), lambda i: (layer, 0, 0)),
            pl.BlockSpec((None, 1, D_MODEL), lambda i: (layer, 0, 0)),
            pl.BlockSpec((None, 1, D_MODEL), lambda i: (layer, 0, 0)),
        ],
        out_specs=[
            pl.BlockSpec((tm, D_MODEL), lambda i: (i, 0)),
            pl.BlockSpec((tm, D_MODEL), lambda i: (i, 0)),
        ],
        out_shape=[
            jax.ShapeDtypeStruct((n_out, D_MODEL), F32),
            jax.ShapeDtypeStruct((n_out, D_MODEL), BF16),
        ],
        scratch_shapes=[pltpu.VMEM((tm, D_MODEL), BF16)],
        compiler_params=_params(("arbitrary",)),
        name="merge",
    )(*x_args, mod3, mod3, mod3, yp_lat, yp_ctx, z_lat, z_ctx, pb, pb, wp, wr, wo, ln_g, ln_b)


SWIGLU_COLS = 256


def _swiglu_hidden(t, w1_ref, w3_ref, h_ref):
    width = h_ref.shape[1]
    for c0 in range(0, width, SWIGLU_COLS):
        cols = slice(c0, min(c0 + SWIGLU_COLS, width))
        a = _dot(t, w1_ref[:, cols])
        h_ref[:, cols] = (a * _sigmoid(a) * _dot(t, w3_ref[:, cols])).astype(BF16)


def _ffn_kernel(t_ref, x1_ref, gm_ref, w1_ref, w3_ref, w2_ref, lg_ref, lb_ref, o_ref, h_ref):
    for r in range(FFN_ROWS // TOKEN_TILE):
        rows = slice(r * TOKEN_TILE, (r + 1) * TOKEN_TILE)
        _swiglu_hidden(t_ref[rows, :], w1_ref, w3_ref, h_ref.at[r])
        f = _dot(h_ref[r], w2_ref[...])
        o_ref[rows, :] = _layer_norm(DEEPNORM_ALPHA * x1_ref[rows, :] + gm_ref[...] * f, lg_ref[...], lb_ref[...])


def _ffn(t, x1, mod3, w1, w3, w2, ln_g, ln_b, layer, j, n_lat, seq, batch):
    n = t.shape[0]
    tm = FFN_ROWS
    _, _, row, _, _ = _tile_maps(n_lat, seq, batch, tm)
    resident = pl.Buffered(1)

    return pl.pallas_call(
        _ffn_kernel,
        grid=(n // tm,),
        in_specs=[
            pl.BlockSpec((tm, D_MODEL), lambda i: (i, 0)),
            pl.BlockSpec((tm, D_MODEL), lambda i: (i, 0)),
            pl.BlockSpec((None, 1, D_MODEL), lambda i: (row(i), 0, 5)),
            pl.BlockSpec((None, D_MODEL, D_FF), lambda i: (j, 0, 0), pipeline_mode=resident),
            pl.BlockSpec((None, D_MODEL, D_FF), lambda i: (j, 0, 0), pipeline_mode=resident),
            pl.BlockSpec((None, D_FF, D_MODEL), lambda i: (j, 0, 0), pipeline_mode=resident),
            pl.BlockSpec((None, 1, D_MODEL), lambda i: (layer, 0, 0)),
            pl.BlockSpec((None, 1, D_MODEL), lambda i: (layer, 0, 0)),
        ],
        out_specs=pl.BlockSpec((tm, D_MODEL), lambda i: (i, 0)),
        out_shape=jax.ShapeDtypeStruct((n, D_MODEL), F32),
        scratch_shapes=[pltpu.VMEM((tm // TOKEN_TILE, TOKEN_TILE, D_FF), BF16)],
        compiler_params=_params(("arbitrary",)),
        name="ffn_dense",
    )(t, x1, mod3, w1, w3, w2, ln_g, ln_b)


ROUTE_LANES = 128


def _router_kernel(t_ref, wr_ref, tri_ref, route_ref, before_ref, total_ref, run_ref):
    @pl.when(pl.program_id(0) == 0)
    def _():
        run_ref[...] = jnp.zeros_like(run_ref)

    tt = t_ref.shape[0]
    logits = _dot(t_ref[...], wr_ref[...])
    lane = lax.broadcasted_iota(jnp.int32, (tt, ROUTE_LANES), 1)
    neg = jnp.float32(-jnp.inf)
    lg = jnp.where(lane < N_EXPERTS, logits, neg)
    m1 = jnp.max(lg, axis=1, keepdims=True)
    i1 = jnp.min(jnp.where(lg == m1, lane, ROUTE_LANES), axis=1, keepdims=True)
    lg2 = jnp.where(lane == i1, neg, lg)
    m2 = jnp.max(lg2, axis=1, keepdims=True)
    i2 = jnp.min(jnp.where(lg2 == m2, lane, ROUTE_LANES), axis=1, keepdims=True)
    e = jnp.exp(m2 - m1)
    w1 = 1.0 / (1.0 + e)
    w2 = e / (1.0 + e)
    hit1 = lane == i1
    hit2 = lane == i2
    onehot = jnp.where(hit1 | hit2, 1.0, 0.0)
    run = run_ref[...]
    prefix = _dot(tri_ref[...], onehot.astype(BF16)) + run
    r1 = jnp.sum(jnp.where(hit1, prefix, 0.0), axis=1, keepdims=True)
    r2 = jnp.sum(jnp.where(hit2, prefix, 0.0), axis=1, keepdims=True)
    for k in range(tt // ROUTE_TILE):
        before_ref[k] = prefix[k * ROUTE_TILE:k * ROUTE_TILE + 1, :]
    run = run + jnp.sum(onehot, axis=0, keepdims=True)
    run_ref[...] = run
    total_ref[...] = run
    out = jnp.where(lane == 0, i1.astype(F32), 0.0)
    out = jnp.where(lane == 1, i2.astype(F32), out)
    out = jnp.where(lane == 2, w1, out)
    out = jnp.where(lane == 3, w2, out)
    out = jnp.where(lane == 4, r1, out)
    out = jnp.where(lane == 5, r2, out)
    route_ref[...] = out


def _router(t, wr, tri):
    n = t.shape[0]
    tt = ROUTER_STEP
    sub = tt // ROUTE_TILE
    return pl.pallas_call(
        _router_kernel,
        grid=(n // tt,),
        in_specs=[
            pl.BlockSpec((tt, D_MODEL), lambda i: (i, 0)),
            pl.BlockSpec((D_MODEL, ROUTE_LANES), lambda i: (0, 0)),
            pl.BlockSpec((tt, tt), lambda i: (0, 0)),
        ],
        out_specs=[
            pl.BlockSpec((tt, ROUTE_LANES), lambda i: (i, 0)),
            pl.BlockSpec((sub, 1, ROUTE_LANES), lambda i: (i, 0, 0)),
            pl.BlockSpec((1, ROUTE_LANES), lambda i: (0, 0)),
        ],
        out_shape=[
            jax.ShapeDtypeStruct((n, ROUTE_LANES), F32),
            jax.ShapeDtypeStruct((n // ROUTE_TILE, 1, ROUTE_LANES), F32),
            jax.ShapeDtypeStruct((1, ROUTE_LANES), F32),
        ],
        scratch_shapes=[pltpu.VMEM((1, ROUTE_LANES), F32)],
        compiler_params=_params(("arbitrary",)),
        name="moe_router",
    )(t, wr, tri)


DMA_RING = 16
DMA_AHEAD = 8
GATHER_ROWS = 144
GATHER_SPLIT = 4


def _gather_kernel(es_ref, sa_ref, rr_ref, dst_ref, t_hbm, o_ref, buf, sem):
    b = pl.program_id(0)
    total = es_ref[pl.num_programs(0)]
    o_ref[...] = jnp.zeros_like(o_ref)

    def tile_copy(j):
        slot = j & (DMA_RING - 1)
        start = pl.multiple_of((sa_ref[j] & 0xFFFF) * ROUTE_TILE, ROUTE_TILE)
        return pltpu.make_async_copy(t_hbm.at[pl.ds(start, ROUTE_TILE)], buf.at[slot], sem.at[slot])

    @pl.when(b == 0)
    def _():
        for i in range(DMA_AHEAD):
            @pl.when(i < total)
            def _(i=i):
                tile_copy(i).start()

    def entry(j, slot, owned):
        tile = sa_ref[j] & 0xFFFF
        a = pl.multiple_of(lax.shift_right_logical(sa_ref[j], 16), 16)
        lo = rr_ref[j] & 0xFFFF
        hi = jnp.where(owned, lax.shift_right_logical(rr_ref[j], 16), 0)
        local = a + lax.broadcasted_iota(jnp.int32, (GATHER_ROWS, ROUTE_TILE), 0)
        rows = jnp.where((local >= lo) & (local < hi), local + b * EXPERT_ROWS, -1)
        d = dst_ref[tile]
        sel = (d[0:1, :] == rows) | (d[1:2, :] == rows)
        win = pl.ds(a, GATHER_ROWS)
        picked = _dot(jnp.where(sel, 1.0, 0.0).astype(BF16), buf[slot])
        o_ref[win, :] = o_ref[win, :] + picked.astype(BF16)

    j0 = es_ref[b]
    j1 = es_ref[b + 1]

    def body(k, carry):
        ja = j0 + 2 * k
        jb = ja + 1
        has_b = jb < j1
        tile_copy(ja).wait()

        @pl.when(ja + DMA_AHEAD < total)
        def _():
            tile_copy(ja + DMA_AHEAD).start()

        @pl.when(has_b)
        def _():
            tile_copy(jb).wait()

        @pl.when(has_b & (jb + DMA_AHEAD < total))
        def _():
            tile_copy(jb + DMA_AHEAD).start()

        entry(ja, ja & (DMA_RING - 1), True)
        entry(jb, jnp.where(has_b, jb, ja) & (DMA_RING - 1), has_b)
        return carry

    lax.fori_loop(0, (j1 - j0 + 1) // 2, body, 0)


def _gather(lists, t, dst, n_blocks):
    n_tiles = dst.shape[0]
    spec = pltpu.PrefetchScalarGridSpec(
        num_scalar_prefetch=3,
        grid=(n_blocks,),
        in_specs=[
            pl.BlockSpec((n_tiles, 2, ROUTE_TILE), lambda b, es, sa, rr: (0, 0, 0)),
            pl.BlockSpec(memory_space=pl.ANY),
        ],
        out_specs=pl.BlockSpec((EXPERT_ROWS, D_MODEL), lambda b, es, sa, rr: (b, 0)),
        scratch_shapes=[
            pltpu.VMEM((DMA_RING, ROUTE_TILE, D_MODEL), BF16),
            pltpu.SemaphoreType.DMA((DMA_RING,)),
        ],
    )
    return pl.pallas_call(
        _gather_kernel,
        grid_spec=spec,
        out_shape=jax.ShapeDtypeStruct((n_blocks * EXPERT_ROWS, D_MODEL), BF16),
        compiler_params=_params(("arbitrary",)),
        name="moe_gather",
    )(*lists, dst, t)


def _expert_kernel(be_ref, bv_ref, x_ref, w1_ref, w3_ref, w2_ref, o_ref, h_ref):
    used = bv_ref[pl.program_id(0)] == 1

    @pl.when(used)
    def _():
        _swiglu_hidden(x_ref[...], w1_ref, w3_ref, h_ref)
        o_ref[...] = _dot(h_ref[...], w2_ref[...]).astype(BF16)

    @pl.when(jnp.logical_not(used))
    def _():
        o_ref[...] = jnp.zeros_like(o_ref)


def _experts(block_e, block_used, xs, w1, w3, w2, j):
    n_blocks = block_e.shape[0]
    resident = pl.Buffered(1)
    spec = pltpu.PrefetchScalarGridSpec(
        num_scalar_prefetch=2,
        grid=(n_blocks,),
        in_specs=[
            pl.BlockSpec((EXPERT_ROWS, D_MODEL), lambda b, be, bv: (b, 0)),
            pl.BlockSpec((None, None, D_MODEL, EXPERT_FF), lambda b, be, bv: (j, be[b], 0, 0), pipeline_mode=resident),
            pl.BlockSpec((None, None, D_MODEL, EXPERT_FF), lambda b, be, bv: (j, be[b], 0, 0), pipeline_mode=resident),
            pl.BlockSpec((None, None, EXPERT_FF, D_MODEL), lambda b, be, bv: (j, be[b], 0, 0), pipeline_mode=resident),
        ],
        out_specs=pl.BlockSpec((EXPERT_ROWS, D_MODEL), lambda b, be, bv: (b, 0)),
        scratch_shapes=[pltpu.VMEM((EXPERT_ROWS, EXPERT_FF), BF16)],
    )
    return pl.pallas_call(
        _expert_kernel,
        grid_spec=spec,
        out_shape=jax.ShapeDtypeStruct(xs.shape, BF16),
        compiler_params=_params(("arbitrary",)),
        name="moe_experts",
    )(block_e, block_used, xs, w1, w3, w2)


COMBINE_ROWS = ROUTE_TILE
DW_LANES = 4


def _combine_kernel(r_ref, lo_ref, hi_ref, es_ref, xr_ref, xl_ref, xh_ref, dw_ref, x1_ref, gm_ref, lg_ref, lb_ref,
                    y_hbm, o_ref, buf, sem, xbuf, xsem, acc_ref, d_ref, w_ref):
    s = pl.program_id(0)
    half = s & 1
    for k in range(2):
        d_ref[k] = jnp.broadcast_to(dw_ref[:, k:k + 1].astype(jnp.int32), d_ref.shape[1:])
        w_ref[k] = jnp.broadcast_to(dw_ref[:, 2 + k:3 + k], w_ref.shape[1:])

    def win_copy(tile, e, which):
        start = pl.multiple_of(r_ref[tile * N_EXPERTS + e], 16)
        return pltpu.make_async_copy(y_hbm.at[pl.ds(start, COMBINE_ROWS)], buf.at[which, e], sem.at[which, e])

    @pl.when(s == 0)
    def _():
        for e in range(N_EXPERTS):
            win_copy(0, e, 0).start()

    @pl.when(s + 1 < pl.num_programs(0))
    def _():
        for e in range(N_EXPERTS):
            win_copy(s + 1, e, 1 - half).start()

    def weighted(first, lo, hi, rows):
        d1 = d_ref[0]
        d2 = d_ref[1]
        in1 = (d1 >= lo) & (d1 < hi)
        in2 = (d2 >= lo) & (d2 < hi)
        wsel = jnp.where(in1, w_ref[0], 0.0) + jnp.where(in2, w_ref[1], 0.0)
        hit = jnp.where(in1, d1, jnp.where(in2, d2, -1)) - first
        hit = jnp.concatenate([hit] * (COMBINE_ROWS // 128), axis=1)
        col = lax.broadcasted_iota(jnp.int32, (ROUTE_TILE, COMBINE_ROWS), 1)
        picked = _dot(jnp.where(hit == col, 1.0, 0.0).astype(BF16), rows)
        return jnp.concatenate([wsel] * (D_MODEL // 128), axis=1) * picked

    for e in range(N_EXPERTS):
        win_copy(s, e, half).wait()
    acc = jnp.zeros((ROUTE_TILE, D_MODEL), F32)
    for e in range(N_EXPERTS):
        i = s * N_EXPERTS + e
        first = r_ref[i]
        acc = acc + weighted(first, lo_ref[i], jnp.minimum(hi_ref[i], first + COMBINE_ROWS), buf[half, e])
    acc_ref[...] = acc

    def overflow(j, carry):
        start = pl.multiple_of(xr_ref[j], 16)
        copy = pltpu.make_async_copy(y_hbm.at[pl.ds(start, COMBINE_ROWS)], xbuf, xsem.at[0])
        copy.start()
        copy.wait()
        acc_ref[...] = acc_ref[...] + weighted(xr_ref[j], xl_ref[j], xh_ref[j], xbuf[...])
        return carry

    lax.fori_loop(es_ref[s], es_ref[s + 1], overflow, 0)
    o_ref[...] = _layer_norm(DEEPNORM_ALPHA * x1_ref[...] + gm_ref[...] * acc_ref[...], lg_ref[...], lb_ref[...])


def _combine(lists, y, dw, x1, mod3, ln_g, ln_b, layer, n_lat, seq, batch):
    n = x1.shape[0]
    nl = n_lat // ROUTE_TILE
    tpb = seq // ROUTE_TILE

    def row(s):
        return jnp.where(s < nl, s // tpb, batch)

    spec = pltpu.PrefetchScalarGridSpec(
        num_scalar_prefetch=7,
        grid=(n // ROUTE_TILE,),
        in_specs=[
            pl.BlockSpec((ROUTE_TILE, DW_LANES), lambda s, *_: (s, 0)),
            pl.BlockSpec((ROUTE_TILE, D_MODEL), lambda s, *_: (s, 0)),
            pl.BlockSpec((None, 1, D_MODEL), lambda s, *_: (row(s), 0, 5)),
            pl.BlockSpec((None, 1, D_MODEL), lambda s, *_: (layer, 0, 0)),
            pl.BlockSpec((None, 1, D_MODEL), lambda s, *_: (layer, 0, 0)),
            pl.BlockSpec(memory_space=pl.ANY),
        ],
        out_specs=pl.BlockSpec((ROUTE_TILE, D_MODEL), lambda s, *_: (s, 0)),
        scratch_shapes=[
            pltpu.VMEM((2, N_EXPERTS, COMBINE_ROWS, D_MODEL), BF16),
            pltpu.SemaphoreType.DMA((2, N_EXPERTS)),
            pltpu.VMEM((COMBINE_ROWS, D_MODEL), BF16),
            pltpu.SemaphoreType.DMA((1,)),
            pltpu.VMEM((ROUTE_TILE, D_MODEL), F32),
            pltpu.VMEM((2, ROUTE_TILE, 128), jnp.int32),
            pltpu.VMEM((2, ROUTE_TILE, 128), F32),
        ],
    )
    return pl.pallas_call(
        _combine_kernel,
        grid_spec=spec,
        out_shape=jax.ShapeDtypeStruct((n, D_MODEL), F32),
        compiler_params=_params(("arbitrary",)),
        name="moe_combine",
    )(*lists, dw, x1, mod3, ln_g, ln_b, y)


def _moe(t, x1, mod3, wr, tri, w1, w3, w2, ln_g, ln_b, layer, j, n_lat, seq, batch):
    n = t.shape[0]
    n_tiles = n // ROUTE_TILE
    n_blocks = -(-(2 * n + N_EXPERTS * (EXPERT_ROWS - 1)) // EXPERT_ROWS) + 1
    i32 = jnp.int32

    route, before, total = _router(t, wr, tri)

    e12 = route[:, 0:2].astype(i32)
    rank = route[:, 4:6].astype(i32)
    counts = total[0, :N_EXPERTS].astype(i32)
    padded = (counts + EXPERT_ROWS - 1) // EXPERT_ROWS * EXPERT_ROWS
    pad_end = jnp.cumsum(padded)
    pad_start = pad_end - padded
    dest = pad_start[e12] + rank
    dst = dest.reshape(n_tiles, ROUTE_TILE, 2).transpose(0, 2, 1)
    dw = jnp.concatenate([dest.astype(F32), route[:, 2:4]], axis=1)
    block_start = jnp.arange(n_blocks, dtype=i32) * EXPERT_ROWS
    block_e = jnp.minimum(jnp.sum(block_start[:, None] >= pad_end[None, :], axis=1), N_EXPERTS - 1).astype(i32)
    block_used = (block_start < pad_end[-1]).astype(i32)

    cb = before[:, 0, :N_EXPERTS].astype(i32)
    ca = jnp.concatenate([cb[1:], counts[None, :]], axis=0)
    lo = pad_start[None, :] + cb
    hi = pad_start[None, :] + ca
    some = hi > lo
    tile_id = jnp.arange(n_tiles, dtype=i32)[:, None, None]

    def grouped(ok, group, n_groups, *values):
        key = jnp.where(ok, group, n_groups).reshape(-1)
        order = jnp.argsort(key, stable=True)
        starts = jnp.sum(key[None, :] < jnp.arange(n_groups + 1, dtype=i32)[:, None], axis=1, dtype=i32)
        return (starts,) + tuple(jnp.broadcast_to(v, ok.shape).reshape(-1)[order].astype(i32) for v in values)

    last_off = EXPERT_ROWS - GATHER_ROWS
    b_lo = lo // EXPERT_ROWS
    parts = []
    for blk, p_lo, p_hi in ((b_lo, lo, jnp.minimum(hi, (b_lo + 1) * EXPERT_ROWS)),
                            (b_lo + 1, (b_lo + 1) * EXPERT_ROWS, hi)):
        l_lo = p_lo - blk * EXPERT_ROWS
        l_hi = p_hi - blk * EXPERT_ROWS
        a0 = jnp.minimum(l_lo // 16 * 16, last_off)
        cut = a0 + GATHER_ROWS
        a1 = jnp.minimum(cut, last_off)
        parts.append((blk, a0, l_lo, jnp.minimum(l_hi, cut)))
        parts.append((blk, a1, jnp.maximum(l_lo, cut), l_hi))
    g_block, g_off, g_lo, g_hi = (jnp.stack(v, axis=-1) for v in zip(*parts))
    g_ok = some[:, :, None] & (g_hi > g_lo)
    g_start, g_tile, g_off, g_lo, g_hi = grouped(g_ok, g_block, n_blocks, tile_id, g_off, g_lo, g_hi)
    xs = _gather((g_start, g_tile | (g_off << 16), g_lo | (g_hi << 16)), t, dst, n_blocks)

    y = _experts(block_e, block_used, xs, w1, w3, w2, j)

    r0 = lo // 16 * 16
    over = some & (hi > r0 + COMBINE_ROWS)
    x_start, x_row, x_lo, x_hi = grouped(over, tile_id[:, :, 0], n_tiles, r0 + COMBINE_ROWS, r0 + COMBINE_ROWS, hi)
    lists = (r0.reshape(-1), lo.reshape(-1), hi.reshape(-1), x_start, x_row, x_lo, x_hi)
    return _combine(lists, y, dw, x1, mod3, ln_g, ln_b, layer, n_lat, seq, batch)


def _window_counts(n, w):
    t = np.arange(n)
    left = w // 2
    right = w - 1 - left
    return (np.minimum(t + right + 1, n) - np.maximum(t - left, 0)).astype(np.float32)


def _window_matrix(n, w):
    left = w // 2
    right = w - 1 - left
    t = np.arange(n)
    return ((t[None, :] >= t[:, None] - left) & (t[None, :] <= t[:, None] + right)).astype(np.float32)


def _pool_tables(seq, n_ctx):
    rows = seq // GRID_W
    per_tile = 256 // GRID_W
    mc = np.stack([np.kron(np.eye(per_tile, dtype=np.float32), _window_matrix(GRID_W, w)) for w in POOL_WINDOWS])
    inv = np.stack([1.0 / np.outer(_window_counts(rows, w), _window_counts(GRID_W, w)).reshape(seq)
                    for w in POOL_WINDOWS])
    inv = np.broadcast_to(inv[:, :, None], (4, seq, POOL_GROUP)).astype(np.float32)
    m1d = np.stack([_window_matrix(n_ctx, w) for w in POOL_WINDOWS])
    inv1d = np.stack([1.0 / _window_counts(n_ctx, w) for w in POOL_WINDOWS])
    inv1d = np.broadcast_to(inv1d[:, :, None], (4, n_ctx, POOL_GROUP)).astype(np.float32)
    return (jnp.asarray(mc, BF16), jnp.asarray(inv), jnp.asarray(m1d, BF16), jnp.asarray(inv1d))


def _rope_tables(seq):
    f32 = np.float32
    t = np.arange(seq)
    row = (t // GRID_W).astype(f32)
    col = (t % GRID_W).astype(f32)
    n_freq = RET_QK_DIM // 4
    inv = np.exp(-np.log(f32(ROPE_BASE)) * np.arange(n_freq, dtype=f32) / f32(n_freq)).astype(f32)
    ang = np.concatenate([row[:, None] * inv, col[:, None] * inv], -1).astype(f32)
    cos, sin = np.cos(ang).astype(f32), np.sin(ang).astype(f32)
    cos2 = np.concatenate([cos, cos], -1)
    sin2 = np.concatenate([-sin, sin], -1)
    cos2 = np.concatenate([cos2, np.ones((TOKEN_TILE, RET_QK_DIM), f32)], 0)
    sin2 = np.concatenate([sin2, np.zeros((TOKEN_TILE, RET_QK_DIM), f32)], 0)
    return jnp.asarray(cos2), jnp.asarray(sin2)


def kernel(x, c, ctx, c_ctx, ada_w, ada_b, w_in, pool_w, pool_scale, w_pool_out, w_ret_out, ret_decay_logit,
           w_out, ln_mix_g, ln_mix_b, ln_ffn_g, ln_ffn_b, ffn_w1, ffn_w3, ffn_w2, moe_router, moe_w1, moe_w3,
           moe_w2):
    batch, seq, d = x.shape
    n_ctx = ctx.shape[1]
    n_lat = batch * seq
    depth = ada_w.shape[0]
    assert d == D_MODEL and depth == DEPTH and batch < MOD_ROWS
    assert seq % TOKEN_TILE == 0 and (batch * n_ctx) % TOKEN_TILE == 0 and n_lat % n_ctx == 0
    assert FFN_ROWS % ROUTER_STEP == 0 and ROUTER_STEP % ROUTE_TILE == 0
    assert seq % FFN_ROWS == 0 and (batch * n_ctx) % FFN_ROWS == 0

    s_in = jnp.zeros((MOD_ROWS, d), F32).at[:batch].set(c).at[batch].set(c_ctx)
    mod = _modulation(s_in, ada_w, ada_b)
    cos2, sin2 = _rope_tables(seq)
    mc, inv, m1d, inv1d = _pool_tables(seq, n_ctx)
    tri = jnp.asarray(np.tril(np.ones((ROUTER_STEP, ROUTER_STEP), np.float32), -1), BF16)

    w_in_b = w_in.astype(BF16)
    pool_w_b = pool_w.astype(BF16)
    pool_s = pool_scale.reshape(depth, 1, POOL_WIDTH)
    wp_b, wr_b, wo_b = w_pool_out.astype(BF16), w_ret_out.astype(BF16), w_out.astype(BF16)
    ffn_b = ffn_w1.astype(BF16), ffn_w3.astype(BF16), ffn_w2.astype(BF16)
    moe_b = moe_w1.astype(BF16), moe_w3.astype(BF16), moe_w2.astype(BF16)
    ln_mix = ln_mix_g.reshape(depth, 1, d), ln_mix_b.reshape(depth, 1, d)
    ln_ffn = ln_ffn_g.reshape(depth, 1, d), ln_ffn_b.reshape(depth, 1, d)

    n_tok = n_lat + batch * n_ctx
    x_parts = (x.reshape(n_lat, d), ctx.reshape(batch * n_ctx, d))
    for l in range(depth):
        last = l == depth - 1
        mod3 = mod[l].reshape(MOD_ROWS, 1, 6 * d)
        pa, pb = _in_proj(x_parts, mod3, cos2, sin2, w_in_b, l, n_tok, n_lat, seq, batch)
        yp_lat = _pool_lat(pa, mc, inv, pool_w_b, pool_s, l, batch, seq)
        yp_ctx = _pool_ctx(pa, m1d, inv1d, pool_w_b, pool_s, l, batch, n_lat, n_ctx)
        dl = jnp.broadcast_to(ret_decay_logit[l].astype(F32)[:, :, None, None],
                              (2, RET_HEADS, RET_CHUNK, RET_V_DIM))
        z_lat, z_ctx = _retention(pa, pb, dl, batch, seq, n_ctx)
        n_out = n_lat if last else n_tok
        x1, t = _merge(x_parts, mod3, yp_lat, yp_ctx, z_lat, z_ctx, pb, wp_b, wr_b, wo_b,
                       *ln_mix, l, n_out, n_lat, seq, batch)
        j = l // 2
        if l % 2 == 0:
            xs = _ffn(t, x1, mod3, *ffn_b, *ln_ffn, l, j, n_lat, seq, batch)
        else:
            wr = jnp.zeros((d, ROUTE_LANES), BF16).at[:, :N_EXPERTS].set(moe_router[j].astype(BF16))
            xs = _moe(t, x1, mod3, wr, tri, *moe_b, *ln_ffn, l, j, n_lat, seq, batch)
        x_parts = (xs,)
    return xs[:n_lat].reshape(batch, seq, d)
```

```python
import functools

import jax
import jax.numpy as jnp
import numpy as np
from jax import lax
from jax.experimental import pallas as pl
from jax.experimental.pallas import tpu as pltpu

F32 = jnp.float32
BF16 = jnp.bfloat16

D_MODEL = 1024
DEPTH = 4
GRID_W = 64
POOL_WINDOWS = (2, 4, 8, 16)
POOL_GROUP = 128
POOL_WIDTH = POOL_GROUP * len(POOL_WINDOWS)
RET_HEADS = 4
RET_QK_DIM = 128
RET_V_DIM = 256
RET_CHUNK = 128
ROPE_BASE = 10000.0
OFF_Q = POOL_WIDTH
OFF_K = OFF_Q + RET_HEADS * RET_QK_DIM
OFF_V = OFF_K + RET_HEADS * RET_QK_DIM
OFF_G = OFF_V + RET_HEADS * RET_V_DIM
WIDTH_B = 3 * D_MODEL
D_FF = 2816
N_EXPERTS = 8
EXPERT_FF = 3584
DEEPNORM_ALPHA = (2 * DEPTH) ** 0.25
LN_EPS = 1e-5
K_SCALE = RET_QK_DIM ** -0.5

MOD_ROWS = 24
TOKEN_TILE = 512
ROUTE_TILE = 256
ROUTER_STEP = 512
EXPERT_ROWS = 512
FFN_ROWS = 2 * TOKEN_TILE
VMEM_LIMIT = 56 * 1024 * 1024


def _dot(a, b):
    return jnp.dot(a, b, preferred_element_type=F32)


def _sigmoid(x):
    return 1.0 / (1.0 + jnp.exp(-x))


def _split_bf16(a):
    hi = a.astype(BF16)
    lo = (a - hi.astype(F32)).astype(BF16)
    return hi, lo


def _layer_norm(v, g, b):
    mean = jnp.mean(v, axis=-1, keepdims=True)
    vc = v - mean
    var = jnp.mean(vc * vc, axis=-1, keepdims=True)
    return vc * lax.rsqrt(var + LN_EPS) * g + b


def _params(sem, vmem=VMEM_LIMIT):
    return pltpu.CompilerParams(dimension_semantics=sem, vmem_limit_bytes=vmem)


def _mod_kernel(s_ref, w_ref, b_ref, o_ref):
    s = s_ref[...]
    s = s * _sigmoid(s)
    s_hi, s_lo = _split_bf16(s)
    w_hi, w_lo = _split_bf16(w_ref[...])
    o_ref[...] = _dot(s_hi, w_hi) + (_dot(s_hi, w_lo) + _dot(s_lo, w_hi)) + b_ref[...]


def _modulation(s_in, ada_w, ada_b):
    depth, d, width = ada_w.shape
    tn = 1536
    return pl.pallas_call(
        _mod_kernel,
        grid=(depth, width // tn),
        in_specs=[
            pl.BlockSpec((MOD_ROWS, d), lambda l, j: (0, 0)),
            pl.BlockSpec((None, d, tn), lambda l, j: (l, 0, j)),
            pl.BlockSpec((None, 1, tn), lambda l, j: (l, 0, j)),
        ],
        out_specs=pl.BlockSpec((None, MOD_ROWS, tn), lambda l, j: (l, 0, j)),
        out_shape=jax.ShapeDtypeStruct((depth, MOD_ROWS, width), F32),
        compiler_params=_params(("arbitrary", "arbitrary")),
        name="adaln_mod",
    )(s_in, ada_w, ada_b.reshape(depth, 1, width))


IN_COLS = 512
IN_BLOCKS = (OFF_G + WIDTH_B) // IN_COLS


def _pick_x(x_refs, rows, is_lat):
    if len(x_refs) == 1:
        return x_refs[0][rows, :]
    return jnp.where(is_lat, x_refs[0][rows, :], x_refs[1][rows, :])


def _in_kernel(*refs, n_lat_tiles, n_x):
    x_refs = refs[:n_x]
    sh_ref, sc_ref, cos_ref, sin_ref = refs[n_x:n_x + 4]
    w_refs = refs[n_x + 4:n_x + 4 + IN_BLOCKS]
    u_ref, q_ref, k_ref, v_ref, g_ref, gate_ref = refs[n_x + 4 + IN_BLOCKS:]
    x = _pick_x(x_refs, slice(None), pl.program_id(0) < n_lat_tiles)
    h = (x * (1.0 + sc_ref[...]) + sh_ref[...]).astype(BF16)
    cos = cos_ref[...]
    sin = sin_ref[...]
    per_blk = IN_COLS // RET_V_DIM
    for blk in range(IN_BLOCKS):
        c0 = blk * IN_COLS
        acc = _dot(h, w_refs[blk][...])
        if c0 < OFF_Q:
            u_ref[...] = acc.astype(BF16)
        elif c0 < OFF_V:
            for hh in range(RET_HEADS):
                t = acc[:, hh * RET_QK_DIM:(hh + 1) * RET_QK_DIM]
                r = t * cos + pltpu.roll(t, RET_QK_DIM // 2, 1) * sin
                if c0 == OFF_K:
                    k_ref[hh] = (r * K_SCALE).astype(BF16)
                else:
                    q_ref[hh] = r.astype(BF16)
        elif c0 < OFF_G + D_MODEL:
            out_ref, base = (v_ref, OFF_V) if c0 < OFF_G else (g_ref, OFF_G)
            for j in range(per_blk):
                out_ref[(c0 - base) // RET_V_DIM + j] = acc[:, j * RET_V_DIM:(j + 1) * RET_V_DIM].astype(BF16)
        else:
            g0 = c0 - OFF_G - D_MODEL
            gate_ref[:, g0:g0 + IN_COLS] = acc.astype(BF16)


def _tile_maps(n_lat, seq, batch, tile):
    nl = n_lat // tile
    tpb = seq // tile

    def row(i):
        return jnp.where(i < nl, i // tpb, batch)

    def lat(i):
        return jnp.minimum(i, nl - 1)

    def ctx(i):
        return jnp.maximum(i - nl, 0)

    return nl, tpb, row, lat, ctx


def _x_operands(x_parts, n_lat, seq, batch, tile):
    _, _, _, lat, ctx = _tile_maps(n_lat, seq, batch, tile)
    if len(x_parts) == 1:
        return [pl.BlockSpec((tile, D_MODEL), lambda i: (i, 0))], list(x_parts)
    return ([pl.BlockSpec((tile, D_MODEL), lambda i: (lat(i), 0)),
             pl.BlockSpec((tile, D_MODEL), lambda i: (ctx(i), 0))], list(x_parts))


def _in_proj(x_parts, mod3, cos2, sin2, w_in, layer, n_tok, n_lat, seq, batch):
    tm = TOKEN_TILE
    nl, tpb, row, _, _ = _tile_maps(n_lat, seq, batch, tm)
    x_specs, x_args = _x_operands(x_parts, n_lat, seq, batch, tm)

    def rope(i):
        return jnp.where(i < nl, i % tpb, tpb)

    def w_spec(blk):
        return pl.BlockSpec((None, D_MODEL, IN_COLS), lambda i: (layer, 0, blk))

    return pl.pallas_call(
        functools.partial(_in_kernel, n_lat_tiles=nl, n_x=len(x_args)),
        grid=(n_tok // tm,),
        in_specs=x_specs + [
            pl.BlockSpec((None, 1, D_MODEL), lambda i: (row(i), 0, 0)),
            pl.BlockSpec((None, 1, D_MODEL), lambda i: (row(i), 0, 1)),
            pl.BlockSpec((tm, RET_QK_DIM), lambda i: (rope(i), 0)),
            pl.BlockSpec((tm, RET_QK_DIM), lambda i: (rope(i), 0)),
        ] + [w_spec(blk) for blk in range(IN_BLOCKS)],
        out_specs=[
            pl.BlockSpec((tm, POOL_WIDTH), lambda i: (i, 0)),
            pl.BlockSpec((RET_HEADS, tm, RET_QK_DIM), lambda i: (0, i, 0)),
            pl.BlockSpec((RET_HEADS, tm, RET_QK_DIM), lambda i: (0, i, 0)),
            pl.BlockSpec((RET_HEADS, tm, RET_V_DIM), lambda i: (0, i, 0)),
            pl.BlockSpec((RET_HEADS, tm, RET_V_DIM), lambda i: (0, i, 0)),
            pl.BlockSpec((tm, 2 * D_MODEL), lambda i: (i, 0)),
        ],
        out_shape=[
            jax.ShapeDtypeStruct((n_tok, POOL_WIDTH), BF16),
            jax.ShapeDtypeStruct((RET_HEADS, n_tok, RET_QK_DIM), BF16),
            jax.ShapeDtypeStruct((RET_HEADS, n_tok, RET_QK_DIM), BF16),
            jax.ShapeDtypeStruct((RET_HEADS, n_tok, RET_V_DIM), BF16),
            jax.ShapeDtypeStruct((RET_HEADS, n_tok, RET_V_DIM), BF16),
            jax.ShapeDtypeStruct((n_tok, 2 * D_MODEL), BF16),
        ],
        compiler_params=_params(("arbitrary",)),
        name="in_proj",
    )(*x_args, mod3, mod3, cos2, sin2, *([w_in] * IN_BLOCKS))


POOL_PAD_ROWS = 8
POOL_ROW_CHUNK = 8


def _pool_kernel(u_ref, mc_ref, inv_ref, pw_ref, ps_ref, o_ref, zp_ref, *, rows):
    seq = rows * GRID_W
    pad = POOL_PAD_ROWS * GRID_W
    chunk = POOL_ROW_CHUNK * GRID_W
    zeros = jnp.zeros((pad, POOL_GROUP), F32)
    for g, w in enumerate(POOL_WINDOWS):
        lanes = slice(g * POOL_GROUP, (g + 1) * POOL_GROUP)
        zp_ref[0:pad, :] = zeros
        zp_ref[pad + seq:pad + seq + pad, :] = zeros
        for c in range(seq // 256):
            zp_ref[pad + c * 256:pad + (c + 1) * 256, :] = _dot(mc_ref[g], u_ref[c * 256:(c + 1) * 256, lanes])
        left = w // 2

        def body(rc, carry, g=g, w=w, left=left, lanes=lanes):
            tok = pl.multiple_of(rc * chunk, chunk)
            acc = zp_ref[pl.ds(tok + pad - left * GRID_W, chunk), :]
            for k in range(1, w):
                acc = acc + zp_ref[pl.ds(tok + pad + (k - left) * GRID_W, chunk), :]
            pooled = acc * inv_ref[g, pl.ds(tok, chunk), :]
            d = (pooled - u_ref[pl.ds(tok, chunk), lanes].astype(F32)).astype(BF16)
            y = _dot(d, pw_ref[g]) * ps_ref[:, lanes]
            o_ref[pl.ds(tok, chunk), lanes] = y.astype(BF16)
            return carry

        lax.fori_loop(0, rows // POOL_ROW_CHUNK, body, 0, unroll=4)


def _pool_lat(pa, mc, inv, pw, ps, layer, batch, seq):
    rows = seq // GRID_W
    return pl.pallas_call(
        functools.partial(_pool_kernel, rows=rows),
        grid=(batch,),
        in_specs=[
            pl.BlockSpec((seq, POOL_WIDTH), lambda b: (b, 0)),
            pl.BlockSpec((4, 256, 256), lambda b: (0, 0, 0)),
            pl.BlockSpec((4, seq, POOL_GROUP), lambda b: (0, 0, 0)),
            pl.BlockSpec((None, 4, POOL_GROUP, POOL_GROUP), lambda b: (layer, 0, 0, 0)),
            pl.BlockSpec((None, 1, POOL_WIDTH), lambda b: (layer, 0, 0)),
        ],
        out_specs=pl.BlockSpec((seq, POOL_WIDTH), lambda b: (b, 0)),
        out_shape=jax.ShapeDtypeStruct((batch * seq, POOL_WIDTH), BF16),
        scratch_shapes=[pltpu.VMEM(((rows + 2 * POOL_PAD_ROWS) * GRID_W, POOL_GROUP), F32)],
        compiler_params=_params(("arbitrary",)),
        name="pool_lat",
    )(pa, mc, inv, pw, ps)


def _pool_ctx_kernel(u_ref, m_ref, inv_ref, pw_ref, ps_ref, o_ref):
    for g in range(len(POOL_WINDOWS)):
        lanes = slice(g * POOL_GROUP, (g + 1) * POOL_GROUP)
        ug = u_ref[:, lanes]
        pooled = _dot(m_ref[g], ug) * inv_ref[g]
        d = (pooled - ug.astype(F32)).astype(BF16)
        o_ref[:, lanes] = (_dot(d, pw_ref[g]) * ps_ref[:, lanes]).astype(BF16)


def _pool_ctx(pa, m1d, inv1d, pw, ps, layer, batch, n_lat, n_ctx):
    first = n_lat // n_ctx
    return pl.pallas_call(
        _pool_ctx_kernel,
        grid=(batch,),
        in_specs=[
            pl.BlockSpec((n_ctx, POOL_WIDTH), lambda b: (first + b, 0)),
            pl.BlockSpec((4, n_ctx, n_ctx), lambda b: (0, 0, 0)),
            pl.BlockSpec((4, n_ctx, POOL_GROUP), lambda b: (0, 0, 0)),
            pl.BlockSpec((None, 4, POOL_GROUP, POOL_GROUP), lambda b: (layer, 0, 0, 0)),
            pl.BlockSpec((None, 1, POOL_WIDTH), lambda b: (layer, 0, 0)),
        ],
        out_specs=pl.BlockSpec((n_ctx, POOL_WIDTH), lambda b: (b, 0)),
        out_shape=jax.ShapeDtypeStruct((batch * n_ctx, POOL_WIDTH), BF16),
        compiler_params=_params(("arbitrary",)),
        name="pool_ctx",
    )(pa, m1d, inv1d, pw, ps)


def _log_sigmoid(x):
    return jnp.minimum(x, 0.0) - jnp.log1p(jnp.exp(-jnp.abs(x)))


def _decays(dl, backward):
    c = RET_CHUNK
    lg = _log_sigmoid(dl)
    lgq = lg[:, :RET_QK_DIM]
    ii = lax.broadcasted_iota(jnp.int32, (c, c), 0)
    jj = lax.broadcasted_iota(jnp.int32, (c, c), 1)
    pos = lax.broadcasted_iota(jnp.int32, (c, RET_QK_DIM), 0).astype(F32)
    if backward:
        diff = (jj - ii).astype(F32)
        qdec = jnp.exp(lgq * (c - pos))
        kdec = jnp.exp(lgq * pos)
    else:
        diff = (ii - jj).astype(F32)
        qdec = jnp.exp(lgq * (pos + 1.0))
        kdec = jnp.exp(lgq * (c - 1.0 - pos))
    inner = jnp.where(diff >= 0, jnp.exp(lgq * jnp.maximum(diff, 0.0)), 0.0)
    cdec = jnp.exp(lg * float(c))
    return inner, qdec, kdec, cdec


def _chunk_kv(kc, vc, kdec):
    kd = (kc.astype(F32) * kdec).astype(BF16)
    return lax.dot_general(kd, vc, (((0,), (0,)), ((), ())), preferred_element_type=F32)


def _chunk_out(qc, kc, vc, gc, states, mask, qdec2):
    scores = lax.dot_general(qc, kc, (((1,), (1,)), ((), ())), preferred_element_type=F32) * mask
    qf = qc.astype(F32)
    qd = (jnp.concatenate([qf, qf], axis=1) * qdec2).astype(BF16)
    o = _dot(scores.astype(BF16), vc) + _dot(qd, states)
    mean = jnp.mean(o, axis=-1, keepdims=True)
    oc = o - mean
    var = jnp.mean(oc * oc, axis=-1, keepdims=True)
    on = oc * lax.rsqrt(var + LN_EPS)
    gf = gc.astype(F32)
    return (gf * _sigmoid(gf) * on).astype(BF16)


def _ret_kernel(q_ref, k_ref, v_ref, g_ref, qc_ref, kc_ref, vc_ref, gc_ref, dl_ref,
                z_ref, zc_ref, st_ref, stc_ref, run_ref, *, n_chunks, n_ctx_chunks):
    c = RET_CHUNK
    dk = RET_QK_DIM
    inner_f, qdec_f, kdec_f, cdec_f = _decays(dl_ref[0], False)
    inner_b, qdec_b, kdec_b, cdec_b = _decays(dl_ref[1], True)
    mask = inner_f + inner_b
    qdec2 = jnp.concatenate([qdec_f, qdec_b], axis=1)

    def rows(j):
        return slice(j * c, (j + 1) * c)

    s = jnp.zeros((dk, RET_V_DIM), F32)
    for j in range(n_ctx_chunks):
        stc_ref[j, 0:dk, :] = s.astype(BF16)
        s = s * cdec_f + _chunk_kv(kc_ref[rows(j), :], vc_ref[rows(j), :], kdec_f)
    run_ref[0] = s
    s = jnp.zeros((dk, RET_V_DIM), F32)
    for j in reversed(range(n_ctx_chunks)):
        stc_ref[j, dk:2 * dk, :] = s.astype(BF16)
        s = s * cdec_b + _chunk_kv(kc_ref[rows(j), :], vc_ref[rows(j), :], kdec_b)
    run_ref[1] = s
    for j in range(n_ctx_chunks):
        zc_ref[rows(j), :] = _chunk_out(qc_ref[rows(j), :], kc_ref[rows(j), :], vc_ref[rows(j), :],
                                        gc_ref[rows(j), :], stc_ref[j], mask, qdec2)

    def scan(t, carry):
        rf = pl.ds(pl.multiple_of(t * c, c), c)
        sf = run_ref[0]
        st_ref[t, 0:dk, :] = sf.astype(BF16)
        run_ref[0] = sf * cdec_f + _chunk_kv(k_ref[rf, :], v_ref[rf, :], kdec_f)
        tb = n_chunks - 1 - t
        rb = pl.ds(pl.multiple_of(tb * c, c), c)
        sb = run_ref[1]
        st_ref[tb, dk:2 * dk, :] = sb.astype(BF16)
        run_ref[1] = sb * cdec_b + _chunk_kv(k_ref[rb, :], v_ref[rb, :], kdec_b)
        return carry

    lax.fori_loop(0, n_chunks, scan, 0, unroll=4)

    def emit(t, carry):
        r = pl.ds(pl.multiple_of(t * c, c), c)
        z_ref[r, :] = _chunk_out(q_ref[r, :], k_ref[r, :], v_ref[r, :], g_ref[r, :], st_ref[t], mask, qdec2)
        return carry

    lax.fori_loop(0, n_chunks, emit, 0, unroll=8)


def _retention(q, k, v, g, dl, batch, seq, n_ctx):
    n_lat = batch * seq
    first = n_lat // n_ctx
    return pl.pallas_call(
        functools.partial(_ret_kernel, n_chunks=seq // RET_CHUNK, n_ctx_chunks=n_ctx // RET_CHUNK),
        grid=(batch, RET_HEADS),
        in_specs=[
            pl.BlockSpec((None, seq, RET_QK_DIM), lambda b, h: (h, b, 0)),
            pl.BlockSpec((None, seq, RET_QK_DIM), lambda b, h: (h, b, 0)),
            pl.BlockSpec((None, seq, RET_V_DIM), lambda b, h: (h, b, 0)),
            pl.BlockSpec((None, seq, RET_V_DIM), lambda b, h: (h, b, 0)),
            pl.BlockSpec((None, n_ctx, RET_QK_DIM), lambda b, h: (h, first + b, 0)),
            pl.BlockSpec((None, n_ctx, RET_QK_DIM), lambda b, h: (h, first + b, 0)),
            pl.BlockSpec((None, n_ctx, RET_V_DIM), lambda b, h: (h, first + b, 0)),
            pl.BlockSpec((None, n_ctx, RET_V_DIM), lambda b, h: (h, first + b, 0)),
            pl.BlockSpec((2, None, RET_CHUNK, RET_V_DIM), lambda b, h: (0, h, 0, 0)),
        ],
        out_specs=[
            pl.BlockSpec((seq, RET_V_DIM), lambda b, h: (b, h)),
            pl.BlockSpec((n_ctx, RET_V_DIM), lambda b, h: (b, h)),
        ],
        out_shape=[
            jax.ShapeDtypeStruct((n_lat, RET_HEADS * RET_V_DIM), BF16),
            jax.ShapeDtypeStruct((batch * n_ctx, RET_HEADS * RET_V_DIM), BF16),
        ],
        scratch_shapes=[
            pltpu.VMEM((seq // RET_CHUNK, 2 * RET_QK_DIM, RET_V_DIM), BF16),
            pltpu.VMEM((n_ctx // RET_CHUNK, 2 * RET_QK_DIM, RET_V_DIM), BF16),
            pltpu.VMEM((2, RET_QK_DIM, RET_V_DIM), F32),
        ],
        compiler_params=_params(("arbitrary", "arbitrary")),
        name="retention",
    )(q, k, v, g, q, k, v, g, dl)


MIX_COLS = 256


def _merge_kernel(*refs, n_lat_tiles, n_x):
    x_refs = refs[:n_x]
    (gm_ref, sh_ref, sc_ref, ypl_ref, ypc_ref, zl_ref, zc_ref, gp_ref, gr_ref,
     wp_ref, wr_ref, wo_ref, lg_ref, lb_ref, x1_ref, t_ref, mix_ref) = refs[n_x:]
    is_lat = pl.program_id(0) < n_lat_tiles
    yp = jnp.where(is_lat, ypl_ref[...], ypc_ref[...])
    z = jnp.where(is_lat, zl_ref[...], zc_ref[...])
    x = _pick_x(x_refs, slice(None), is_lat)
    for c0 in range(0, D_MODEL, MIX_COLS):
        cols = slice(c0, c0 + MIX_COLS)
        y_pool = _dot(yp, wp_ref[:, cols])
        y_ret = _dot(z, wr_ref[:, cols])
        mix = _sigmoid(gp_ref[:, cols].astype(F32)) * y_pool + _sigmoid(gr_ref[:, cols].astype(F32)) * y_ret
        mix_ref[:, cols] = mix.astype(BF16)
    y = _dot(mix_ref[...], wo_ref[...])
    x1 = _layer_norm(DEEPNORM_ALPHA * x + gm_ref[...] * y, lg_ref[...], lb_ref[...])
    x1_ref[...] = x1
    t_ref[...] = (x1 * (1.0 + sc_ref[...]) + sh_ref[...]).astype(BF16)


def _merge(x_parts, mod3, yp_lat, yp_ctx, z_lat, z_ctx, gates, wp, wr, wo, ln_g, ln_b, layer,
           n_out, n_lat, seq, batch):
    tm = TOKEN_TILE
    nl, _, row, lat, ctx = _tile_maps(n_lat, seq, batch, tm)
    x_specs, x_args = _x_operands(x_parts, n_lat, seq, batch, tm)

    return pl.pallas_call(
        functools.partial(_merge_kernel, n_lat_tiles=nl, n_x=len(x_args)),
        grid=(n_out // tm,),
        in_specs=x_specs + [
            pl.BlockSpec((None, 1, D_MODEL), lambda i: (row(i), 0, 2)),
            pl.BlockSpec((None, 1, D_MODEL), lambda i: (row(i), 0, 3)),
            pl.BlockSpec((None, 1, D_MODEL), lambda i: (row(i), 0, 4)),
            pl.BlockSpec((tm, POOL_WIDTH), lambda i: (lat(i), 0)),
            pl.BlockSpec((tm, POOL_WIDTH), lambda i: (ctx(i), 0)),
            pl.BlockSpec((tm, D_MODEL), lambda i: (lat(i), 0)),
            pl.BlockSpec((tm, D_MODEL), lambda i: (ctx(i), 0)),
            pl.BlockSpec((tm, D_MODEL), lambda i: (i, 0)),
            pl.BlockSpec((tm, D_MODEL), lambda i: (i, 1)),
            pl.BlockSpec((None, POOL_WIDTH, D_MODEL), lambda i: (layer, 0, 0)),
            pl.BlockSpec((None, D_MODEL, D_MODEL), lambda i: (layer, 0, 0)),
            pl.BlockSpec((None, D_MODEL, D_MODEL), lambda i: (layer, 0, 0)),
            pl.BlockSpec((None, 1, D_MODEL), lambda i: (layer, 0, 0)),
            pl.BlockSpec((None, 1, D_MODEL), lambda i: (layer, 0, 0)),
        ],
        out_specs=[
            pl.BlockSpec((tm, D_MODEL), lambda i: (i, 0)),
            pl.BlockSpec((tm, D_MODEL), lambda i: (i, 0)),
        ],
        out_shape=[
            jax.ShapeDtypeStruct((n_out, D_MODEL), F32),
            jax.ShapeDtypeStruct((n_out, D_MODEL), BF16),
        ],
        scratch_shapes=[pltpu.VMEM((tm, D_MODEL), BF16)],
        compiler_params=_params(("arbitrary",)),
        name="merge",
    )(*x_args, mod3, mod3, mod3, yp_lat, yp_ctx, z_lat, z_ctx, gates, gates, wp, wr, wo, ln_g, ln_b)


SWIGLU_COLS = 256


def _swiglu_hidden(t, w1_ref, w3_ref, h_ref):
    width = h_ref.shape[1]
    for c0 in range(0, width, SWIGLU_COLS):
        cols = slice(c0, min(c0 + SWIGLU_COLS, width))
        a = _dot(t, w1_ref[:, cols])
        h_ref[:, cols] = (a * _sigmoid(a) * _dot(t, w3_ref[:, cols])).astype(BF16)


def _ffn_kernel(t_ref, x1_ref, gm_ref, w1_ref, w3_ref, w2_ref, lg_ref, lb_ref, o_ref, h_ref):
    for r in range(FFN_ROWS // TOKEN_TILE):
        rows = slice(r * TOKEN_TILE, (r + 1) * TOKEN_TILE)
        _swiglu_hidden(t_ref[rows, :], w1_ref, w3_ref, h_ref.at[r])
        f = _dot(h_ref[r], w2_ref[...])
        o_ref[rows, :] = _layer_norm(DEEPNORM_ALPHA * x1_ref[rows, :] + gm_ref[...] * f, lg_ref[...], lb_ref[...])


def _ffn(t, x1, mod3, w1, w3, w2, ln_g, ln_b, layer, j, n_lat, seq, batch):
    n = t.shape[0]
    tm = FFN_ROWS
    _, _, row, _, _ = _tile_maps(n_lat, seq, batch, tm)
    resident = pl.Buffered(1)

    return pl.pallas_call(
        _ffn_kernel,
        grid=(n // tm,),
        in_specs=[
            pl.BlockSpec((tm, D_MODEL), lambda i: (i, 0)),
            pl.BlockSpec((tm, D_MODEL), lambda i: (i, 0)),
            pl.BlockSpec((None, 1, D_MODEL), lambda i: (row(i), 0, 5)),
            pl.BlockSpec((None, D_MODEL, D_FF), lambda i: (j, 0, 0), pipeline_mode=resident),
            pl.BlockSpec((None, D_MODEL, D_FF), lambda i: (j, 0, 0), pipeline_mode=resident),
            pl.BlockSpec((None, D_FF, D_MODEL), lambda i: (j, 0, 0), pipeline_mode=resident),
            pl.BlockSpec((None, 1, D_MODEL), lambda i: (layer, 0, 0)),
            pl.BlockSpec((None, 1, D_MODEL), lambda i: (layer, 0, 0)),
        ],
        out_specs=pl.BlockSpec((tm, D_MODEL), lambda i: (i, 0)),
        out_shape=jax.ShapeDtypeStruct((n, D_MODEL), F32),
        scratch_shapes=[pltpu.VMEM((tm // TOKEN_TILE, TOKEN_TILE, D_FF), BF16)],
        compiler_params=_params(("arbitrary",)),
        name="ffn_dense",
    )(t, x1, mod3, w1, w3, w2, ln_g, ln_b)


ROUTE_LANES = 128


def _router_kernel(t_ref, wr_ref, tri_ref, route_ref, before_ref, total_ref, run_ref):
    @pl.when(pl.program_id(0) == 0)
    def _():
        run_ref[...] = jnp.zeros_like(run_ref)

    tt = t_ref.shape[0]
    logits = _dot(t_ref[...], wr_ref[...])
    lane = lax.broadcasted_iota(jnp.int32, (tt, ROUTE_LANES), 1)
    neg = jnp.float32(-jnp.inf)
    lg = jnp.where(lane < N_EXPERTS, logits, neg)
    m1 = jnp.max(lg, axis=1, keepdims=True)
    i1 = jnp.min(jnp.where(lg == m1, lane, ROUTE_LANES), axis=1, keepdims=True)
    lg2 = jnp.where(lane == i1, neg, lg)
    m2 = jnp.max(lg2, axis=1, keepdims=True)
    i2 = jnp.min(jnp.where(lg2 == m2, lane, ROUTE_LANES), axis=1, keepdims=True)
    e = jnp.exp(m2 - m1)
    w1 = 1.0 / (1.0 + e)
    w2 = e / (1.0 + e)
    hit1 = lane == i1
    hit2 = lane == i2
    onehot = jnp.where(hit1 | hit2, 1.0, 0.0)
    run = run_ref[...]
    prefix = _dot(tri_ref[...], onehot.astype(BF16)) + run
    r1 = jnp.sum(jnp.where(hit1, prefix, 0.0), axis=1, keepdims=True)
    r2 = jnp.sum(jnp.where(hit2, prefix, 0.0), axis=1, keepdims=True)
    for k in range(tt // ROUTE_TILE):
        before_ref[k] = prefix[k * ROUTE_TILE:k * ROUTE_TILE + 1, :]
    run = run + jnp.sum(onehot, axis=0, keepdims=True)
    run_ref[...] = run
    total_ref[...] = run
    out = jnp.where(lane == 0, i1.astype(F32), 0.0)
    out = jnp.where(lane == 1, i2.astype(F32), out)
    out = jnp.where(lane == 2, w1, out)
    out = jnp.where(lane == 3, w2, out)
    out = jnp.where(lane == 4, r1, out)
    out = jnp.where(lane == 5, r2, out)
    route_ref[...] = out


def _router(t, wr, tri):
    n = t.shape[0]
    tt = ROUTER_STEP
    sub = tt // ROUTE_TILE
    return pl.pallas_call(
        _router_kernel,
        grid=(n // tt,),
        in_specs=[
            pl.BlockSpec((tt, D_MODEL), lambda i: (i, 0)),
            pl.BlockSpec((D_MODEL, ROUTE_LANES), lambda i: (0, 0)),
            pl.BlockSpec((tt, tt), lambda i: (0, 0)),
        ],
        out_specs=[
            pl.BlockSpec((tt, ROUTE_LANES), lambda i: (i, 0)),
            pl.BlockSpec((sub, 1, ROUTE_LANES), lambda i: (i, 0, 0)),
            pl.BlockSpec((1, ROUTE_LANES), lambda i: (0, 0)),
        ],
        out_shape=[
            jax.ShapeDtypeStruct((n, ROUTE_LANES), F32),
            jax.ShapeDtypeStruct((n // ROUTE_TILE, 1, ROUTE_LANES), F32),
            jax.ShapeDtypeStruct((1, ROUTE_LANES), F32),
        ],
        scratch_shapes=[pltpu.VMEM((1, ROUTE_LANES), F32)],
        compiler_params=_params(("arbitrary",)),
        name="moe_router",
    )(t, wr, tri)


DMA_RING = 16
DMA_AHEAD = 8
GATHER_ROWS = 144
GATHER_SPLIT = 4


def _gather_kernel(es_ref, sa_ref, rr_ref, dst_ref, t_hbm, o_ref, buf, sem):
    b = pl.program_id(0)
    total = es_ref[pl.num_programs(0)]
    o_ref[...] = jnp.zeros_like(o_ref)

    def tile_copy(j):
        slot = j & (DMA_RING - 1)
        start = pl.multiple_of((sa_ref[j] & 0xFFFF) * ROUTE_TILE, ROUTE_TILE)
        return pltpu.make_async_copy(t_hbm.at[pl.ds(start, ROUTE_TILE)], buf.at[slot], sem.at[slot])

    @pl.when(b == 0)
    def _():
        for i in range(DMA_AHEAD):
            @pl.when(i < total)
            def _(i=i):
                tile_copy(i).start()

    def entry(j, slot, owned):
        tile = sa_ref[j] & 0xFFFF
        a = pl.multiple_of(lax.shift_right_logical(sa_ref[j], 16), 16)
        lo = rr_ref[j] & 0xFFFF
        hi = jnp.where(owned, lax.shift_right_logical(rr_ref[j], 16), 0)
        local = a + lax.broadcasted_iota(jnp.int32, (GATHER_ROWS, ROUTE_TILE), 0)
        rows = jnp.where((local >= lo) & (local < hi), local + b * EXPERT_ROWS, -1)
        d = dst_ref[tile]
        sel = (d[0:1, :] == rows) | (d[1:2, :] == rows)
        win = pl.ds(a, GATHER_ROWS)
        picked = _dot(jnp.where(sel, 1.0, 0.0).astype(BF16), buf[slot])
        o_ref[win, :] = o_ref[win, :] + picked.astype(BF16)

    j0 = es_ref[b]
    j1 = es_ref[b + 1]

    def body(k, carry):
        ja = j0 + 2 * k
        jb = ja + 1
        has_b = jb < j1
        tile_copy(ja).wait()

        @pl.when(ja + DMA_AHEAD < total)
        def _():
            tile_copy(ja + DMA_AHEAD).start()

        @pl.when(has_b)
        def _():
            tile_copy(jb).wait()

        @pl.when(has_b & (jb + DMA_AHEAD < total))
        def _():
            tile_copy(jb + DMA_AHEAD).start()

        entry(ja, ja & (DMA_RING - 1), True)
        entry(jb, jnp.where(has_b, jb, ja) & (DMA_RING - 1), has_b)
        return carry

    lax.fori_loop(0, (j1 - j0 + 1) // 2, body, 0)


def _gather(lists, t, dst, n_blocks):
    n_tiles = dst.shape[0]
    spec = pltpu.PrefetchScalarGridSpec(
        num_scalar_prefetch=3,
        grid=(n_blocks,),
        in_specs=[
            pl.BlockSpec((n_tiles, 2, ROUTE_TILE), lambda b, es, sa, rr: (0, 0, 0)),
            pl.BlockSpec(memory_space=pl.ANY),
        ],
        out_specs=pl.BlockSpec((EXPERT_ROWS, D_MODEL), lambda b, es, sa, rr: (b, 0)),
        scratch_shapes=[
            pltpu.VMEM((DMA_RING, ROUTE_TILE, D_MODEL), BF16),
            pltpu.SemaphoreType.DMA((DMA_RING,)),
        ],
    )
    return pl.pallas_call(
        _gather_kernel,
        grid_spec=spec,
        out_shape=jax.ShapeDtypeStruct((n_blocks * EXPERT_ROWS, D_MODEL), BF16),
        compiler_params=_params(("arbitrary",)),
        name="moe_gather",
    )(*lists, dst, t)


def _expert_kernel(be_ref, bv_ref, x_ref, w1_ref, w3_ref, w2_ref, o_ref, h_ref):
    used = bv_ref[pl.program_id(0)] == 1

    @pl.when(used)
    def _():
        _swiglu_hidden(x_ref[...], w1_ref, w3_ref, h_ref)
        o_ref[...] = _dot(h_ref[...], w2_ref[...]).astype(BF16)

    @pl.when(jnp.logical_not(used))
    def _():
        o_ref[...] = jnp.zeros_like(o_ref)


def _experts(block_e, block_used, xs, w1, w3, w2, j):
    n_blocks = block_e.shape[0]
    resident = pl.Buffered(1)
    spec = pltpu.PrefetchScalarGridSpec(
        num_scalar_prefetch=2,
        grid=(n_blocks,),
        in_specs=[
            pl.BlockSpec((EXPERT_ROWS, D_MODEL), lambda b, be, bv: (b, 0)),
            pl.BlockSpec((None, None, D_MODEL, EXPERT_FF), lambda b, be, bv: (j, be[b], 0, 0), pipeline_mode=resident),
            pl.BlockSpec((None, None, D_MODEL, EXPERT_FF), lambda b, be, bv: (j, be[b], 0, 0), pipeline_mode=resident),
            pl.BlockSpec((None, None, EXPERT_FF, D_MODEL), lambda b, be, bv: (j, be[b], 0, 0), pipeline_mode=resident),
        ],
        out_specs=pl.BlockSpec((EXPERT_ROWS, D_MODEL), lambda b, be, bv: (b, 0)),
        scratch_shapes=[pltpu.VMEM((EXPERT_ROWS, EXPERT_FF), BF16)],
    )
    return pl.pallas_call(
        _expert_kernel,
        grid_spec=spec,
        out_shape=jax.ShapeDtypeStruct(xs.shape, BF16),
        compiler_params=_params(("arbitrary",)),
        name="moe_experts",
    )(block_e, block_used, xs, w1, w3, w2)


COMBINE_ROWS = ROUTE_TILE
DW_LANES = 4


def _combine_kernel(r_ref, lo_ref, hi_ref, es_ref, xr_ref, xl_ref, xh_ref, dw_ref, x1_ref, gm_ref, lg_ref, lb_ref,
                    y_hbm, o_ref, buf, sem, xbuf, xsem, acc_ref, d_ref, w_ref):
    s = pl.program_id(0)
    half = s & 1
    for k in range(2):
        d_ref[k] = jnp.broadcast_to(dw_ref[:, k:k + 1].astype(jnp.int32), d_ref.shape[1:])
        w_ref[k] = jnp.broadcast_to(dw_ref[:, 2 + k:3 + k], w_ref.shape[1:])

    def win_copy(tile, e, which):
        start = pl.multiple_of(r_ref[tile * N_EXPERTS + e], 16)
        return pltpu.make_async_copy(y_hbm.at[pl.ds(start, COMBINE_ROWS)], buf.at[which, e], sem.at[which, e])

    @pl.when(s == 0)
    def _():
        for e in range(N_EXPERTS):
            win_copy(0, e, 0).start()

    @pl.when(s + 1 < pl.num_programs(0))
    def _():
        for e in range(N_EXPERTS):
            win_copy(s + 1, e, 1 - half).start()

    def weighted(first, lo, hi, rows):
        d1 = d_ref[0]
        d2 = d_ref[1]
        in1 = (d1 >= lo) & (d1 < hi)
        in2 = (d2 >= lo) & (d2 < hi)
        wsel = jnp.where(in1, w_ref[0], 0.0) + jnp.where(in2, w_ref[1], 0.0)
        hit = jnp.where(in1, d1, jnp.where(in2, d2, -1)) - first
        hit = jnp.concatenate([hit] * (COMBINE_ROWS // 128), axis=1)
        col = lax.broadcasted_iota(jnp.int32, (ROUTE_TILE, COMBINE_ROWS), 1)
        picked = _dot(jnp.where(hit == col, 1.0, 0.0).astype(BF16), rows)
        return jnp.concatenate([wsel] * (D_MODEL // 128), axis=1) * picked

    for e in range(N_EXPERTS):
        win_copy(s, e, half).wait()
    acc = jnp.zeros((ROUTE_TILE, D_MODEL), F32)
    for e in range(N_EXPERTS):
        i = s * N_EXPERTS + e
        first = r_ref[i]
        acc = acc + weighted(first, lo_ref[i], jnp.minimum(hi_ref[i], first + COMBINE_ROWS), buf[half, e])
    acc_ref[...] = acc

    def overflow(j, carry):
        start = pl.multiple_of(xr_ref[j], 16)
        copy = pltpu.make_async_copy(y_hbm.at[pl.ds(start, COMBINE_ROWS)], xbuf, xsem.at[0])
        copy.start()
        copy.wait()
        acc_ref[...] = acc_ref[...] + weighted(xr_ref[j], xl_ref[j], xh_ref[j], xbuf[...])
        return carry

    lax.fori_loop(es_ref[s], es_ref[s + 1], overflow, 0)
    o_ref[...] = _layer_norm(DEEPNORM_ALPHA * x1_ref[...] + gm_ref[...] * acc_ref[...], lg_ref[...], lb_ref[...])


def _combine(lists, y, dw, x1, mod3, ln_g, ln_b, layer, n_lat, seq, batch):
    n = x1.shape[0]
    nl = n_lat // ROUTE_TILE
    tpb = seq // ROUTE_TILE

    def row(s):
        return jnp.where(s < nl, s // tpb, batch)

    spec = pltpu.PrefetchScalarGridSpec(
        num_scalar_prefetch=7,
        grid=(n // ROUTE_TILE,),
        in_specs=[
            pl.BlockSpec((ROUTE_TILE, DW_LANES), lambda s, *_: (s, 0)),
            pl.BlockSpec((ROUTE_TILE, D_MODEL), lambda s, *_: (s, 0)),
            pl.BlockSpec((None, 1, D_MODEL), lambda s, *_: (row(s), 0, 5)),
            pl.BlockSpec((None, 1, D_MODEL), lambda s, *_: (layer, 0, 0)),
            pl.BlockSpec((None, 1, D_MODEL), lambda s, *_: (layer, 0, 0)),
            pl.BlockSpec(memory_space=pl.ANY),
        ],
        out_specs=pl.BlockSpec((ROUTE_TILE, D_MODEL), lambda s, *_: (s, 0)),
        scratch_shapes=[
            pltpu.VMEM((2, N_EXPERTS, COMBINE_ROWS, D_MODEL), BF16),
            pltpu.SemaphoreType.DMA((2, N_EXPERTS)),
            pltpu.VMEM((COMBINE_ROWS, D_MODEL), BF16),
            pltpu.SemaphoreType.DMA((1,)),
            pltpu.VMEM((ROUTE_TILE, D_MODEL), F32),
            pltpu.VMEM((2, ROUTE_TILE, 128), jnp.int32),
            pltpu.VMEM((2, ROUTE_TILE, 128), F32),
        ],
    )
    return pl.pallas_call(
        _combine_kernel,
        grid_spec=spec,
        out_shape=jax.ShapeDtypeStruct((n, D_MODEL), F32),
        compiler_params=_params(("arbitrary",)),
        name="moe_combine",
    )(*lists, dw, x1, mod3, ln_g, ln_b, y)


def _moe(t, x1, mod3, wr, tri, w1, w3, w2, ln_g, ln_b, layer, j, n_lat, seq, batch):
    n = t.shape[0]
    n_tiles = n // ROUTE_TILE
    n_blocks = -(-(2 * n + N_EXPERTS * (EXPERT_ROWS - 1)) // EXPERT_ROWS) + 1
    i32 = jnp.int32

    route, before, total = _router(t, wr, tri)

    e12 = route[:, 0:2].astype(i32)
    rank = route[:, 4:6].astype(i32)
    counts = total[0, :N_EXPERTS].astype(i32)
    padded = (counts + EXPERT_ROWS - 1) // EXPERT_ROWS * EXPERT_ROWS
    pad_end = jnp.cumsum(padded)
    pad_start = pad_end - padded
    dest = pad_start[e12] + rank
    dst = dest.reshape(n_tiles, ROUTE_TILE, 2).transpose(0, 2, 1)
    dw = jnp.concatenate([dest.astype(F32), route[:, 2:4]], axis=1)
    block_start = jnp.arange(n_blocks, dtype=i32) * EXPERT_ROWS
    block_e = jnp.minimum(jnp.sum(block_start[:, None] >= pad_end[None, :], axis=1), N_EXPERTS - 1).astype(i32)
    block_used = (block_start < pad_end[-1]).astype(i32)

    cb = before[:, 0, :N_EXPERTS].astype(i32)
    ca = jnp.concatenate([cb[1:], counts[None, :]], axis=0)
    lo = pad_start[None, :] + cb
    hi = pad_start[None, :] + ca
    some = hi > lo
    tile_id = jnp.arange(n_tiles, dtype=i32)[:, None, None]

    def grouped(ok, group, n_groups, *values):
        key = jnp.where(ok, group, n_groups).reshape(-1)
        order = jnp.argsort(key, stable=True)
        starts = jnp.sum(key[None, :] < jnp.arange(n_groups + 1, dtype=i32)[:, None], axis=1, dtype=i32)
        return (starts,) + tuple(jnp.broadcast_to(v, ok.shape).reshape(-1)[order].astype(i32) for v in values)

    last_off = EXPERT_ROWS - GATHER_ROWS
    b_lo = lo // EXPERT_ROWS
    parts = []
    for blk, p_lo, p_hi in ((b_lo, lo, jnp.minimum(hi, (b_lo + 1) * EXPERT_ROWS)),
                            (b_lo + 1, (b_lo + 1) * EXPERT_ROWS, hi)):
        l_lo = p_lo - blk * EXPERT_ROWS
        l_hi = p_hi - blk * EXPERT_ROWS
        a0 = jnp.minimum(l_lo // 16 * 16, last_off)
        cut = a0 + GATHER_ROWS
        a1 = jnp.minimum(cut, last_off)
        parts.append((blk, a0, l_lo, jnp.minimum(l_hi, cut)))
        parts.append((blk, a1, jnp.maximum(l_lo, cut), l_hi))
    g_block, g_off, g_lo, g_hi = (jnp.stack(v, axis=-1) for v in zip(*parts))
    g_ok = some[:, :, None] & (g_hi > g_lo)
    g_start, g_tile, g_off, g_lo, g_hi = grouped(g_ok, g_block, n_blocks, tile_id, g_off, g_lo, g_hi)
    xs = _gather((g_start, g_tile | (g_off << 16), g_lo | (g_hi << 16)), t, dst, n_blocks)

    y = _experts(block_e, block_used, xs, w1, w3, w2, j)

    r0 = lo // 16 * 16
    over = some & (hi > r0 + COMBINE_ROWS)
    x_start, x_row, x_lo, x_hi = grouped(over, tile_id[:, :, 0], n_tiles, r0 + COMBINE_ROWS, r0 + COMBINE_ROWS, hi)
    lists = (r0.reshape(-1), lo.reshape(-1), hi.reshape(-1), x_start, x_row, x_lo, x_hi)
    return _combine(lists, y, dw, x1, mod3, ln_g, ln_b, layer, n_lat, seq, batch)


def _window_counts(n, w):
    t = np.arange(n)
    left = w // 2
    right = w - 1 - left
    return (np.minimum(t + right + 1, n) - np.maximum(t - left, 0)).astype(np.float32)


def _window_matrix(n, w):
    left = w // 2
    right = w - 1 - left
    t = np.arange(n)
    return ((t[None, :] >= t[:, None] - left) & (t[None, :] <= t[:, None] + right)).astype(np.float32)


def _pool_tables(seq, n_ctx):
    rows = seq // GRID_W
    per_tile = 256 // GRID_W
    mc = np.stack([np.kron(np.eye(per_tile, dtype=np.float32), _window_matrix(GRID_W, w)) for w in POOL_WINDOWS])
    inv = np.stack([1.0 / np.outer(_window_counts(rows, w), _window_counts(GRID_W, w)).reshape(seq)
                    for w in POOL_WINDOWS])
    inv = np.broadcast_to(inv[:, :, None], (4, seq, POOL_GROUP)).astype(np.float32)
    m1d = np.stack([_window_matrix(n_ctx, w) for w in POOL_WINDOWS])
    inv1d = np.stack([1.0 / _window_counts(n_ctx, w) for w in POOL_WINDOWS])
    inv1d = np.broadcast_to(inv1d[:, :, None], (4, n_ctx, POOL_GROUP)).astype(np.float32)
    return (jnp.asarray(mc, BF16), jnp.asarray(inv), jnp.asarray(m1d, BF16), jnp.asarray(inv1d))


def _rope_tables(seq):
    f32 = np.float32
    t = np.arange(seq)
    row = (t // GRID_W).astype(f32)
    col = (t % GRID_W).astype(f32)
    n_freq = RET_QK_DIM // 4
    inv = np.exp(-np.log(f32(ROPE_BASE)) * np.arange(n_freq, dtype=f32) / f32(n_freq)).astype(f32)
    ang = np.concatenate([row[:, None] * inv, col[:, None] * inv], -1).astype(f32)
    cos, sin = np.cos(ang).astype(f32), np.sin(ang).astype(f32)
    cos2 = np.concatenate([cos, cos], -1)
    sin2 = np.concatenate([-sin, sin], -1)
    cos2 = np.concatenate([cos2, np.ones((TOKEN_TILE, RET_QK_DIM), f32)], 0)
    sin2 = np.concatenate([sin2, np.zeros((TOKEN_TILE, RET_QK_DIM), f32)], 0)
    return jnp.asarray(cos2), jnp.asarray(sin2)


def kernel(x, c, ctx, c_ctx, ada_w, ada_b, w_in, pool_w, pool_scale, w_pool_out, w_ret_out, ret_decay_logit,
           w_out, ln_mix_g, ln_mix_b, ln_ffn_g, ln_ffn_b, ffn_w1, ffn_w3, ffn_w2, moe_router, moe_w1, moe_w3,
           moe_w2):
    batch, seq, d = x.shape
    n_ctx = ctx.shape[1]
    n_lat = batch * seq
    depth = ada_w.shape[0]
    assert d == D_MODEL and depth == DEPTH and batch < MOD_ROWS
    assert seq % TOKEN_TILE == 0 and (batch * n_ctx) % TOKEN_TILE == 0 and n_lat % n_ctx == 0
    assert FFN_ROWS % ROUTER_STEP == 0 and ROUTER_STEP % ROUTE_TILE == 0
    assert seq % FFN_ROWS == 0 and (batch * n_ctx) % FFN_ROWS == 0

    s_in = jnp.zeros((MOD_ROWS, d), F32).at[:batch].set(c).at[batch].set(c_ctx)
    mod = _modulation(s_in, ada_w, ada_b)
    cos2, sin2 = _rope_tables(seq)
    mc, inv, m1d, inv1d = _pool_tables(seq, n_ctx)
    tri = jnp.asarray(np.tril(np.ones((ROUTER_STEP, ROUTER_STEP), np.float32), -1), BF16)

    w_in_b = w_in.astype(BF16)
    pool_w_b = pool_w.astype(BF16)
    pool_s = pool_scale.reshape(depth, 1, POOL_WIDTH)
    wp_b, wr_b, wo_b = w_pool_out.astype(BF16), w_ret_out.astype(BF16), w_out.astype(BF16)
    ffn_b = ffn_w1.astype(BF16), ffn_w3.astype(BF16), ffn_w2.astype(BF16)
    moe_b = moe_w1.astype(BF16), moe_w3.astype(BF16), moe_w2.astype(BF16)
    ln_mix = ln_mix_g.reshape(depth, 1, d), ln_mix_b.reshape(depth, 1, d)
    ln_ffn = ln_ffn_g.reshape(depth, 1, d), ln_ffn_b.reshape(depth, 1, d)

    n_tok = n_lat + batch * n_ctx
    x_parts = (x.reshape(n_lat, d), ctx.reshape(batch * n_ctx, d))
    for l in range(depth):
        last = l == depth - 1
        mod3 = mod[l].reshape(MOD_ROWS, 1, 6 * d)
        pu, pq, pk, pv, pg, gates = _in_proj(x_parts, mod3, cos2, sin2, w_in_b, l, n_tok, n_lat, seq, batch)
        yp_lat = _pool_lat(pu, mc, inv, pool_w_b, pool_s, l, batch, seq)
        yp_ctx = _pool_ctx(pu, m1d, inv1d, pool_w_b, pool_s, l, batch, n_lat, n_ctx)
        dl = jnp.broadcast_to(ret_decay_logit[l].astype(F32)[:, :, None, None],
                              (2, RET_HEADS, RET_CHUNK, RET_V_DIM))
        z_lat, z_ctx = _retention(pq, pk, pv, pg, dl, batch, seq, n_ctx)
        n_out = n_lat if last else n_tok
        x1, t = _merge(x_parts, mod3, yp_lat, yp_ctx, z_lat, z_ctx, gates, wp_b, wr_b, wo_b,
                       *ln_mix, l, n_out, n_lat, seq, batch)
        j = l // 2
        if l % 2 == 0:
            xs = _ffn(t, x1, mod3, *ffn_b, *ln_ffn, l, j, n_lat, seq, batch)
        else:
            wr = jnp.zeros((d, ROUTE_LANES), BF16).at[:, :N_EXPERTS].set(moe_router[j].astype(BF16))
            xs = _moe(t, x1, mod3, wr, tri, *moe_b, *ln_ffn, l, j, n_lat, seq, batch)
        x_parts = (xs,)
    return xs[:n_lat].reshape(batch, seq, d)
```

```python
import functools

import jax
import jax.numpy as jnp
import numpy as np
from jax import lax
from jax.experimental import pallas as pl
from jax.experimental.pallas import tpu as pltpu

F32 = jnp.float32
BF16 = jnp.bfloat16

D_MODEL = 1024
DEPTH = 4
GRID_W = 64
POOL_WINDOWS = (2, 4, 8, 16)
POOL_GROUP = 128
POOL_WIDTH = POOL_GROUP * len(POOL_WINDOWS)
RET_HEADS = 4
RET_QK_DIM = 128
RET_V_DIM = 256
RET_CHUNK = 128
ROPE_BASE = 10000.0
OFF_Q = POOL_WIDTH
OFF_K = OFF_Q + RET_HEADS * RET_QK_DIM
OFF_V = OFF_K + RET_HEADS * RET_QK_DIM
OFF_G = OFF_V + RET_HEADS * RET_V_DIM
WIDTH_B = 3 * D_MODEL
D_FF = 2816
N_EXPERTS = 8
EXPERT_FF = 3584
DEEPNORM_ALPHA = (2 * DEPTH) ** 0.25
LN_EPS = 1e-5
K_SCALE = RET_QK_DIM ** -0.5

MOD_ROWS = 24
TOKEN_TILE = 512
ROUTE_TILE = 256
ROUTER_STEP = 512
EXPERT_ROWS = 512
FFN_ROWS = 2 * TOKEN_TILE
VMEM_LIMIT = 56 * 1024 * 1024


def _dot(a, b):
    return jnp.dot(a, b, preferred_element_type=F32)


def _sigmoid(x):
    return 1.0 / (1.0 + jnp.exp(-x))


def _split_bf16(a):
    hi = a.astype(BF16)
    lo = (a - hi.astype(F32)).astype(BF16)
    return hi, lo


def _layer_norm(v, g, b):
    mean = jnp.mean(v, axis=-1, keepdims=True)
    vc = v - mean
    var = jnp.mean(vc * vc, axis=-1, keepdims=True)
    return vc * lax.rsqrt(var + LN_EPS) * g + b


def _params(sem, vmem=VMEM_LIMIT):
    return pltpu.CompilerParams(dimension_semantics=sem, vmem_limit_bytes=vmem)


def _mod_kernel(s_ref, w_ref, b_ref, o_ref):
    s = s_ref[...]
    s = s * _sigmoid(s)
    s_hi, s_lo = _split_bf16(s)
    w_hi, w_lo = _split_bf16(w_ref[...])
    o_ref[...] = _dot(s_hi, w_hi) + (_dot(s_hi, w_lo) + _dot(s_lo, w_hi)) + b_ref[...]


def _modulation(s_in, ada_w, ada_b):
    depth, d, width = ada_w.shape
    tn = 1536
    return pl.pallas_call(
        _mod_kernel,
        grid=(depth, width // tn),
        in_specs=[
            pl.BlockSpec((MOD_ROWS, d), lambda l, j: (0, 0)),
            pl.BlockSpec((None, d, tn), lambda l, j: (l, 0, j)),
            pl.BlockSpec((None, 1, tn), lambda l, j: (l, 0, j)),
        ],
        out_specs=pl.BlockSpec((None, MOD_ROWS, tn), lambda l, j: (l, 0, j)),
        out_shape=jax.ShapeDtypeStruct((depth, MOD_ROWS, width), F32),
        compiler_params=_params(("arbitrary", "arbitrary")),
        name="adaln_mod",
    )(s_in, ada_w, ada_b.reshape(depth, 1, width))


IN_COLS = 512
IN_BLOCKS = (OFF_G + WIDTH_B) // IN_COLS


def _pick_x(x_refs, rows, is_lat):
    if len(x_refs) == 1:
        return x_refs[0][rows, :]
    return jnp.where(is_lat, x_refs[0][rows, :], x_refs[1][rows, :])


def _in_kernel(*refs, n_lat_tiles, n_x):
    x_refs = refs[:n_x]
    sh_ref, sc_ref, cos_ref, sin_ref = refs[n_x:n_x + 4]
    w_refs = refs[n_x + 4:n_x + 4 + IN_BLOCKS]
    pa_ref, pb_ref = refs[n_x + 4 + IN_BLOCKS:]
    x = _pick_x(x_refs, slice(None), pl.program_id(0) < n_lat_tiles)
    h = (x * (1.0 + sc_ref[...]) + sh_ref[...]).astype(BF16)
    cos = cos_ref[...]
    sin = sin_ref[...]
    for blk in range(IN_BLOCKS):
        c0 = blk * IN_COLS
        acc = _dot(h, w_refs[blk][...])
        if c0 in (OFF_Q, OFF_K):
            for hh in range(RET_HEADS):
                t = acc[:, hh * RET_QK_DIM:(hh + 1) * RET_QK_DIM]
                r = t * cos + pltpu.roll(t, RET_QK_DIM // 2, 1) * sin
                if c0 == OFF_K:
                    r = r * K_SCALE
                pa_ref[:, c0 + hh * RET_QK_DIM:c0 + (hh + 1) * RET_QK_DIM] = r.astype(BF16)
        elif c0 < OFF_G:
            pa_ref[:, c0:c0 + IN_COLS] = acc.astype(BF16)
        else:
            pb_ref[:, c0 - OFF_G:c0 - OFF_G + IN_COLS] = acc.astype(BF16)


def _tile_maps(n_lat, seq, batch, tile):
    nl = n_lat // tile
    tpb = seq // tile

    def row(i):
        return jnp.where(i < nl, i // tpb, batch)

    def lat(i):
        return jnp.minimum(i, nl - 1)

    def ctx(i):
        return jnp.maximum(i - nl, 0)

    return nl, tpb, row, lat, ctx


def _x_operands(x_parts, n_lat, seq, batch, tile):
    _, _, _, lat, ctx = _tile_maps(n_lat, seq, batch, tile)
    if len(x_parts) == 1:
        return [pl.BlockSpec((tile, D_MODEL), lambda i: (i, 0))], list(x_parts)
    return ([pl.BlockSpec((tile, D_MODEL), lambda i: (lat(i), 0)),
             pl.BlockSpec((tile, D_MODEL), lambda i: (ctx(i), 0))], list(x_parts))


def _in_proj(x_parts, mod3, cos2, sin2, w_in, layer, n_tok, n_lat, seq, batch):
    tm = TOKEN_TILE
    nl, tpb, row, _, _ = _tile_maps(n_lat, seq, batch, tm)
    x_specs, x_args = _x_operands(x_parts, n_lat, seq, batch, tm)

    def rope(i):
        return jnp.where(i < nl, i % tpb, tpb)

    def w_spec(blk):
        return pl.BlockSpec((None, D_MODEL, IN_COLS), lambda i: (layer, 0, blk))

    return pl.pallas_call(
        functools.partial(_in_kernel, n_lat_tiles=nl, n_x=len(x_args)),
        grid=(n_tok // tm,),
        in_specs=x_specs + [
            pl.BlockSpec((None, 1, D_MODEL), lambda i: (row(i), 0, 0)),
            pl.BlockSpec((None, 1, D_MODEL), lambda i: (row(i), 0, 1)),
            pl.BlockSpec((tm, RET_QK_DIM), lambda i: (rope(i), 0)),
            pl.BlockSpec((tm, RET_QK_DIM), lambda i: (rope(i), 0)),
        ] + [w_spec(blk) for blk in range(IN_BLOCKS)],
        out_specs=[
            pl.BlockSpec((tm, OFF_G), lambda i: (i, 0)),
            pl.BlockSpec((tm, WIDTH_B), lambda i: (i, 0)),
        ],
        out_shape=[
            jax.ShapeDtypeStruct((n_tok, OFF_G), BF16),
            jax.ShapeDtypeStruct((n_tok, WIDTH_B), BF16),
        ],
        compiler_params=_params(("arbitrary",)),
        name="in_proj",
    )(*x_args, mod3, mod3, cos2, sin2, *([w_in] * IN_BLOCKS))


POOL_PAD_ROWS = 8
POOL_ROW_CHUNK = 8


def _pool_kernel(u_ref, mc_ref, inv_ref, pw_ref, ps_ref, o_ref, zp_ref, *, rows):
    seq = rows * GRID_W
    pad = POOL_PAD_ROWS * GRID_W
    chunk = POOL_ROW_CHUNK * GRID_W
    zeros = jnp.zeros((pad, POOL_GROUP), F32)
    for g, w in enumerate(POOL_WINDOWS):
        lanes = slice(g * POOL_GROUP, (g + 1) * POOL_GROUP)
        zp_ref[0:pad, :] = zeros
        zp_ref[pad + seq:pad + seq + pad, :] = zeros
        for c in range(seq // 256):
            zp_ref[pad + c * 256:pad + (c + 1) * 256, :] = _dot(mc_ref[g], u_ref[c * 256:(c + 1) * 256, lanes])
        left = w // 2

        def body(rc, carry, g=g, w=w, left=left, lanes=lanes):
            tok = pl.multiple_of(rc * chunk, chunk)
            acc = zp_ref[pl.ds(tok + pad - left * GRID_W, chunk), :]
            for k in range(1, w):
                acc = acc + zp_ref[pl.ds(tok + pad + (k - left) * GRID_W, chunk), :]
            pooled = acc * inv_ref[g, pl.ds(tok, chunk), :]
            d = (pooled - u_ref[pl.ds(tok, chunk), lanes].astype(F32)).astype(BF16)
            y = _dot(d, pw_ref[g]) * ps_ref[:, lanes]
            o_ref[pl.ds(tok, chunk), lanes] = y.astype(BF16)
            return carry

        lax.fori_loop(0, rows // POOL_ROW_CHUNK, body, 0, unroll=4)


def _pool_lat(pa, mc, inv, pw, ps, layer, batch, seq):
    rows = seq // GRID_W
    return pl.pallas_call(
        functools.partial(_pool_kernel, rows=rows),
        grid=(batch,),
        in_specs=[
            pl.BlockSpec((seq, POOL_WIDTH), lambda b: (b, 0)),
            pl.BlockSpec((4, 256, 256), lambda b: (0, 0, 0)),
            pl.BlockSpec((4, seq, POOL_GROUP), lambda b: (0, 0, 0)),
            pl.BlockSpec((None, 4, POOL_GROUP, POOL_GROUP), lambda b: (layer, 0, 0, 0)),
            pl.BlockSpec((None, 1, POOL_WIDTH), lambda b: (layer, 0, 0)),
        ],
        out_specs=pl.BlockSpec((seq, POOL_WIDTH), lambda b: (b, 0)),
        out_shape=jax.ShapeDtypeStruct((batch * seq, POOL_WIDTH), BF16),
        scratch_shapes=[pltpu.VMEM(((rows + 2 * POOL_PAD_ROWS) * GRID_W, POOL_GROUP), F32)],
        compiler_params=_params(("arbitrary",)),
        name="pool_lat",
    )(pa, mc, inv, pw, ps)


def _pool_ctx_kernel(u_ref, m_ref, inv_ref, pw_ref, ps_ref, o_ref):
    for g in range(len(POOL_WINDOWS)):
        lanes = slice(g * POOL_GROUP, (g + 1) * POOL_GROUP)
        ug = u_ref[:, lanes]
        pooled = _dot(m_ref[g], ug) * inv_ref[g]
        d = (pooled - ug.astype(F32)).astype(BF16)
        o_ref[:, lanes] = (_dot(d, pw_ref[g]) * ps_ref[:, lanes]).astype(BF16)


def _pool_ctx(pa, m1d, inv1d, pw, ps, layer, batch, n_lat, n_ctx):
    first = n_lat // n_ctx
    return pl.pallas_call(
        _pool_ctx_kernel,
        grid=(batch,),
        in_specs=[
            pl.BlockSpec((n_ctx, POOL_WIDTH), lambda b: (first + b, 0)),
            pl.BlockSpec((4, n_ctx, n_ctx), lambda b: (0, 0, 0)),
            pl.BlockSpec((4, n_ctx, POOL_GROUP), lambda b: (0, 0, 0)),
            pl.BlockSpec((None, 4, POOL_GROUP, POOL_GROUP), lambda b: (layer, 0, 0, 0)),
            pl.BlockSpec((None, 1, POOL_WIDTH), lambda b: (layer, 0, 0)),
        ],
        out_specs=pl.BlockSpec((n_ctx, POOL_WIDTH), lambda b: (b, 0)),
        out_shape=jax.ShapeDtypeStruct((batch * n_ctx, POOL_WIDTH), BF16),
        compiler_params=_params(("arbitrary",)),
        name="pool_ctx",
    )(pa, m1d, inv1d, pw, ps)


def _log_sigmoid(x):
    return jnp.minimum(x, 0.0) - jnp.log1p(jnp.exp(-jnp.abs(x)))


def _decays(dl, backward):
    c = RET_CHUNK
    lg = _log_sigmoid(dl)
    lgq = lg[:, :RET_QK_DIM]
    ii = lax.broadcasted_iota(jnp.int32, (c, c), 0)
    jj = lax.broadcasted_iota(jnp.int32, (c, c), 1)
    pos = lax.broadcasted_iota(jnp.int32, (c, RET_QK_DIM), 0).astype(F32)
    if backward:
        diff = (jj - ii).astype(F32)
        qdec = jnp.exp(lgq * (c - pos))
        kdec = jnp.exp(lgq * pos)
    else:
        diff = (ii - jj).astype(F32)
        qdec = jnp.exp(lgq * (pos + 1.0))
        kdec = jnp.exp(lgq * (c - 1.0 - pos))
    inner = jnp.where(diff >= 0, jnp.exp(lgq * jnp.maximum(diff, 0.0)), 0.0)
    cdec = jnp.exp(lg * float(c))
    return inner, qdec, kdec, cdec


def _chunk_kv(kc, vc, kdec):
    kd = (kc.astype(F32) * kdec).astype(BF16)
    return lax.dot_general(kd, vc, (((0,), (0,)), ((), ())), preferred_element_type=F32)


def _chunk_out(qc, kc, vc, gc, states, mask, qdec2):
    scores = lax.dot_general(qc, kc, (((1,), (1,)), ((), ())), preferred_element_type=F32) * mask
    qf = qc.astype(F32)
    qd = (jnp.concatenate([qf, qf], axis=1) * qdec2).astype(BF16)
    o = _dot(scores.astype(BF16), vc) + _dot(qd, states)
    mean = jnp.mean(o, axis=-1, keepdims=True)
    oc = o - mean
    var = jnp.mean(oc * oc, axis=-1, keepdims=True)
    on = oc * lax.rsqrt(var + LN_EPS)
    gf = gc.astype(F32)
    return (gf * _sigmoid(gf) * on).astype(BF16)


def _ret_kernel(q_ref, k_ref, v_ref, g_ref, qc_ref, kc_ref, vc_ref, gc_ref, dl_ref,
                z_ref, zc_ref, st_ref, stc_ref, run_ref, *, n_chunks, n_ctx_chunks):
    c = RET_CHUNK
    dk = RET_QK_DIM
    inner_f, qdec_f, kdec_f, cdec_f = _decays(dl_ref[0], False)
    inner_b, qdec_b, kdec_b, cdec_b = _decays(dl_ref[1], True)
    mask = inner_f + inner_b
    qdec2 = jnp.concatenate([qdec_f, qdec_b], axis=1)

    def rows(j):
        return slice(j * c, (j + 1) * c)

    s = jnp.zeros((dk, RET_V_DIM), F32)
    for j in range(n_ctx_chunks):
        stc_ref[j, 0:dk, :] = s.astype(BF16)
        s = s * cdec_f + _chunk_kv(kc_ref[rows(j), :], vc_ref[rows(j), :], kdec_f)
    run_ref[0] = s
    s = jnp.zeros((dk, RET_V_DIM), F32)
    for j in reversed(range(n_ctx_chunks)):
        stc_ref[j, dk:2 * dk, :] = s.astype(BF16)
        s = s * cdec_b + _chunk_kv(kc_ref[rows(j), :], vc_ref[rows(j), :], kdec_b)
    run_ref[1] = s
    for j in range(n_ctx_chunks):
        zc_ref[rows(j), :] = _chunk_out(qc_ref[rows(j), :], kc_ref[rows(j), :], vc_ref[rows(j), :],
                                        gc_ref[rows(j), :], stc_ref[j], mask, qdec2)

    def scan(t, carry):
        rf = pl.ds(pl.multiple_of(t * c, c), c)
        sf = run_ref[0]
        st_ref[t, 0:dk, :] = sf.astype(BF16)
        run_ref[0] = sf * cdec_f + _chunk_kv(k_ref[rf, :], v_ref[rf, :], kdec_f)
        tb = n_chunks - 1 - t
        rb = pl.ds(pl.multiple_of(tb * c, c), c)
        sb = run_ref[1]
        st_ref[tb, dk:2 * dk, :] = sb.astype(BF16)
        run_ref[1] = sb * cdec_b + _chunk_kv(k_ref[rb, :], v_ref[rb, :], kdec_b)
        return carry

    lax.fori_loop(0, n_chunks, scan, 0, unroll=4)

    def emit(t, carry):
        r = pl.ds(pl.multiple_of(t * c, c), c)
        z_ref[r, :] = _chunk_out(q_ref[r, :], k_ref[r, :], v_ref[r, :], g_ref[r, :], st_ref[t], mask, qdec2)
        return carry

    lax.fori_loop(0, n_chunks, emit, 0, unroll=8)


def _retention(pa, pb, dl, batch, seq, n_ctx):
    n_lat = batch * seq
    first = n_lat // n_ctx
    qk0 = OFF_Q // RET_QK_DIM
    kk0 = OFF_K // RET_QK_DIM
    v0 = OFF_V // RET_V_DIM
    return pl.pallas_call(
        functools.partial(_ret_kernel, n_chunks=seq // RET_CHUNK, n_ctx_chunks=n_ctx // RET_CHUNK),
        grid=(batch, RET_HEADS),
        in_specs=[
            pl.BlockSpec((seq, RET_QK_DIM), lambda b, h: (b, qk0 + h)),
            pl.BlockSpec((seq, RET_QK_DIM), lambda b, h: (b, kk0 + h)),
            pl.BlockSpec((seq, RET_V_DIM), lambda b, h: (b, v0 + h)),
            pl.BlockSpec((seq, RET_V_DIM), lambda b, h: (b, h)),
            pl.BlockSpec((n_ctx, RET_QK_DIM), lambda b, h: (first + b, qk0 + h)),
            pl.BlockSpec((n_ctx, RET_QK_DIM), lambda b, h: (first + b, kk0 + h)),
            pl.BlockSpec((n_ctx, RET_V_DIM), lambda b, h: (first + b, v0 + h)),
            pl.BlockSpec((n_ctx, RET_V_DIM), lambda b, h: (first + b, h)),
            pl.BlockSpec((2, None, RET_CHUNK, RET_V_DIM), lambda b, h: (0, h, 0, 0)),
        ],
        out_specs=[
            pl.BlockSpec((seq, RET_V_DIM), lambda b, h: (b, h)),
            pl.BlockSpec((n_ctx, RET_V_DIM), lambda b, h: (b, h)),
        ],
        out_shape=[
            jax.ShapeDtypeStruct((n_lat, RET_HEADS * RET_V_DIM), BF16),
            jax.ShapeDtypeStruct((batch * n_ctx, RET_HEADS * RET_V_DIM), BF16),
        ],
        scratch_shapes=[
            pltpu.VMEM((seq // RET_CHUNK, 2 * RET_QK_DIM, RET_V_DIM), BF16),
            pltpu.VMEM((n_ctx // RET_CHUNK, 2 * RET_QK_DIM, RET_V_DIM), BF16),
            pltpu.VMEM((2, RET_QK_DIM, RET_V_DIM), F32),
        ],
        compiler_params=_params(("arbitrary", "arbitrary")),
        name="retention",
    )(pa, pa, pa, pb, pa, pa, pa, pb, dl)


MIX_COLS = 256


def _merge_kernel(*refs, n_lat_tiles, n_x):
    x_refs = refs[:n_x]
    (gm_ref, sh_ref, sc_ref, ypl_ref, ypc_ref, zl_ref, zc_ref, gp_ref, gr_ref,
     wp_ref, wr_ref, wo_ref, lg_ref, lb_ref, x1_ref, t_ref, mix_ref) = refs[n_x:]
    is_lat = pl.program_id(0) < n_lat_tiles
    yp = jnp.where(is_lat, ypl_ref[...], ypc_ref[...])
    z = jnp.where(is_lat, zl_ref[...], zc_ref[...])
    x = _pick_x(x_refs, slice(None), is_lat)
    for c0 in range(0, D_MODEL, MIX_COLS):
        cols = slice(c0, c0 + MIX_COLS)
        y_pool = _dot(yp, wp_ref[:, cols])
        y_ret = _dot(z, wr_ref[:, cols])
        mix = _sigmoid(gp_ref[:, cols].astype(F32)) * y_pool + _sigmoid(gr_ref[:, cols].astype(F32)) * y_ret
        mix_ref[:, cols] = mix.astype(BF16)
    y = _dot(mix_ref[...], wo_ref[...])
    x1 = _layer_norm(DEEPNORM_ALPHA * x + gm_ref[...] * y, lg_ref[...], lb_ref[...])
    x1_ref[...] = x1
    t_ref[...] = (x1 * (1.0 + sc_ref[...]) + sh_ref[...]).astype(BF16)


def _merge(x_parts, mod3, yp_lat, yp_ctx, z_lat, z_ctx, pb, wp, wr, wo, ln_g, ln_b, layer,
           n_out, n_lat, seq, batch):
    tm = TOKEN_TILE
    nl, _, row, lat, ctx = _tile_maps(n_lat, seq, batch, tm)
    x_specs, x_args = _x_operands(x_parts, n_lat, seq, batch, tm)

    return pl.pallas_call(
        functools.partial(_merge_kernel, n_lat_tiles=nl, n_x=len(x_args)),
        grid=(n_out // tm,),
        in_specs=x_specs + [
            pl.BlockSpec((None, 1, D_MODEL), lambda i: (row(i), 0, 2)),
            pl.BlockSpec((None, 1, D_MODEL), lambda i: (row(i), 0, 3)),
            pl.BlockSpec((None, 1, D_MODEL), lambda i: (row(i), 0, 4)),
            pl.BlockSpec((tm, POOL_WIDTH), lambda i: (lat(i), 0)),
            pl.BlockSpec((tm, POOL_WIDTH), lambda i: (ctx(i), 0)),
            pl.BlockSpec((tm, D_MODEL), lambda i: (lat(i), 0)),
            pl.BlockSpec((tm, D_MODEL), lambda i: (ctx(i), 0)),
            pl.BlockSpec((tm, D_MODEL), lambda i: (i, 1)),
            pl.BlockSpec((tm, D_MODEL), lambda i: (i, 2)),
            pl.BlockSpec((None, POOL_WIDTH, D_MODEL), lambda i: (layer, 0, 0)),
            pl.BlockSpec((None, D_MODEL, D_MODEL), lambda i: (layer, 0, 0)),
            pl.BlockSpec((None, D_MODEL, D_MODEL), lambda i: (layer, 0, 0)),
            pl.BlockSpec((None, 1, D_MODEL), lambda i: (layer, 0, 0)),
            pl.BlockSpec((None, 1, D_MODEL), lambda i: (layer, 0, 0)),
        ],
        out_specs=[
            pl.BlockSpec((tm, D_MODEL), lambda i: (i, 0)),
            pl.BlockSpec((tm, D_MODEL), lambda i: (i, 0)),
        ],
        out_shape=[
            jax.ShapeDtypeStruct((n_out, D_MODEL), F32),
            jax.ShapeDtypeStruct((n_out, D_MODEL), BF16),
        ],
        scratch_shapes=[pltpu.VMEM((tm, D_MODEL), BF16)],
        compiler_params=_params(("arbitrary",)),
        name="merge",
    )(*x_args, mod3, mod3, mod3, yp_lat, yp_ctx, z_lat, z_ctx, pb, pb, wp, wr, wo, ln_g, ln_b)


SWIGLU_COLS = 256


def _swiglu_hidden(t, w1_ref, w3_ref, h_ref):
    width = h_ref.shape[1]
    for c0 in range(0, width, SWIGLU_COLS):
        cols = slice(c0, min(c0 + SWIGLU_COLS, width))
        a = _dot(t, w1_ref[:, cols])
        h_ref[:, cols] = (a * _sigmoid(a) * _dot(t, w3_ref[:, cols])).astype(BF16)


def _ffn_kernel(t_ref, x1_ref, gm_ref, w1_ref, w3_ref, w2_ref, lg_ref, lb_ref, o_ref, h_ref):
    for r in range(FFN_ROWS // TOKEN_TILE):
        rows = slice(r * TOKEN_TILE, (r + 1) * TOKEN_TILE)
        _swiglu_hidden(t_ref[rows, :], w1_ref, w3_ref, h_ref.at[r])
        f = _dot(h_ref[r], w2_ref[...])
        o_ref[rows, :] = _layer_norm(DEEPNORM_ALPHA * x1_ref[rows, :] + gm_ref[...] * f, lg_ref[...], lb_ref[...])


def _ffn(t, x1, mod3, w1, w3, w2, ln_g, ln_b, layer, j, n_lat, seq, batch):
    n = t.shape[0]
    tm = FFN_ROWS
    _, _, row, _, _ = _tile_maps(n_lat, seq, batch, tm)
    resident = pl.Buffered(1)

    return pl.pallas_call(
        _ffn_kernel,
        grid=(n // tm,),
        in_specs=[
            pl.BlockSpec((tm, D_MODEL), lambda i: (i, 0)),
            pl.BlockSpec((tm, D_MODEL), lambda i: (i, 0)),
            pl.BlockSpec((None, 1, D_MODEL), lambda i: (row(i), 0, 5)),
            pl.BlockSpec((None, D_MODEL, D_FF), lambda i: (j, 0, 0), pipeline_mode=resident),
            pl.BlockSpec((None, D_MODEL, D_FF), lambda i: (j, 0, 0), pipeline_mode=resident),
            pl.BlockSpec((None, D_FF, D_MODEL), lambda i: (j, 0, 0), pipeline_mode=resident),
            pl.BlockSpec((None, 1, D_MODEL), lambda i: (layer, 0, 0)),
            pl.BlockSpec((None, 1, D_MODEL), lambda i: (layer, 0, 0)),
        ],
        out_specs=pl.BlockSpec((tm, D_MODEL), lambda i: (i, 0)),
        out_shape=jax.ShapeDtypeStruct((n, D_MODEL), F32),
        scratch_shapes=[pltpu.VMEM((tm // TOKEN_TILE, TOKEN_TILE, D_FF), BF16)],
        compiler_params=_params(("arbitrary",)),
        name="ffn_dense",
    )(t, x1, mod3, w1, w3, w2, ln_g, ln_b)


ROUTE_LANES = 128


def _router_kernel(t_ref, wr_ref, tri_ref, route_ref, before_ref, total_ref, run_ref):
    @pl.when(pl.program_id(0) == 0)
    def _():
        run_ref[...] = jnp.zeros_like(run_ref)

    tt = t_ref.shape[0]
    logits = _dot(t_ref[...], wr_ref[...])
    lane = lax.broadcasted_iota(jnp.int32, (tt, ROUTE_LANES), 1)
    neg = jnp.float32(-jnp.inf)
    lg = jnp.where(lane < N_EXPERTS, logits, neg)
    m1 = jnp.max(lg, axis=1, keepdims=True)
    i1 = jnp.min(jnp.where(lg == m1, lane, ROUTE_LANES), axis=1, keepdims=True)
    lg2 = jnp.where(lane == i1, neg, lg)
    m2 = jnp.max(lg2, axis=1, keepdims=True)
    i2 = jnp.min(jnp.where(lg2 == m2, lane, ROUTE_LANES), axis=1, keepdims=True)
    e = jnp.exp(m2 - m1)
    w1 = 1.0 / (1.0 + e)
    w2 = e / (1.0 + e)
    hit1 = lane == i1
    hit2 = lane == i2
    onehot = jnp.where(hit1 | hit2, 1.0, 0.0)
    run = run_ref[...]
    prefix = _dot(tri_ref[...], onehot.astype(BF16)) + run
    r1 = jnp.sum(jnp.where(hit1, prefix, 0.0), axis=1, keepdims=True)
    r2 = jnp.sum(jnp.where(hit2, prefix, 0.0), axis=1, keepdims=True)
    for k in range(tt // ROUTE_TILE):
        before_ref[k] = prefix[k * ROUTE_TILE:k * ROUTE_TILE + 1, :]
    run = run + jnp.sum(onehot, axis=0, keepdims=True)
    run_ref[...] = run
    total_ref[...] = run
    out = jnp.where(lane == 0, i1.astype(F32), 0.0)
    out = jnp.where(lane == 1, i2.astype(F32), out)
    out = jnp.where(lane == 2, w1, out)
    out = jnp.where(lane == 3, w2, out)
    out = jnp.where(lane == 4, r1, out)
    out = jnp.where(lane == 5, r2, out)
    route_ref[...] = out


def _router(t, wr, tri):
    n = t.shape[0]
    tt = ROUTER_STEP
    sub = tt // ROUTE_TILE
    return pl.pallas_call(
        _router_kernel,
        grid=(n // tt,),
        in_specs=[
            pl.BlockSpec((tt, D_MODEL), lambda i: (i, 0)),
            pl.BlockSpec((D_MODEL, ROUTE_LANES), lambda i: (0, 0)),
            pl.BlockSpec((tt, tt), lambda i: (0, 0)),
        ],
        out_specs=[
            pl.BlockSpec((tt, ROUTE_LANES), lambda i: (i, 0)),
            pl.BlockSpec((sub, 1, ROUTE_LANES), lambda i: (i, 0, 0)),
            pl.BlockSpec((1, ROUTE_LANES), lambda i: (0, 0)),
        ],
        out_shape=[
            jax.ShapeDtypeStruct((n, ROUTE_LANES), F32),
            jax.ShapeDtypeStruct((n // ROUTE_TILE, 1, ROUTE_LANES), F32),
            jax.ShapeDtypeStruct((1, ROUTE_LANES), F32),
        ],
        scratch_shapes=[pltpu.VMEM((1, ROUTE_LANES), F32)],
        compiler_params=_params(("arbitrary",)),
        name="moe_router",
    )(t, wr, tri)


DMA_RING = 16
DMA_AHEAD = 8
GATHER_ROWS = 144
GATHER_SPLIT = 4


def _gather_kernel(es_ref, sa_ref, rr_ref, dst_ref, t_hbm, o_ref, buf, sem):
    b = pl.program_id(0)
    total = es_ref[pl.num_programs(0)]
    o_ref[...] = jnp.zeros_like(o_ref)

    def tile_copy(j):
        slot = j & (DMA_RING - 1)
        start = pl.multiple_of((sa_ref[j] & 0xFFFF) * ROUTE_TILE, ROUTE_TILE)
        return pltpu.make_async_copy(t_hbm.at[pl.ds(start, ROUTE_TILE)], buf.at[slot], sem.at[slot])

    @pl.when(b == 0)
    def _():
        for i in range(DMA_AHEAD):
            @pl.when(i < total)
            def _(i=i):
                tile_copy(i).start()

    def entry(j, slot, owned):
        tile = sa_ref[j] & 0xFFFF
        a = pl.multiple_of(lax.shift_right_logical(sa_ref[j], 16), 16)
        lo = rr_ref[j] & 0xFFFF
        hi = jnp.where(owned, lax.shift_right_logical(rr_ref[j], 16), 0)
        local = a + lax.broadcasted_iota(jnp.int32, (GATHER_ROWS, ROUTE_TILE), 0)
        rows = jnp.where((local >= lo) & (local < hi), local + b * EXPERT_ROWS, -1)
        d = dst_ref[tile]
        sel = (d[0:1, :] == rows) | (d[1:2, :] == rows)
        win = pl.ds(a, GATHER_ROWS)
        picked = _dot(jnp.where(sel, 1.0, 0.0).astype(BF16), buf[slot])
        o_ref[win, :] = o_ref[win, :] + picked.astype(BF16)

    j0 = es_ref[b]
    j1 = es_ref[b + 1]

    def body(k, carry):
        ja = j0 + 2 * k
        jb = ja + 1
        has_b = jb < j1
        tile_copy(ja).wait()

        @pl.when(ja + DMA_AHEAD < total)
        def _():
            tile_copy(ja + DMA_AHEAD).start()

        @pl.when(has_b)
        def _():
            tile_copy(jb).wait()

        @pl.when(has_b & (jb + DMA_AHEAD < total))
        def _():
            tile_copy(jb + DMA_AHEAD).start()

        entry(ja, ja & (DMA_RING - 1), True)
        entry(jb, jnp.where(has_b, jb, ja) & (DMA_RING - 1), has_b)
        return carry

    lax.fori_loop(0, (j1 - j0 + 1) // 2, body, 0)


def _gather(lists, t, dst, n_blocks):
    n_tiles = dst.shape[0]
    spec = pltpu.PrefetchScalarGridSpec(
        num_scalar_prefetch=3,
        grid=(n_blocks,),
        in_specs=[
            pl.BlockSpec((n_tiles, 2, ROUTE_TILE), lambda b, es, sa, rr: (0, 0, 0)),
            pl.BlockSpec(memory_space=pl.ANY),
        ],
        out_specs=pl.BlockSpec((EXPERT_ROWS, D_MODEL), lambda b, es, sa, rr: (b, 0)),
        scratch_shapes=[
            pltpu.VMEM((DMA_RING, ROUTE_TILE, D_MODEL), BF16),
            pltpu.SemaphoreType.DMA((DMA_RING,)),
        ],
    )
    return pl.pallas_call(
        _gather_kernel,
        grid_spec=spec,
        out_shape=jax.ShapeDtypeStruct((n_blocks * EXPERT_ROWS, D_MODEL), BF16),
        compiler_params=_params(("arbitrary",)),
        name="moe_gather",
    )(*lists, dst, t)


def _expert_kernel(be_ref, bv_ref, x_ref, w1_ref, w3_ref, w2_ref, o_ref, h_ref):
    used = bv_ref[pl.program_id(0)] == 1

    @pl.when(used)
    def _():
        _swiglu_hidden(x_ref[...], w1_ref, w3_ref, h_ref)
        o_ref[...] = _dot(h_ref[...], w2_ref[...]).astype(BF16)

    @pl.when(jnp.logical_not(used))
    def _():
        o_ref[...] = jnp.zeros_like(o_ref)


def _experts(block_e, block_used, xs, w1, w3, w2, j):
    n_blocks = block_e.shape[0]
    resident = pl.Buffered(1)
    spec = pltpu.PrefetchScalarGridSpec(
        num_scalar_prefetch=2,
        grid=(n_blocks,),
        in_specs=[
            pl.BlockSpec((EXPERT_ROWS, D_MODEL), lambda b, be, bv: (b, 0)),
            pl.BlockSpec((None, None, D_MODEL, EXPERT_FF), lambda b, be, bv: (j, be[b], 0, 0), pipeline_mode=resident),
            pl.BlockSpec((None, None, D_MODEL, EXPERT_FF), lambda b, be, bv: (j, be[b], 0, 0), pipeline_mode=resident),
            pl.BlockSpec((None, None, EXPERT_FF, D_MODEL), lambda b, be, bv: (j, be[b], 0, 0), pipeline_mode=resident),
        ],
        out_specs=pl.BlockSpec((EXPERT_ROWS, D_MODEL), lambda b, be, bv: (b, 0)),
        scratch_shapes=[pltpu.VMEM((EXPERT_ROWS, EXPERT_FF), BF16)],
    )
    return pl.pallas_call(
        _expert_kernel,
        grid_spec=spec,
        out_shape=jax.ShapeDtypeStruct(xs.shape, BF16),
        compiler_params=_params(("arbitrary",)),
        name="moe_experts",
    )(block_e, block_used, xs, w1, w3, w2)


COMBINE_ROWS = ROUTE_TILE


def _combine_kernel(r_ref, lo_ref, hi_ref, es_ref, xr_ref, xl_ref, xh_ref, dest_ref, route_ref, x1_ref, gm_ref, lg_ref, lb_ref,
                    y_hbm, o_ref, buf, sem, xbuf, xsem, acc_ref, d_ref, w_ref):
    s = pl.program_id(0)
    half = s & 1
    for k in range(2):
        d_ref[k] = jnp.broadcast_to(dest_ref[:, k:k + 1], d_ref.shape[1:])
        w_ref[k] = jnp.broadcast_to(route_ref[:, 2 + k:3 + k], w_ref.shape[1:])

    def win_copy(tile, e, which):
        start = pl.multiple_of(r_ref[tile * N_EXPERTS + e], 16)
        return pltpu.make_async_copy(y_hbm.at[pl.ds(start, COMBINE_ROWS)], buf.at[which, e], sem.at[which, e])

    @pl.when(s == 0)
    def _():
        for e in range(N_EXPERTS):
            win_copy(0, e, 0).start()

    @pl.when(s + 1 < pl.num_programs(0))
    def _():
        for e in range(N_EXPERTS):
            win_copy(s + 1, e, 1 - half).start()

    def weighted(first, lo, hi, rows):
        d1 = d_ref[0]
        d2 = d_ref[1]
        in1 = (d1 >= lo) & (d1 < hi)
        in2 = (d2 >= lo) & (d2 < hi)
        wsel = jnp.where(in1, w_ref[0], 0.0) + jnp.where(in2, w_ref[1], 0.0)
        hit = jnp.where(in1, d1, jnp.where(in2, d2, -1)) - first
        hit = jnp.concatenate([hit] * (COMBINE_ROWS // 128), axis=1)
        col = lax.broadcasted_iota(jnp.int32, (ROUTE_TILE, COMBINE_ROWS), 1)
        picked = _dot(jnp.where(hit == col, 1.0, 0.0).astype(BF16), rows)
        return jnp.concatenate([wsel] * (D_MODEL // 128), axis=1) * picked

    for e in range(N_EXPERTS):
        win_copy(s, e, half).wait()
    acc = jnp.zeros((ROUTE_TILE, D_MODEL), F32)
    for e in range(N_EXPERTS):
        i = s * N_EXPERTS + e
        first = r_ref[i]
        acc = acc + weighted(first, lo_ref[i], jnp.minimum(hi_ref[i], first + COMBINE_ROWS), buf[half, e])
    acc_ref[...] = acc

    def overflow(j, carry):
        start = pl.multiple_of(xr_ref[j], 16)
        copy = pltpu.make_async_copy(y_hbm.at[pl.ds(start, COMBINE_ROWS)], xbuf, xsem.at[0])
        copy.start()
        copy.wait()
        acc_ref[...] = acc_ref[...] + weighted(xr_ref[j], xl_ref[j], xh_ref[j], xbuf[...])
        return carry

    lax.fori_loop(es_ref[s], es_ref[s + 1], overflow, 0)
    o_ref[...] = _layer_norm(DEEPNORM_ALPHA * x1_ref[...] + gm_ref[...] * acc_ref[...], lg_ref[...], lb_ref[...])


def _combine(lists, y, dest, route, x1, mod3, ln_g, ln_b, layer, n_lat, seq, batch):
    n = x1.shape[0]
    nl = n_lat // ROUTE_TILE
    tpb = seq // ROUTE_TILE

    def row(s):
        return jnp.where(s < nl, s // tpb, batch)

    spec = pltpu.PrefetchScalarGridSpec(
        num_scalar_prefetch=7,
        grid=(n // ROUTE_TILE,),
        in_specs=[
            pl.BlockSpec((ROUTE_TILE, 2), lambda s, *_: (s, 0)),
            pl.BlockSpec((ROUTE_TILE, ROUTE_LANES), lambda s, *_: (s, 0)),
            pl.BlockSpec((ROUTE_TILE, D_MODEL), lambda s, *_: (s, 0)),
            pl.BlockSpec((None, 1, D_MODEL), lambda s, *_: (row(s), 0, 5)),
            pl.BlockSpec((None, 1, D_MODEL), lambda s, *_: (layer, 0, 0)),
            pl.BlockSpec((None, 1, D_MODEL), lambda s, *_: (layer, 0, 0)),
            pl.BlockSpec(memory_space=pl.ANY),
        ],
        out_specs=pl.BlockSpec((ROUTE_TILE, D_MODEL), lambda s, *_: (s, 0)),
        scratch_shapes=[
            pltpu.VMEM((2, N_EXPERTS, COMBINE_ROWS, D_MODEL), BF16),
            pltpu.SemaphoreType.DMA((2, N_EXPERTS)),
            pltpu.VMEM((COMBINE_ROWS, D_MODEL), BF16),
            pltpu.SemaphoreType.DMA((1,)),
            pltpu.VMEM((ROUTE_TILE, D_MODEL), F32),
            pltpu.VMEM((2, ROUTE_TILE, 128), jnp.int32),
            pltpu.VMEM((2, ROUTE_TILE, 128), F32),
        ],
    )
    return pl.pallas_call(
        _combine_kernel,
        grid_spec=spec,
        out_shape=jax.ShapeDtypeStruct((n, D_MODEL), F32),
        compiler_params=_params(("arbitrary",)),
        name="moe_combine",
    )(*lists, dest, route, x1, mod3, ln_g, ln_b, y)


def _moe(t, x1, mod3, wr, tri, w1, w3, w2, ln_g, ln_b, layer, j, n_lat, seq, batch):
    n = t.shape[0]
    n_tiles = n // ROUTE_TILE
    n_blocks = -(-(2 * n + N_EXPERTS * (EXPERT_ROWS - 1)) // EXPERT_ROWS) + 1
    i32 = jnp.int32

    route, before, total = _router(t, wr, tri)

    e12 = route[:, 0:2].astype(i32)
    rank = route[:, 4:6].astype(i32)
    counts = total[0, :N_EXPERTS].astype(i32)
    padded = (counts + EXPERT_ROWS - 1) // EXPERT_ROWS * EXPERT_ROWS
    pad_end = jnp.cumsum(padded)
    pad_start = pad_end - padded
    dest = pad_start[e12] + rank
    dst = dest.reshape(n_tiles, ROUTE_TILE, 2).transpose(0, 2, 1)
    block_start = jnp.arange(n_blocks, dtype=i32) * EXPERT_ROWS
    block_e = jnp.minimum(jnp.sum(block_start[:, None] >= pad_end[None, :], axis=1), N_EXPERTS - 1).astype(i32)
    block_used = (block_start < pad_end[-1]).astype(i32)

    cb = before[:, 0, :N_EXPERTS].astype(i32)
    ca = jnp.concatenate([cb[1:], counts[None, :]], axis=0)
    lo = pad_start[None, :] + cb
    hi = pad_start[None, :] + ca
    some = hi > lo
    tile_id = jnp.arange(n_tiles, dtype=i32)[:, None, None]

    def grouped(ok, group, n_groups, *values):
        key = jnp.where(ok, group, n_groups).reshape(-1)
        order = jnp.argsort(key, stable=True)
        starts = jnp.sum(key[None, :] < jnp.arange(n_groups + 1, dtype=i32)[:, None], axis=1, dtype=i32)
        return (starts,) + tuple(jnp.broadcast_to(v, ok.shape).reshape(-1)[order].astype(i32) for v in values)

    last_off = EXPERT_ROWS - GATHER_ROWS
    b_lo = lo // EXPERT_ROWS
    parts = []
    for blk, p_lo, p_hi in ((b_lo, lo, jnp.minimum(hi, (b_lo + 1) * EXPERT_ROWS)),
                            (b_lo + 1, (b_lo + 1) * EXPERT_ROWS, hi)):
        l_lo = p_lo - blk * EXPERT_ROWS
        l_hi = p_hi - blk * EXPERT_ROWS
        a0 = jnp.minimum(l_lo // 16 * 16, last_off)
        cut = a0 + GATHER_ROWS
        a1 = jnp.minimum(cut, last_off)
        parts.append((blk, a0, l_lo, jnp.minimum(l_hi, cut)))
        parts.append((blk, a1, jnp.maximum(l_lo, cut), l_hi))
    g_block, g_off, g_lo, g_hi = (jnp.stack(v, axis=-1) for v in zip(*parts))
    g_ok = some[:, :, None] & (g_hi > g_lo)
    g_start, g_tile, g_off, g_lo, g_hi = grouped(g_ok, g_block, n_blocks, tile_id, g_off, g_lo, g_hi)
    xs = _gather((g_start, g_tile | (g_off << 16), g_lo | (g_hi << 16)), t, dst, n_blocks)

    y = _experts(block_e, block_used, xs, w1, w3, w2, j)

    r0 = lo // 16 * 16
    over = some & (hi > r0 + COMBINE_ROWS)
    x_start, x_row, x_lo, x_hi = grouped(over, tile_id[:, :, 0], n_tiles, r0 + COMBINE_ROWS, r0 + COMBINE_ROWS, hi)
    lists = (r0.reshape(-1), lo.reshape(-1), hi.reshape(-1), x_start, x_row, x_lo, x_hi)
    return _combine(lists, y, dest, route, x1, mod3, ln_g, ln_b, layer, n_lat, seq, batch)


def _window_counts(n, w):
    t = np.arange(n)
    left = w // 2
    right = w - 1 - left
    return (np.minimum(t + right + 1, n) - np.maximum(t - left, 0)).astype(np.float32)


def _window_matrix(n, w):
    left = w // 2
    right = w - 1 - left
    t = np.arange(n)
    return ((t[None, :] >= t[:, None] - left) & (t[None, :] <= t[:, None] + right)).astype(np.float32)


def _pool_tables(seq, n_ctx):
    rows = seq // GRID_W
    per_tile = 256 // GRID_W
    mc = np.stack([np.kron(np.eye(per_tile, dtype=np.float32), _window_matrix(GRID_W, w)) for w in POOL_WINDOWS])
    inv = np.stack([1.0 / np.outer(_window_counts(rows, w), _window_counts(GRID_W, w)).reshape(seq)
                    for w in POOL_WINDOWS])
    inv = np.broadcast_to(inv[:, :, None], (4, seq, POOL_GROUP)).astype(np.float32)
    m1d = np.stack([_window_matrix(n_ctx, w) for w in POOL_WINDOWS])
    inv1d = np.stack([1.0 / _window_counts(n_ctx, w) for w in POOL_WINDOWS])
    inv1d = np.broadcast_to(inv1d[:, :, None], (4, n_ctx, POOL_GROUP)).astype(np.float32)
    return (jnp.asarray(mc, BF16), jnp.asarray(inv), jnp.asarray(m1d, BF16), jnp.asarray(inv1d))


def _rope_tables(seq):
    f32 = np.float32
    t = np.arange(seq)
    row = (t // GRID_W).astype(f32)
    col = (t % GRID_W).astype(f32)
    n_freq = RET_QK_DIM // 4
    inv = np.exp(-np.log(f32(ROPE_BASE)) * np.arange(n_freq, dtype=f32) / f32(n_freq)).astype(f32)
    ang = np.concatenate([row[:, None] * inv, col[:, None] * inv], -1).astype(f32)
    cos, sin = np.cos(ang).astype(f32), np.sin(ang).astype(f32)
    cos2 = np.concatenate([cos, cos], -1)
    sin2 = np.concatenate([-sin, sin], -1)
    cos2 = np.concatenate([cos2, np.ones((TOKEN_TILE, RET_QK_DIM), f32)], 0)
    sin2 = np.concatenate([sin2, np.zeros((TOKEN_TILE, RET_QK_DIM), f32)], 0)
    return jnp.asarray(cos2), jnp.asarray(sin2)


def kernel(x, c, ctx, c_ctx, ada_w, ada_b, w_in, pool_w, pool_scale, w_pool_out, w_ret_out, ret_decay_logit,
           w_out, ln_mix_g, ln_mix_b, ln_ffn_g, ln_ffn_b, ffn_w1, ffn_w3, ffn_w2, moe_router, moe_w1, moe_w3,
           moe_w2):
    batch, seq, d = x.shape
    n_ctx = ctx.shape[1]
    n_lat = batch * seq
    depth = ada_w.shape[0]
    assert d == D_MODEL and depth == DEPTH and batch < MOD_ROWS
    assert seq % TOKEN_TILE == 0 and (batch * n_ctx) % TOKEN_TILE == 0 and n_lat % n_ctx == 0
    assert FFN_ROWS % ROUTER_STEP == 0 and ROUTER_STEP % ROUTE_TILE == 0
    assert seq % FFN_ROWS == 0 and (batch * n_ctx) % FFN_ROWS == 0

    s_in = jnp.zeros((MOD_ROWS, d), F32).at[:batch].set(c).at[batch].set(c_ctx)
    mod = _modulation(s_in, ada_w, ada_b)
    cos2, sin2 = _rope_tables(seq)
    mc, inv, m1d, inv1d = _pool_tables(seq, n_ctx)
    tri = jnp.asarray(np.tril(np.ones((ROUTER_STEP, ROUTER_STEP), np.float32), -1), BF16)

    w_in_b = w_in.astype(BF16)
    pool_w_b = pool_w.astype(BF16)
    pool_s = pool_scale.reshape(depth, 1, POOL_WIDTH)
    wp_b, wr_b, wo_b = w_pool_out.astype(BF16), w_ret_out.astype(BF16), w_out.astype(BF16)
    ffn_b = ffn_w1.astype(BF16), ffn_w3.astype(BF16), ffn_w2.astype(BF16)
    moe_b = moe_w1.astype(BF16), moe_w3.astype(BF16), moe_w2.astype(BF16)
    ln_mix = ln_mix_g.reshape(depth, 1, d), ln_mix_b.reshape(depth, 1, d)
    ln_ffn = ln_ffn_g.reshape(depth, 1, d), ln_ffn_b.reshape(depth, 1, d)

    n_tok = n_lat + batch * n_ctx
    x_parts = (x.reshape(n_lat, d), ctx.reshape(batch * n_ctx, d))
    for l in range(depth):
        last = l == depth - 1
        mod3 = mod[l].reshape(MOD_ROWS, 1, 6 * d)
        pa, pb = _in_proj(x_parts, mod3, cos2, sin2, w_in_b, l, n_tok, n_lat, seq, batch)
        yp_lat = _pool_lat(pa, mc, inv, pool_w_b, pool_s, l, batch, seq)
        yp_ctx = _pool_ctx(pa, m1d, inv1d, pool_w_b, pool_s, l, batch, n_lat, n_ctx)
        dl = jnp.broadcast_to(ret_decay_logit[l].astype(F32)[:, :, None, None],
                              (2, RET_HEADS, RET_CHUNK, RET_V_DIM))
        z_lat, z_ctx = _retention(pa, pb, dl, batch, seq, n_ctx)
        n_out = n_lat if last else n_tok
        x1, t = _merge(x_parts, mod3, yp_lat, yp_ctx, z_lat, z_ctx, pb, wp_b, wr_b, wo_b,
                       *ln_mix, l, n_out, n_lat, seq, batch)
        j = l // 2
        if l % 2 == 0:
            xs = _ffn(t, x1, mod3, *ffn_b, *ln_ffn, l, j, n_lat, seq, batch)
        else:
            wr = jnp.zeros((d, ROUTE_LANES), BF16).at[:, :N_EXPERTS].set(moe_router[j].astype(BF16))
            xs = _moe(t, x1, mod3, wr, tri, *moe_b, *ln_ffn, l, j, n_lat, seq, batch)
        x_parts = (xs,)
    return xs[:n_lat].reshape(batch, seq, d)
```

```python
import functools

import jax
import jax.numpy as jnp
import numpy as np
from jax import lax
from jax.experimental import pallas as pl
from jax.experimental.pallas import tpu as pltpu

F32 = jnp.float32
BF16 = jnp.bfloat16

D_MODEL = 1024
DEPTH = 4
GRID_W = 64
POOL_WINDOWS = (2, 4, 8, 16)
POOL_GROUP = 128
POOL_WIDTH = POOL_GROUP * len(POOL_WINDOWS)
RET_HEADS = 4
RET_QK_DIM = 128
RET_V_DIM = 256
RET_CHUNK = 128
ROPE_BASE = 10000.0
OFF_Q = POOL_WIDTH
OFF_K = OFF_Q + RET_HEADS * RET_QK_DIM
OFF_V = OFF_K + RET_HEADS * RET_QK_DIM
OFF_G = OFF_V + RET_HEADS * RET_V_DIM
WIDTH_B = 3 * D_MODEL
D_FF = 2816
N_EXPERTS = 8
EXPERT_FF = 3584
DEEPNORM_ALPHA = (2 * DEPTH) ** 0.25
LN_EPS = 1e-5
K_SCALE = RET_QK_DIM ** -0.5

MOD_ROWS = 24
TOKEN_TILE = 512
ROUTE_TILE = 256
ROUTER_STEP = 512
EXPERT_ROWS = 512
FFN_ROWS = 2 * TOKEN_TILE
VMEM_LIMIT = 56 * 1024 * 1024


def _dot(a, b):
    return jnp.dot(a, b, preferred_element_type=F32)


def _sigmoid(x):
    return 1.0 / (1.0 + jnp.exp(-x))


def _split_bf16(a):
    hi = a.astype(BF16)
    lo = (a - hi.astype(F32)).astype(BF16)
    return hi, lo


def _layer_norm(v, g, b):
    mean = jnp.mean(v, axis=-1, keepdims=True)
    vc = v - mean
    var = jnp.mean(vc * vc, axis=-1, keepdims=True)
    return vc * lax.rsqrt(var + LN_EPS) * g + b


def _params(sem, vmem=VMEM_LIMIT):
    return pltpu.CompilerParams(dimension_semantics=sem, vmem_limit_bytes=vmem)


def _mod_kernel(s_ref, w_ref, b_ref, o_ref):
    s = s_ref[...]
    s = s * _sigmoid(s)
    s_hi, s_lo = _split_bf16(s)
    w_hi, w_lo = _split_bf16(w_ref[...])
    o_ref[...] = _dot(s_hi, w_hi) + (_dot(s_hi, w_lo) + _dot(s_lo, w_hi)) + b_ref[...]


def _modulation(s_in, ada_w, ada_b):
    depth, d, width = ada_w.shape
    tn = 1536
    return pl.pallas_call(
        _mod_kernel,
        grid=(depth, width // tn),
        in_specs=[
            pl.BlockSpec((MOD_ROWS, d), lambda l, j: (0, 0)),
            pl.BlockSpec((None, d, tn), lambda l, j: (l, 0, j)),
            pl.BlockSpec((None, 1, tn), lambda l, j: (l, 0, j)),
        ],
        out_specs=pl.BlockSpec((None, MOD_ROWS, tn), lambda l, j: (l, 0, j)),
        out_shape=jax.ShapeDtypeStruct((depth, MOD_ROWS, width), F32),
        compiler_params=_params(("arbitrary", "arbitrary")),
        name="adaln_mod",
    )(s_in, ada_w, ada_b.reshape(depth, 1, width))


IN_COLS = 512
IN_BLOCKS = (OFF_G + WIDTH_B) // IN_COLS


def _pick_x(x_refs, rows, is_lat):
    if len(x_refs) == 1:
        return x_refs[0][rows, :]
    return jnp.where(is_lat, x_refs[0][rows, :], x_refs[1][rows, :])


def _in_kernel(*refs, n_lat_tiles, n_x):
    x_refs = refs[:n_x]
    sh_ref, sc_ref, cos_ref, sin_ref = refs[n_x:n_x + 4]
    w_refs = refs[n_x + 4:n_x + 4 + IN_BLOCKS]
    pa_ref, pb_ref = refs[n_x + 4 + IN_BLOCKS:]
    x = _pick_x(x_refs, slice(None), pl.program_id(0) < n_lat_tiles)
    h = (x * (1.0 + sc_ref[...]) + sh_ref[...]).astype(BF16)
    cos = cos_ref[...]
    sin = sin_ref[...]
    for blk in range(IN_BLOCKS):
        c0 = blk * IN_COLS
        acc = _dot(h, w_refs[blk][...])
        if c0 in (OFF_Q, OFF_K):
            for hh in range(RET_HEADS):
                t = acc[:, hh * RET_QK_DIM:(hh + 1) * RET_QK_DIM]
                r = t * cos + pltpu.roll(t, RET_QK_DIM // 2, 1) * sin
                if c0 == OFF_K:
                    r = r * K_SCALE
                pa_ref[:, c0 + hh * RET_QK_DIM:c0 + (hh + 1) * RET_QK_DIM] = r.astype(BF16)
        elif c0 < OFF_G:
            pa_ref[:, c0:c0 + IN_COLS] = acc.astype(BF16)
        else:
            pb_ref[:, c0 - OFF_G:c0 - OFF_G + IN_COLS] = acc.astype(BF16)


def _tile_maps(n_lat, seq, batch, tile):
    nl = n_lat // tile
    tpb = seq // tile

    def row(i):
        return jnp.where(i < nl, i // tpb, batch)

    def lat(i):
        return jnp.minimum(i, nl - 1)

    def ctx(i):
        return jnp.maximum(i - nl, 0)

    return nl, tpb, row, lat, ctx


def _x_operands(x_parts, n_lat, seq, batch, tile):
    _, _, _, lat, ctx = _tile_maps(n_lat, seq, batch, tile)
    if len(x_parts) == 1:
        return [pl.BlockSpec((tile, D_MODEL), lambda i: (i, 0))], list(x_parts)
    return ([pl.BlockSpec((tile, D_MODEL), lambda i: (lat(i), 0)),
             pl.BlockSpec((tile, D_MODEL), lambda i: (ctx(i), 0))], list(x_parts))


def _in_proj(x_parts, mod3, cos2, sin2, w_in, layer, n_tok, n_lat, seq, batch):
    tm = TOKEN_TILE
    nl, tpb, row, _, _ = _tile_maps(n_lat, seq, batch, tm)
    x_specs, x_args = _x_operands(x_parts, n_lat, seq, batch, tm)

    def rope(i):
        return jnp.where(i < nl, i % tpb, tpb)

    def w_spec(blk):
        return pl.BlockSpec((None, D_MODEL, IN_COLS), lambda i: (layer, 0, blk))

    return pl.pallas_call(
        functools.partial(_in_kernel, n_lat_tiles=nl, n_x=len(x_args)),
        grid=(n_tok // tm,),
        in_specs=x_specs + [
            pl.BlockSpec((None, 1, D_MODEL), lambda i: (row(i), 0, 0)),
            pl.BlockSpec((None, 1, D_MODEL), lambda i: (row(i), 0, 1)),
            pl.BlockSpec((tm, RET_QK_DIM), lambda i: (rope(i), 0)),
            pl.BlockSpec((tm, RET_QK_DIM), lambda i: (rope(i), 0)),
        ] + [w_spec(blk) for blk in range(IN_BLOCKS)],
        out_specs=[
            pl.BlockSpec((tm, OFF_G), lambda i: (i, 0)),
            pl.BlockSpec((tm, WIDTH_B), lambda i: (i, 0)),
        ],
        out_shape=[
            jax.ShapeDtypeStruct((n_tok, OFF_G), BF16),
            jax.ShapeDtypeStruct((n_tok, WIDTH_B), BF16),
        ],
        compiler_params=_params(("arbitrary",)),
        name="in_proj",
    )(*x_args, mod3, mod3, cos2, sin2, *([w_in] * IN_BLOCKS))


POOL_PAD_ROWS = 8
POOL_ROW_CHUNK = 8


def _pool_kernel(u_ref, mc_ref, inv_ref, pw_ref, ps_ref, o_ref, zp_ref, *, rows):
    seq = rows * GRID_W
    pad = POOL_PAD_ROWS * GRID_W
    chunk = POOL_ROW_CHUNK * GRID_W
    zeros = jnp.zeros((pad, POOL_GROUP), F32)
    for g, w in enumerate(POOL_WINDOWS):
        lanes = slice(g * POOL_GROUP, (g + 1) * POOL_GROUP)
        zp_ref[0:pad, :] = zeros
        zp_ref[pad + seq:pad + seq + pad, :] = zeros
        for c in range(seq // 256):
            zp_ref[pad + c * 256:pad + (c + 1) * 256, :] = _dot(mc_ref[g], u_ref[c * 256:(c + 1) * 256, lanes])
        left = w // 2

        def body(rc, carry, g=g, w=w, left=left, lanes=lanes):
            tok = pl.multiple_of(rc * chunk, chunk)
            acc = zp_ref[pl.ds(tok + pad - left * GRID_W, chunk), :]
            for k in range(1, w):
                acc = acc + zp_ref[pl.ds(tok + pad + (k - left) * GRID_W, chunk), :]
            pooled = acc * inv_ref[g, pl.ds(tok, chunk), :]
            d = (pooled - u_ref[pl.ds(tok, chunk), lanes].astype(F32)).astype(BF16)
            y = _dot(d, pw_ref[g]) * ps_ref[:, lanes]
            o_ref[pl.ds(tok, chunk), lanes] = y.astype(BF16)
            return carry

        lax.fori_loop(0, rows // POOL_ROW_CHUNK, body, 0, unroll=4)


def _pool_lat(pa, mc, inv, pw, ps, layer, batch, seq):
    rows = seq // GRID_W
    return pl.pallas_call(
        functools.partial(_pool_kernel, rows=rows),
        grid=(batch,),
        in_specs=[
            pl.BlockSpec((seq, POOL_WIDTH), lambda b: (b, 0)),
            pl.BlockSpec((4, 256, 256), lambda b: (0, 0, 0)),
            pl.BlockSpec((4, seq, POOL_GROUP), lambda b: (0, 0, 0)),
            pl.BlockSpec((None, 4, POOL_GROUP, POOL_GROUP), lambda b: (layer, 0, 0, 0)),
            pl.BlockSpec((None, 1, POOL_WIDTH), lambda b: (layer, 0, 0)),
        ],
        out_specs=pl.BlockSpec((seq, POOL_WIDTH), lambda b: (b, 0)),
        out_shape=jax.ShapeDtypeStruct((batch * seq, POOL_WIDTH), BF16),
        scratch_shapes=[pltpu.VMEM(((rows + 2 * POOL_PAD_ROWS) * GRID_W, POOL_GROUP), F32)],
        compiler_params=_params(("arbitrary",)),
        name="pool_lat",
    )(pa, mc, inv, pw, ps)


def _pool_ctx_kernel(u_ref, m_ref, inv_ref, pw_ref, ps_ref, o_ref):
    for g in range(len(POOL_WINDOWS)):
        lanes = slice(g * POOL_GROUP, (g + 1) * POOL_GROUP)
        ug = u_ref[:, lanes]
        pooled = _dot(m_ref[g], ug) * inv_ref[g]
        d = (pooled - ug.astype(F32)).astype(BF16)
        o_ref[:, lanes] = (_dot(d, pw_ref[g]) * ps_ref[:, lanes]).astype(BF16)


def _pool_ctx(pa, m1d, inv1d, pw, ps, layer, batch, n_lat, n_ctx):
    first = n_lat // n_ctx
    return pl.pallas_call(
        _pool_ctx_kernel,
        grid=(batch,),
        in_specs=[
            pl.BlockSpec((n_ctx, POOL_WIDTH), lambda b: (first + b, 0)),
            pl.BlockSpec((4, n_ctx, n_ctx), lambda b: (0, 0, 0)),
            pl.BlockSpec((4, n_ctx, POOL_GROUP), lambda b: (0, 0, 0)),
            pl.BlockSpec((None, 4, POOL_GROUP, POOL_GROUP), lambda b: (layer, 0, 0, 0)),
            pl.BlockSpec((None, 1, POOL_WIDTH), lambda b: (layer, 0, 0)),
        ],
        out_specs=pl.BlockSpec((n_ctx, POOL_WIDTH), lambda b: (b, 0)),
        out_shape=jax.ShapeDtypeStruct((batch * n_ctx, POOL_WIDTH), BF16),
        compiler_params=_params(("arbitrary",)),
        name="pool_ctx",
    )(pa, m1d, inv1d, pw, ps)


def _log_sigmoid(x):
    return jnp.minimum(x, 0.0) - jnp.log1p(jnp.exp(-jnp.abs(x)))


def _decays(dl, backward):
    c = RET_CHUNK
    lg = _log_sigmoid(dl)
    lgq = lg[:, :RET_QK_DIM]
    ii = lax.broadcasted_iota(jnp.int32, (c, c), 0)
    jj = lax.broadcasted_iota(jnp.int32, (c, c), 1)
    pos = lax.broadcasted_iota(jnp.int32, (c, RET_QK_DIM), 0).astype(F32)
    if backward:
        diff = (jj - ii).astype(F32)
        qdec = jnp.exp(lgq * (c - pos))
        kdec = jnp.exp(lgq * pos)
    else:
        diff = (ii - jj).astype(F32)
        qdec = jnp.exp(lgq * (pos + 1.0))
        kdec = jnp.exp(lgq * (c - 1.0 - pos))
    inner = jnp.where(diff >= 0, jnp.exp(lgq * jnp.maximum(diff, 0.0)), 0.0)
    cdec = jnp.exp(lg * float(c))
    return inner, qdec, kdec, cdec


def _chunk_kv(kc, vc, kdec):
    kd = (kc.astype(F32) * kdec).astype(BF16)
    return lax.dot_general(kd, vc, (((0,), (0,)), ((), ())), preferred_element_type=F32)


def _chunk_out(qc, kc, vc, gc, states, mask, qdec2):
    scores = lax.dot_general(qc, kc, (((1,), (1,)), ((), ())), preferred_element_type=F32) * mask
    qf = qc.astype(F32)
    qd = (jnp.concatenate([qf, qf], axis=1) * qdec2).astype(BF16)
    o = _dot(scores.astype(BF16), vc) + _dot(qd, states)
    mean = jnp.mean(o, axis=-1, keepdims=True)
    oc = o - mean
    var = jnp.mean(oc * oc, axis=-1, keepdims=True)
    on = oc * lax.rsqrt(var + LN_EPS)
    gf = gc.astype(F32)
    return (gf * _sigmoid(gf) * on).astype(BF16)


def _ret_kernel(q_ref, k_ref, v_ref, g_ref, qc_ref, kc_ref, vc_ref, gc_ref, dl_ref,
                z_ref, zc_ref, st_ref, stc_ref, run_ref, *, n_chunks, n_ctx_chunks):
    c = RET_CHUNK
    dk = RET_QK_DIM
    inner_f, qdec_f, kdec_f, cdec_f = _decays(dl_ref[0], False)
    inner_b, qdec_b, kdec_b, cdec_b = _decays(dl_ref[1], True)
    mask = inner_f + inner_b
    qdec2 = jnp.concatenate([qdec_f, qdec_b], axis=1)

    def rows(j):
        return slice(j * c, (j + 1) * c)

    s = jnp.zeros((dk, RET_V_DIM), F32)
    for j in range(n_ctx_chunks):
        stc_ref[j, 0:dk, :] = s.astype(BF16)
        s = s * cdec_f + _chunk_kv(kc_ref[rows(j), :], vc_ref[rows(j), :], kdec_f)
    run_ref[0] = s
    s = jnp.zeros((dk, RET_V_DIM), F32)
    for j in reversed(range(n_ctx_chunks)):
        stc_ref[j, dk:2 * dk, :] = s.astype(BF16)
        s = s * cdec_b + _chunk_kv(kc_ref[rows(j), :], vc_ref[rows(j), :], kdec_b)
    run_ref[1] = s
    for j in range(n_ctx_chunks):
        zc_ref[rows(j), :] = _chunk_out(qc_ref[rows(j), :], kc_ref[rows(j), :], vc_ref[rows(j), :],
                                        gc_ref[rows(j), :], stc_ref[j], mask, qdec2)

    def scan(t, carry):
        rf = pl.ds(pl.multiple_of(t * c, c), c)
        sf = run_ref[0]
        st_ref[t, 0:dk, :] = sf.astype(BF16)
        run_ref[0] = sf * cdec_f + _chunk_kv(k_ref[rf, :], v_ref[rf, :], kdec_f)
        tb = n_chunks - 1 - t
        rb = pl.ds(pl.multiple_of(tb * c, c), c)
        sb = run_ref[1]
        st_ref[tb, dk:2 * dk, :] = sb.astype(BF16)
        run_ref[1] = sb * cdec_b + _chunk_kv(k_ref[rb, :], v_ref[rb, :], kdec_b)
        return carry

    lax.fori_loop(0, n_chunks, scan, 0, unroll=4)

    def emit(t, carry):
        r = pl.ds(pl.multiple_of(t * c, c), c)
        z_ref[r, :] = _chunk_out(q_ref[r, :], k_ref[r, :], v_ref[r, :], g_ref[r, :], st_ref[t], mask, qdec2)
        return carry

    lax.fori_loop(0, n_chunks, emit, 0, unroll=8)


def _retention(pa, pb, dl, batch, seq, n_ctx):
    n_lat = batch * seq
    first = n_lat // n_ctx
    qk0 = OFF_Q // RET_QK_DIM
    kk0 = OFF_K // RET_QK_DIM
    v0 = OFF_V // RET_V_DIM
    return pl.pallas_call(
        functools.partial(_ret_kernel, n_chunks=seq // RET_CHUNK, n_ctx_chunks=n_ctx // RET_CHUNK),
        grid=(batch, RET_HEADS),
        in_specs=[
            pl.BlockSpec((seq, RET_QK_DIM), lambda b, h: (b, qk0 + h)),
            pl.BlockSpec((seq, RET_QK_DIM), lambda b, h: (b, kk0 + h)),
            pl.BlockSpec((seq, RET_V_DIM), lambda b, h: (b, v0 + h)),
            pl.BlockSpec((seq, RET_V_DIM), lambda b, h: (b, h)),
            pl.BlockSpec((n_ctx, RET_QK_DIM), lambda b, h: (first + b, qk0 + h)),
            pl.BlockSpec((n_ctx, RET_QK_DIM), lambda b, h: (first + b, kk0 + h)),
            pl.BlockSpec((n_ctx, RET_V_DIM), lambda b, h: (first + b, v0 + h)),
            pl.BlockSpec((n_ctx, RET_V_DIM), lambda b, h: (first + b, h)),
            pl.BlockSpec((2, None, RET_CHUNK, RET_V_DIM), lambda b, h: (0, h, 0, 0)),
        ],
        out_specs=[
            pl.BlockSpec((seq, RET_V_DIM), lambda b, h: (b, h)),
            pl.BlockSpec((n_ctx, RET_V_DIM), lambda b, h: (b, h)),
        ],
        out_shape=[
            jax.ShapeDtypeStruct((n_lat, RET_HEADS * RET_V_DIM), BF16),
            jax.ShapeDtypeStruct((batch * n_ctx, RET_HEADS * RET_V_DIM), BF16),
        ],
        scratch_shapes=[
            pltpu.VMEM((seq // RET_CHUNK, 2 * RET_QK_DIM, RET_V_DIM), BF16),
            pltpu.VMEM((n_ctx // RET_CHUNK, 2 * RET_QK_DIM, RET_V_DIM), BF16),
            pltpu.VMEM((2, RET_QK_DIM, RET_V_DIM), F32),
        ],
        compiler_params=_params(("arbitrary", "arbitrary")),
        name="retention",
    )(pa, pa, pa, pb, pa, pa, pa, pb, dl)


MIX_COLS = 256


def _merge_kernel(*refs, n_lat_tiles, n_x):
    x_refs = refs[:n_x]
    (gm_ref, sh_ref, sc_ref, ypl_ref, ypc_ref, zl_ref, zc_ref, gp_ref, gr_ref,
     wp_ref, wr_ref, wo_ref, lg_ref, lb_ref, x1_ref, t_ref, mix_ref) = refs[n_x:]
    is_lat = pl.program_id(0) < n_lat_tiles
    yp = jnp.where(is_lat, ypl_ref[...], ypc_ref[...])
    z = jnp.where(is_lat, zl_ref[...], zc_ref[...])
    x = _pick_x(x_refs, slice(None), is_lat)
    for c0 in range(0, D_MODEL, MIX_COLS):
        cols = slice(c0, c0 + MIX_COLS)
        y_pool = _dot(yp, wp_ref[:, cols])
        y_ret = _dot(z, wr_ref[:, cols])
        mix = _sigmoid(gp_ref[:, cols].astype(F32)) * y_pool + _sigmoid(gr_ref[:, cols].astype(F32)) * y_ret
        mix_ref[:, cols] = mix.astype(BF16)
    y = _dot(mix_ref[...], wo_ref[...])
    x1 = _layer_norm(DEEPNORM_ALPHA * x + gm_ref[...] * y, lg_ref[...], lb_ref[...])
    x1_ref[...] = x1
    t_ref[...] = (x1 * (1.0 + sc_ref[...]) + sh_ref[...]).astype(BF16)


def _merge(x_parts, mod3, yp_lat, yp_ctx, z_lat, z_ctx, pb, wp, wr, wo, ln_g, ln_b, layer,
           n_out, n_lat, seq, batch):
    tm = TOKEN_TILE
    nl, _, row, lat, ctx = _tile_maps(n_lat, seq, batch, tm)
    x_specs, x_args = _x_operands(x_parts, n_lat, seq, batch, tm)

    return pl.pallas_call(
        functools.partial(_merge_kernel, n_lat_tiles=nl, n_x=len(x_args)),
        grid=(n_out // tm,),
        in_specs=x_specs + [
            pl.BlockSpec((None, 1, D_MODEL), lambda i: (row(i), 0, 2)),
            pl.BlockSpec((None, 1, D_MODEL), lambda i: (row(i), 0, 3)),
            pl.BlockSpec((None, 1, D_MODEL), lambda i: (row(i), 0, 4)),
            pl.BlockSpec((tm, POOL_WIDTH), lambda i: (lat(i), 0)),
            pl.BlockSpec((tm, POOL_WIDTH), lambda i: (ctx(i), 0)),
            pl.BlockSpec((tm, D_MODEL), lambda i: (lat(i), 0)),
            pl.BlockSpec((tm, D_MODEL), lambda i: (ctx(i), 0)),
            pl.BlockSpec((tm, D_MODEL), lambda i: (i, 1)),
            pl.BlockSpec((tm, D_MODEL), lambda i: (i, 2)),
            pl.BlockSpec((None, POOL_WIDTH, D_MODEL), lambda i: (layer, 0, 0)),
            pl.BlockSpec((None, D_MODEL, D_MODEL), lambda i: (layer, 0, 0)),
            pl.BlockSpec((None, D_MODEL, D_MODEL), lambda i: (layer, 0, 0)),
            pl.BlockSpec((None, 1, D_MODEL), lambda i: (layer, 0, 0)),
            pl.BlockSpec((None, 1, D_MODEL), lambda i: (layer, 0, 0)),
        ],
        out_specs=[
            pl.BlockSpec((tm, D_MODEL), lambda i: (i, 0)),
            pl.BlockSpec((tm, D_MODEL), lambda i: (i, 0)),
        ],
        out_shape=[
            jax.ShapeDtypeStruct((n_out, D_MODEL), F32),
            jax.ShapeDtypeStruct((n_out, D_MODEL), BF16),
        ],
        scratch_shapes=[pltpu.VMEM((tm, D_MODEL), BF16)],
        compiler_params=_params(("arbitrary",)),
        name="merge",
    )(*x_args, mod3, mod3, mod3, yp_lat, yp_ctx, z_lat, z_ctx, pb, pb, wp, wr, wo, ln_g, ln_b)


SWIGLU_COLS = 256


def _swiglu_hidden(t, w1_ref, w3_ref, h_ref):
    width = h_ref.shape[1]
    for c0 in range(0, width, SWIGLU_COLS):
        cols = slice(c0, min(c0 + SWIGLU_COLS, width))
        a = _dot(t, w1_ref[:, cols])
        h_ref[:, cols] = (a * _sigmoid(a) * _dot(t, w3_ref[:, cols])).astype(BF16)


def _ffn_kernel(t_ref, x1_ref, gm_ref, w1_ref, w3_ref, w2_ref, lg_ref, lb_ref, o_ref, h_ref):
    for r in range(FFN_ROWS // TOKEN_TILE):
        rows = slice(r * TOKEN_TILE, (r + 1) * TOKEN_TILE)
        _swiglu_hidden(t_ref[rows, :], w1_ref, w3_ref, h_ref.at[r])
        f = _dot(h_ref[r], w2_ref[...])
        o_ref[rows, :] = _layer_norm(DEEPNORM_ALPHA * x1_ref[rows, :] + gm_ref[...] * f, lg_ref[...], lb_ref[...])


def _ffn(t, x1, mod3, w1, w3, w2, ln_g, ln_b, layer, j, n_lat, seq, batch):
    n = t.shape[0]
    tm = FFN_ROWS
    _, _, row, _, _ = _tile_maps(n_lat, seq, batch, tm)
    resident = pl.Buffered(1)

    return pl.pallas_call(
        _ffn_kernel,
        grid=(n // tm,),
        in_specs=[
            pl.BlockSpec((tm, D_MODEL), lambda i: (i, 0)),
            pl.BlockSpec((tm, D_MODEL), lambda i: (i, 0)),
            pl.BlockSpec((None, 1, D_MODEL), lambda i: (row(i), 0, 5)),
            pl.BlockSpec((None, D_MODEL, D_FF), lambda i: (j, 0, 0), pipeline_mode=resident),
            pl.BlockSpec((None, D_MODEL, D_FF), lambda i: (j, 0, 0), pipeline_mode=resident),
            pl.BlockSpec((None, D_FF, D_MODEL), lambda i: (j, 0, 0), pipeline_mode=resident),
            pl.BlockSpec((None, 1, D_MODEL), lambda i: (layer, 0, 0)),
            pl.BlockSpec((None, 1, D_MODEL), lambda i: (layer, 0, 0)),
        ],
        out_specs=pl.BlockSpec((tm, D_MODEL), lambda i: (i, 0)),
        out_shape=jax.ShapeDtypeStruct((n, D_MODEL), F32),
        scratch_shapes=[pltpu.VMEM((tm // TOKEN_TILE, TOKEN_TILE, D_FF), BF16)],
        compiler_params=_params(("arbitrary",)),
        name="ffn_dense",
    )(t, x1, mod3, w1, w3, w2, ln_g, ln_b)


ROUTE_LANES = 128


def _router_kernel(t_ref, wr_ref, tri_ref, route_ref, before_ref, total_ref, run_ref):
    @pl.when(pl.program_id(0) == 0)
    def _():
        run_ref[...] = jnp.zeros_like(run_ref)

    tt = t_ref.shape[0]
    logits = _dot(t_ref[...], wr_ref[...])
    lane = lax.broadcasted_iota(jnp.int32, (tt, ROUTE_LANES), 1)
    neg = jnp.float32(-jnp.inf)
    lg = jnp.where(lane < N_EXPERTS, logits, neg)
    m1 = jnp.max(lg, axis=1, keepdims=True)
    i1 = jnp.min(jnp.where(lg == m1, lane, ROUTE_LANES), axis=1, keepdims=True)
    lg2 = jnp.where(lane == i1, neg, lg)
    m2 = jnp.max(lg2, axis=1, keepdims=True)
    i2 = jnp.min(jnp.where(lg2 == m2, lane, ROUTE_LANES), axis=1, keepdims=True)
    e = jnp.exp(m2 - m1)
    w1 = 1.0 / (1.0 + e)
    w2 = e / (1.0 + e)
    hit1 = lane == i1
    hit2 = lane == i2
    onehot = jnp.where(hit1 | hit2, 1.0, 0.0)
    run = run_ref[...]
    prefix = _dot(tri_ref[...], onehot.astype(BF16)) + run
    r1 = jnp.sum(jnp.where(hit1, prefix, 0.0), axis=1, keepdims=True)
    r2 = jnp.sum(jnp.where(hit2, prefix, 0.0), axis=1, keepdims=True)
    for k in range(tt // ROUTE_TILE):
        before_ref[k] = prefix[k * ROUTE_TILE:k * ROUTE_TILE + 1, :]
    run = run + jnp.sum(onehot, axis=0, keepdims=True)
    run_ref[...] = run
    total_ref[...] = run
    out = jnp.where(lane == 0, i1.astype(F32), 0.0)
    out = jnp.where(lane == 1, i2.astype(F32), out)
    out = jnp.where(lane == 2, w1, out)
    out = jnp.where(lane == 3, w2, out)
    out = jnp.where(lane == 4, r1, out)
    out = jnp.where(lane == 5, r2, out)
    route_ref[...] = out


def _router(t, wr, tri):
    n = t.shape[0]
    tt = ROUTER_STEP
    sub = tt // ROUTE_TILE
    return pl.pallas_call(
        _router_kernel,
        grid=(n // tt,),
        in_specs=[
            pl.BlockSpec((tt, D_MODEL), lambda i: (i, 0)),
            pl.BlockSpec((D_MODEL, ROUTE_LANES), lambda i: (0, 0)),
            pl.BlockSpec((tt, tt), lambda i: (0, 0)),
        ],
        out_specs=[
            pl.BlockSpec((tt, ROUTE_LANES), lambda i: (i, 0)),
            pl.BlockSpec((sub, 1, ROUTE_LANES), lambda i: (i, 0, 0)),
            pl.BlockSpec((1, ROUTE_LANES), lambda i: (0, 0)),
        ],
        out_shape=[
            jax.ShapeDtypeStruct((n, ROUTE_LANES), F32),
            jax.ShapeDtypeStruct((n // ROUTE_TILE, 1, ROUTE_LANES), F32),
            jax.ShapeDtypeStruct((1, ROUTE_LANES), F32),
        ],
        scratch_shapes=[pltpu.VMEM((1, ROUTE_LANES), F32)],
        compiler_params=_params(("arbitrary",)),
        name="moe_router",
    )(t, wr, tri)


DMA_RING = 16
DMA_AHEAD = 12
GATHER_ROWS = 144


def _gather_kernel(es_ref, sa_ref, rr_ref, dst_ref, t_hbm, o_ref, buf, sem):
    b = pl.program_id(0)
    total = es_ref[pl.num_programs(0)]
    o_ref[...] = jnp.zeros_like(o_ref)

    def tile_copy(j):
        slot = j & (DMA_RING - 1)
        start = pl.multiple_of((sa_ref[j] & 0xFFFF) * ROUTE_TILE, ROUTE_TILE)
        return pltpu.make_async_copy(t_hbm.at[pl.ds(start, ROUTE_TILE)], buf.at[slot], sem.at[slot])

    @pl.when(b == 0)
    def _():
        for i in range(DMA_AHEAD):
            @pl.when(i < total)
            def _(i=i):
                tile_copy(i).start()

    def entry(j, slot, owned):
        tile = sa_ref[j] & 0xFFFF
        a = pl.multiple_of(lax.shift_right_logical(sa_ref[j], 16), 16)
        lo = rr_ref[j] & 0xFFFF
        hi = jnp.where(owned, lax.shift_right_logical(rr_ref[j], 16), 0)
        local = a + lax.broadcasted_iota(jnp.int32, (GATHER_ROWS, ROUTE_TILE), 0)
        rows = jnp.where((local >= lo) & (local < hi), local + b * EXPERT_ROWS, -1)
        d = dst_ref[tile]
        sel = (d[0:1, :] == rows) | (d[1:2, :] == rows)
        win = pl.ds(a, GATHER_ROWS)
        picked = _dot(jnp.where(sel, 1.0, 0.0).astype(BF16), buf[slot])
        o_ref[win, :] = o_ref[win, :] + picked.astype(BF16)

    j0 = es_ref[b]
    j1 = es_ref[b + 1]

    def body(k, carry):
        ja = j0 + 2 * k
        jb = ja + 1
        has_b = jb < j1
        tile_copy(ja).wait()

        @pl.when(ja + DMA_AHEAD < total)
        def _():
            tile_copy(ja + DMA_AHEAD).start()

        @pl.when(has_b)
        def _():
            tile_copy(jb).wait()

        @pl.when(has_b & (jb + DMA_AHEAD < total))
        def _():
            tile_copy(jb + DMA_AHEAD).start()

        entry(ja, ja & (DMA_RING - 1), True)
        entry(jb, jnp.where(has_b, jb, ja) & (DMA_RING - 1), has_b)
        return carry

    lax.fori_loop(0, (j1 - j0 + 1) // 2, body, 0)


def _gather(lists, t, dst, n_blocks):
    n_tiles = dst.shape[0]
    spec = pltpu.PrefetchScalarGridSpec(
        num_scalar_prefetch=3,
        grid=(n_blocks,),
        in_specs=[
            pl.BlockSpec((n_tiles, 2, ROUTE_TILE), lambda b, es, sa, rr: (0, 0, 0)),
            pl.BlockSpec(memory_space=pl.ANY),
        ],
        out_specs=pl.BlockSpec((EXPERT_ROWS, D_MODEL), lambda b, es, sa, rr: (b, 0)),
        scratch_shapes=[
            pltpu.VMEM((DMA_RING, ROUTE_TILE, D_MODEL), BF16),
            pltpu.SemaphoreType.DMA((DMA_RING,)),
        ],
    )
    return pl.pallas_call(
        _gather_kernel,
        grid_spec=spec,
        out_shape=jax.ShapeDtypeStruct((n_blocks * EXPERT_ROWS, D_MODEL), BF16),
        compiler_params=_params(("arbitrary",)),
        name="moe_gather",
    )(*lists, dst, t)


def _expert_kernel(be_ref, bv_ref, x_ref, w1_ref, w3_ref, w2_ref, o_ref, h_ref):
    used = bv_ref[pl.program_id(0)] == 1

    @pl.when(used)
    def _():
        _swiglu_hidden(x_ref[...], w1_ref, w3_ref, h_ref)
        o_ref[...] = _dot(h_ref[...], w2_ref[...]).astype(BF16)

    @pl.when(jnp.logical_not(used))
    def _():
        o_ref[...] = jnp.zeros_like(o_ref)


def _experts(block_e, block_used, xs, w1, w3, w2, j):
    n_blocks = block_e.shape[0]
    resident = pl.Buffered(1)
    spec = pltpu.PrefetchScalarGridSpec(
        num_scalar_prefetch=2,
        grid=(n_blocks,),
        in_specs=[
            pl.BlockSpec((EXPERT_ROWS, D_MODEL), lambda b, be, bv: (b, 0)),
            pl.BlockSpec((None, None, D_MODEL, EXPERT_FF), lambda b, be, bv: (j, be[b], 0, 0), pipeline_mode=resident),
            pl.BlockSpec((None, None, D_MODEL, EXPERT_FF), lambda b, be, bv: (j, be[b], 0, 0), pipeline_mode=resident),
            pl.BlockSpec((None, None, EXPERT_FF, D_MODEL), lambda b, be, bv: (j, be[b], 0, 0), pipeline_mode=resident),
        ],
        out_specs=pl.BlockSpec((EXPERT_ROWS, D_MODEL), lambda b, be, bv: (b, 0)),
        scratch_shapes=[pltpu.VMEM((EXPERT_ROWS, EXPERT_FF), BF16)],
    )
    return pl.pallas_call(
        _expert_kernel,
        grid_spec=spec,
        out_shape=jax.ShapeDtypeStruct(xs.shape, BF16),
        compiler_params=_params(("arbitrary",)),
        name="moe_experts",
    )(block_e, block_used, xs, w1, w3, w2)


COMBINE_ROWS = ROUTE_TILE


def _combine_kernel(r_ref, lo_ref, hi_ref, es_ref, xr_ref, xl_ref, xh_ref, dest_ref, route_ref, x1_ref, gm_ref, lg_ref, lb_ref,
                    y_hbm, o_ref, buf, sem, xbuf, xsem, acc_ref, d_ref, w_ref):
    s = pl.program_id(0)
    half = s & 1
    for k in range(2):
        d_ref[k] = jnp.broadcast_to(dest_ref[:, k:k + 1], d_ref.shape[1:])
        w_ref[k] = jnp.broadcast_to(route_ref[:, 2 + k:3 + k], w_ref.shape[1:])

    def win_copy(tile, e, which):
        start = pl.multiple_of(r_ref[tile * N_EXPERTS + e], 16)
        return pltpu.make_async_copy(y_hbm.at[pl.ds(start, COMBINE_ROWS)], buf.at[which, e], sem.at[which, e])

    @pl.when(s == 0)
    def _():
        for e in range(N_EXPERTS):
            win_copy(0, e, 0).start()

    @pl.when(s + 1 < pl.num_programs(0))
    def _():
        for e in range(N_EXPERTS):
            win_copy(s + 1, e, 1 - half).start()

    def weighted(first, lo, hi, rows):
        d1 = d_ref[0]
        d2 = d_ref[1]
        in1 = (d1 >= lo) & (d1 < hi)
        in2 = (d2 >= lo) & (d2 < hi)
        wsel = jnp.where(in1, w_ref[0], 0.0) + jnp.where(in2, w_ref[1], 0.0)
        hit = jnp.where(in1, d1, jnp.where(in2, d2, -1)) - first
        hit = jnp.concatenate([hit] * (COMBINE_ROWS // 128), axis=1)
        col = lax.broadcasted_iota(jnp.int32, (ROUTE_TILE, COMBINE_ROWS), 1)
        picked = _dot(jnp.where(hit == col, 1.0, 0.0).astype(BF16), rows)
        return jnp.concatenate([wsel] * (D_MODEL // 128), axis=1) * picked

    for e in range(N_EXPERTS):
        win_copy(s, e, half).wait()
    acc = jnp.zeros((ROUTE_TILE, D_MODEL), F32)
    for e in range(N_EXPERTS):
        i = s * N_EXPERTS + e
        first = r_ref[i]
        acc = acc + weighted(first, lo_ref[i], jnp.minimum(hi_ref[i], first + COMBINE_ROWS), buf[half, e])
    acc_ref[...] = acc

    def overflow(j, carry):
        start = pl.multiple_of(xr_ref[j], 16)
        copy = pltpu.make_async_copy(y_hbm.at[pl.ds(start, COMBINE_ROWS)], xbuf, xsem.at[0])
        copy.start()
        copy.wait()
        acc_ref[...] = acc_ref[...] + weighted(xr_ref[j], xl_ref[j], xh_ref[j], xbuf[...])
        return carry

    lax.fori_loop(es_ref[s], es_ref[s + 1], overflow, 0)
    o_ref[...] = _layer_norm(DEEPNORM_ALPHA * x1_ref[...] + gm_ref[...] * acc_ref[...], lg_ref[...], lb_ref[...])


def _combine(lists, y, dest, route, x1, mod3, ln_g, ln_b, layer, n_lat, seq, batch):
    n = x1.shape[0]
    nl = n_lat // ROUTE_TILE
    tpb = seq // ROUTE_TILE

    def row(s):
        return jnp.where(s < nl, s // tpb, batch)

    spec = pltpu.PrefetchScalarGridSpec(
        num_scalar_prefetch=7,
        grid=(n // ROUTE_TILE,),
        in_specs=[
            pl.BlockSpec((ROUTE_TILE, 2), lambda s, *_: (s, 0)),
            pl.BlockSpec((ROUTE_TILE, ROUTE_LANES), lambda s, *_: (s, 0)),
            pl.BlockSpec((ROUTE_TILE, D_MODEL), lambda s, *_: (s, 0)),
            pl.BlockSpec((None, 1, D_MODEL), lambda s, *_: (row(s), 0, 5)),
            pl.BlockSpec((None, 1, D_MODEL), lambda s, *_: (layer, 0, 0)),
            pl.BlockSpec((None, 1, D_MODEL), lambda s, *_: (layer, 0, 0)),
            pl.BlockSpec(memory_space=pl.ANY),
        ],
        out_specs=pl.BlockSpec((ROUTE_TILE, D_MODEL), lambda s, *_: (s, 0)),
        scratch_shapes=[
            pltpu.VMEM((2, N_EXPERTS, COMBINE_ROWS, D_MODEL), BF16),
            pltpu.SemaphoreType.DMA((2, N_EXPERTS)),
            pltpu.VMEM((COMBINE_ROWS, D_MODEL), BF16),
            pltpu.SemaphoreType.DMA((1,)),
            pltpu.VMEM((ROUTE_TILE, D_MODEL), F32),
            pltpu.VMEM((2, ROUTE_TILE, 128), jnp.int32),
            pltpu.VMEM((2, ROUTE_TILE, 128), F32),
        ],
    )
    return pl.pallas_call(
        _combine_kernel,
        grid_spec=spec,
        out_shape=jax.ShapeDtypeStruct((n, D_MODEL), F32),
        compiler_params=_params(("arbitrary",)),
        name="moe_combine",
    )(*lists, dest, route, x1, mod3, ln_g, ln_b, y)


def _moe(t, x1, mod3, wr, tri, w1, w3, w2, ln_g, ln_b, layer, j, n_lat, seq, batch):
    n = t.shape[0]
    n_tiles = n // ROUTE_TILE
    n_blocks = -(-(2 * n + N_EXPERTS * (EXPERT_ROWS - 1)) // EXPERT_ROWS) + 1
    i32 = jnp.int32

    route, before, total = _router(t, wr, tri)

    e12 = route[:, 0:2].astype(i32)
    rank = route[:, 4:6].astype(i32)
    counts = total[0, :N_EXPERTS].astype(i32)
    padded = (counts + EXPERT_ROWS - 1) // EXPERT_ROWS * EXPERT_ROWS
    pad_end = jnp.cumsum(padded)
    pad_start = pad_end - padded
    dest = pad_start[e12] + rank
    dst = dest.reshape(n_tiles, ROUTE_TILE, 2).transpose(0, 2, 1)
    block_start = jnp.arange(n_blocks, dtype=i32) * EXPERT_ROWS
    block_e = jnp.minimum(jnp.sum(block_start[:, None] >= pad_end[None, :], axis=1), N_EXPERTS - 1).astype(i32)
    block_used = (block_start < pad_end[-1]).astype(i32)

    cb = before[:, 0, :N_EXPERTS].astype(i32)
    ca = jnp.concatenate([cb[1:], counts[None, :]], axis=0)
    lo = pad_start[None, :] + cb
    hi = pad_start[None, :] + ca
    some = hi > lo
    tile_id = jnp.arange(n_tiles, dtype=i32)[:, None, None]

    def grouped(ok, group, n_groups, *values):
        key = jnp.where(ok, group, n_groups).reshape(-1)
        order = jnp.argsort(key, stable=True)
        starts = jnp.sum(key[None, :] < jnp.arange(n_groups + 1, dtype=i32)[:, None], axis=1, dtype=i32)
        return (starts,) + tuple(jnp.broadcast_to(v, ok.shape).reshape(-1)[order].astype(i32) for v in values)

    last_off = EXPERT_ROWS - GATHER_ROWS
    b_lo = lo // EXPERT_ROWS
    parts = []
    for blk, p_lo, p_hi in ((b_lo, lo, jnp.minimum(hi, (b_lo + 1) * EXPERT_ROWS)),
                            (b_lo + 1, (b_lo + 1) * EXPERT_ROWS, hi)):
        l_lo = p_lo - blk * EXPERT_ROWS
        l_hi = p_hi - blk * EXPERT_ROWS
        a0 = jnp.minimum(l_lo // 16 * 16, last_off)
        cut = a0 + GATHER_ROWS
        a1 = jnp.minimum(cut, last_off)
        parts.append((blk, a0, l_lo, jnp.minimum(l_hi, cut)))
        parts.append((blk, a1, jnp.maximum(l_lo, cut), l_hi))
    g_block, g_off, g_lo, g_hi = (jnp.stack(v, axis=-1) for v in zip(*parts))
    g_ok = some[:, :, None] & (g_hi > g_lo)
    g_start, g_tile, g_off, g_lo, g_hi = grouped(g_ok, g_block, n_blocks, tile_id, g_off, g_lo, g_hi)
    xs = _gather((g_start, g_tile | (g_off << 16), g_lo | (g_hi << 16)), t, dst, n_blocks)

    y = _experts(block_e, block_used, xs, w1, w3, w2, j)

    r0 = lo // 16 * 16
    over = some & (hi > r0 + COMBINE_ROWS)
    x_start, x_row, x_lo, x_hi = grouped(over, tile_id[:, :, 0], n_tiles, r0 + COMBINE_ROWS, r0 + COMBINE_ROWS, hi)
    lists = (r0.reshape(-1), lo.reshape(-1), hi.reshape(-1), x_start, x_row, x_lo, x_hi)
    return _combine(lists, y, dest, route, x1, mod3, ln_g, ln_b, layer, n_lat, seq, batch)


def _window_counts(n, w):
    t = np.arange(n)
    left = w // 2
    right = w - 1 - left
    return (np.minimum(t + right + 1, n) - np.maximum(t - left, 0)).astype(np.float32)


def _window_matrix(n, w):
    left = w // 2
    right = w - 1 - left
    t = np.arange(n)
    return ((t[None, :] >= t[:, None] - left) & (t[None, :] <= t[:, None] + right)).astype(np.float32)


def _pool_tables(seq, n_ctx):
    rows = seq // GRID_W
    per_tile = 256 // GRID_W
    mc = np.stack([np.kron(np.eye(per_tile, dtype=np.float32), _window_matrix(GRID_W, w)) for w in POOL_WINDOWS])
    inv = np.stack([1.0 / np.outer(_window_counts(rows, w), _window_counts(GRID_W, w)).reshape(seq)
                    for w in POOL_WINDOWS])
    inv = np.broadcast_to(inv[:, :, None], (4, seq, POOL_GROUP)).astype(np.float32)
    m1d = np.stack([_window_matrix(n_ctx, w) for w in POOL_WINDOWS])
    inv1d = np.stack([1.0 / _window_counts(n_ctx, w) for w in POOL_WINDOWS])
    inv1d = np.broadcast_to(inv1d[:, :, None], (4, n_ctx, POOL_GROUP)).astype(np.float32)
    return (jnp.asarray(mc, BF16), jnp.asarray(inv), jnp.asarray(m1d, BF16), jnp.asarray(inv1d))


def _rope_tables(seq):
    f32 = np.float32
    t = np.arange(seq)
    row = (t // GRID_W).astype(f32)
    col = (t % GRID_W).astype(f32)
    n_freq = RET_QK_DIM // 4
    inv = np.exp(-np.log(f32(ROPE_BASE)) * np.arange(n_freq, dtype=f32) / f32(n_freq)).astype(f32)
    ang = np.concatenate([row[:, None] * inv, col[:, None] * inv], -1).astype(f32)
    cos, sin = np.cos(ang).astype(f32), np.sin(ang).astype(f32)
    cos2 = np.concatenate([cos, cos], -1)
    sin2 = np.concatenate([-sin, sin], -1)
    cos2 = np.concatenate([cos2, np.ones((TOKEN_TILE, RET_QK_DIM), f32)], 0)
    sin2 = np.concatenate([sin2, np.zeros((TOKEN_TILE, RET_QK_DIM), f32)], 0)
    return jnp.asarray(cos2), jnp.asarray(sin2)


def kernel(x, c, ctx, c_ctx, ada_w, ada_b, w_in, pool_w, pool_scale, w_pool_out, w_ret_out, ret_decay_logit,
           w_out, ln_mix_g, ln_mix_b, ln_ffn_g, ln_ffn_b, ffn_w1, ffn_w3, ffn_w2, moe_router, moe_w1, moe_w3,
           moe_w2):
    batch, seq, d = x.shape
    n_ctx = ctx.shape[1]
    n_lat = batch * seq
    depth = ada_w.shape[0]
    assert d == D_MODEL and depth == DEPTH and batch < MOD_ROWS
    assert seq % TOKEN_TILE == 0 and (batch * n_ctx) % TOKEN_TILE == 0 and n_lat % n_ctx == 0
    assert FFN_ROWS % ROUTER_STEP == 0 and ROUTER_STEP % ROUTE_TILE == 0
    assert seq % FFN_ROWS == 0 and (batch * n_ctx) % FFN_ROWS == 0

    s_in = jnp.zeros((MOD_ROWS, d), F32).at[:batch].set(c).at[batch].set(c_ctx)
    mod = _modulation(s_in, ada_w, ada_b)
    cos2, sin2 = _rope_tables(seq)
    mc, inv, m1d, inv1d = _pool_tables(seq, n_ctx)
    tri = jnp.asarray(np.tril(np.ones((ROUTER_STEP, ROUTER_STEP), np.float32), -1), BF16)

    w_in_b = w_in.astype(BF16)
    pool_w_b = pool_w.astype(BF16)
    pool_s = pool_scale.reshape(depth, 1, POOL_WIDTH)
    wp_b, wr_b, wo_b = w_pool_out.astype(BF16), w_ret_out.astype(BF16), w_out.astype(BF16)
    ffn_b = ffn_w1.astype(BF16), ffn_w3.astype(BF16), ffn_w2.astype(BF16)
    moe_b = moe_w1.astype(BF16), moe_w3.astype(BF16), moe_w2.astype(BF16)
    ln_mix = ln_mix_g.reshape(depth, 1, d), ln_mix_b.reshape(depth, 1, d)
    ln_ffn = ln_ffn_g.reshape(depth, 1, d), ln_ffn_b.reshape(depth, 1, d)

    n_tok = n_lat + batch * n_ctx
    x_parts = (x.reshape(n_lat, d), ctx.reshape(batch * n_ctx, d))
    for l in range(depth):
        last = l == depth - 1
        mod3 = mod[l].reshape(MOD_ROWS, 1, 6 * d)
        pa, pb = _in_proj(x_parts, mod3, cos2, sin2, w_in_b, l, n_tok, n_lat, seq, batch)
        yp_lat = _pool_lat(pa, mc, inv, pool_w_b, pool_s, l, batch, seq)
        yp_ctx = _pool_ctx(pa, m1d, inv1d, pool_w_b, pool_s, l, batch, n_lat, n_ctx)
        dl = jnp.broadcast_to(ret_decay_logit[l].astype(F32)[:, :, None, None],
                              (2, RET_HEADS, RET_CHUNK, RET_V_DIM))
        z_lat, z_ctx = _retention(pa, pb, dl, batch, seq, n_ctx)
        n_out = n_lat if last else n_tok
        x1, t = _merge(x_parts, mod3, yp_lat, yp_ctx, z_lat, z_ctx, pb, wp_b, wr_b, wo_b,
                       *ln_mix, l, n_out, n_lat, seq, batch)
        j = l // 2
        if l % 2 == 0:
            xs = _ffn(t, x1, mod3, *ffn_b, *ln_ffn, l, j, n_lat, seq, batch)
        else:
            wr = jnp.zeros((d, ROUTE_LANES), BF16).at[:, :N_EXPERTS].set(moe_router[j].astype(BF16))
            xs = _moe(t, x1, mod3, wr, tri, *moe_b, *ln_ffn, l, j, n_lat, seq, batch)
        x_parts = (xs,)
    return xs[:n_lat].reshape(batch, seq, d)
```
